```python
import jax, jax.numpy as jnp
from jax import lax
import numpy as np

D_MODEL = 2048
BATCH = 4
SEQ = 4096
DEPTH = 4

GRID_W = 64
CTX_LEN = 256
N_MIXERS = 2
N_MOD = 6
NORM_EPS = 1e-6

MLA_HEADS = 16
MLA_Q_RANK = 768
MLA_KV_RANK = 256
MLA_NOPE_DIM = 128
MLA_ROPE_DIM = 64
MLA_V_DIM = 128
MLA_QK_DIM = MLA_NOPE_DIM + MLA_ROPE_DIM
Q_BLOCK = 128
ROPE_THETA = 10000.0

GLA_HEADS = 4
GLA_KEY_DIM = D_MODEL // 2
GLA_VALUE_DIM = D_MODEL
GLA_DK = GLA_KEY_DIM // GLA_HEADS
GLA_DV = GLA_VALUE_DIM // GLA_HEADS
GLA_GATE_RANK = 16
GLA_TAU = 16.0
GLA_CHUNK = 64

MOE_GROUPS = 4
MOE_PER_GROUP = 8
MOE_EXPERTS = MOE_GROUPS * MOE_PER_GROUP
MOE_TOP_K = 2
MOE_D_FF = 512
MOE_BLOCK = 256

N_MLA_LAYERS = (DEPTH + N_MIXERS - 1) // N_MIXERS
N_GLA_LAYERS = DEPTH // N_MIXERS

kernel_name = 'hybrid_mla_gla_hmoe_dit'


def layer_norm(x, g, b):
    xf = x.astype(jnp.float32)
    mu = jnp.mean(xf, axis=-1, keepdims=True)
    var = jnp.mean(jnp.square(xf - mu), axis=-1, keepdims=True)
    return ((xf - mu) * lax.rsqrt(var + NORM_EPS) * g + b).astype(x.dtype)


def rms_norm(x, g):
    xf = x.astype(jnp.float32)
    return (xf * lax.rsqrt(jnp.mean(jnp.square(xf), axis=-1, keepdims=True) + NORM_EPS) * g).astype(x.dtype)


def modulate(x, shift, scale):
    return x * (1 + scale) + shift


def adaln(cond, w_mod, b_mod):
    m = jax.nn.silu(cond) @ w_mod + b_mod
    return jnp.split(m[..., None, :], N_MOD, axis=-1)


def split_last(t, sizes):
    return jnp.split(t, np.cumsum(sizes)[:-1].tolist(), axis=-1)


def axial_rope_tables(n_tokens):
    n_rows = n_tokens // GRID_W
    row = jnp.repeat(jnp.arange(n_rows, dtype=jnp.float32), GRID_W)
    col = jnp.tile(jnp.arange(GRID_W, dtype=jnp.float32), n_rows)
    n_freq = MLA_ROPE_DIM // 4
    inv_freq = jnp.power(ROPE_THETA, -jnp.arange(n_freq, dtype=jnp.float32) / n_freq)
    ang = jnp.concatenate([row[:, None] * inv_freq, col[:, None] * inv_freq], axis=-1)
    return jnp.cos(ang), jnp.sin(ang)


def apply_rope(t, cos, sin):
    t1, t2 = jnp.split(t.astype(jnp.float32), 2, axis=-1)
    return jnp.concatenate([t1 * cos - t2 * sin, t1 * sin + t2 * cos], axis=-1).astype(t.dtype)


def mla_project(h, rope, w_in, q_norm_g, w_uq, kv_norm_g, w_ukv):
    B, L, _ = h.shape
    cq, ckv, k_pe = split_last(h @ w_in, [MLA_Q_RANK, MLA_KV_RANK, MLA_ROPE_DIM])
    q = (rms_norm(cq, q_norm_g) @ w_uq).reshape(B, L, MLA_HEADS, MLA_QK_DIM)
    kv = (rms_norm(ckv, kv_norm_g) @ w_ukv).reshape(B, L, MLA_HEADS, MLA_NOPE_DIM + MLA_V_DIM)
    q_nope, q_pe = split_last(q, [MLA_NOPE_DIM, MLA_ROPE_DIM])
    k_nope, v = split_last(kv, [MLA_NOPE_DIM, MLA_V_DIM])
    if rope is not None:
        cos, sin = rope
        q_pe = apply_rope(q_pe, cos[None, :, None], sin[None, :, None])
        k_pe = apply_rope(k_pe, cos[None], sin[None])
    k_pe = jnp.broadcast_to(k_pe[:, :, None, :], (B, L, MLA_HEADS, MLA_ROPE_DIM))
    return (jnp.concatenate([q_nope, q_pe], axis=-1),
            jnp.concatenate([k_nope, k_pe], axis=-1), v)


def block_attention(q, k, v):
    B, Lq, H, dqk = q.shape
    dv = v.shape[-1]
    nb = Lq // Q_BLOCK
    qb = jnp.moveaxis(q.reshape(B, nb, Q_BLOCK, H, dqk), 1, 0)
    scale = dqk ** -0.5

    def one_block(q_blk):
        s = jnp.einsum('bqhd,bkhd->bhqk', q_blk, k, preferred_element_type=jnp.float32) * scale
        p = jax.nn.softmax(s, axis=-1).astype(v.dtype)
        return jnp.einsum('bhqk,bkhd->bqhd', p, v)

    o = lax.map(one_block, qb)
    return jnp.moveaxis(o, 0, 1).reshape(B, Lq, H * dv)


def mla_mixer(h_ctx, h_lat, rope, ctx_out, w_in, q_norm_g, w_uq, kv_norm_g, w_ukv, w_o):
    q_c, k_c, v_c = mla_project(h_ctx, None, w_in, q_norm_g, w_uq, kv_norm_g, w_ukv)
    q_l, k_l, v_l = mla_project(h_lat, rope, w_in, q_norm_g, w_uq, kv_norm_g, w_ukv)
    o_l = block_attention(q_l, jnp.concatenate([k_c, k_l], axis=1), jnp.concatenate([v_c, v_l], axis=1))
    y_ctx = block_attention(q_c, k_c, v_c) @ w_o if ctx_out else None
    return y_ctx, o_l @ w_o


def gla_chunk_scan(q, k, v, g, s0):
    B, H, L, DK = q.shape
    DV = v.shape[-1]
    n = L // GLA_CHUNK

    def chunks(t):
        return jnp.moveaxis(t.reshape(B, H, n, GLA_CHUNK, t.shape[-1]), 2, 0)

    mask = jnp.tril(jnp.ones((GLA_CHUNK, GLA_CHUNK), dtype=bool))[:, :, None]

    def step(S, inp):
        qc, kc, vc, gc = inp
        qf, kf, vf = qc.astype(jnp.float32), kc.astype(jnp.float32), vc.astype(jnp.float32)
        b = jnp.cumsum(gc.astype(jnp.float32), axis=2)
        o_inter = jnp.einsum('bhcd,bhde->bhce', qf * jnp.exp(b), S)
        rel = jnp.where(mask, b[:, :, :, None, :] - b[:, :, None, :, :], -jnp.inf)
        A = jnp.einsum('bhid,bhjd,bhijd->bhij', qf, kf, jnp.exp(rel))
        o_intra = jnp.einsum('bhij,bhje->bhie', A, vf)
        b_last = b[:, :, -1:, :]
        S_new = jnp.exp(b_last[:, :, 0, :])[..., None] * S + jnp.einsum('bhcd,bhce->bhde', kf * jnp.exp(b_last - b), vf)
        return S_new, o_inter + o_intra

    S_fin, o = lax.scan(step, s0, (chunks(q), chunks(k), chunks(v), chunks(g)))
    return jnp.moveaxis(o, 0, 2).reshape(B, H, L, DV), S_fin


def gla_project(h, w_in, gate_a, gate_b, gate_bias):
    B, L, _ = h.shape
    q, k, v, r = split_last(h @ w_in, [GLA_KEY_DIM, GLA_KEY_DIM, GLA_VALUE_DIM, GLA_VALUE_DIM])

    def heads(t):
        return jnp.transpose(t.reshape(B, L, GLA_HEADS, -1), (0, 2, 1, 3))

    g_fwd, g_bwd = [heads(jax.nn.log_sigmoid(((h @ gate_a[d]) @ gate_b[d] + gate_bias[d]).astype(jnp.float32)) / GLA_TAU)
                    for d in range(2)]
    return heads(q) * GLA_DK ** -0.5, heads(k), heads(v), r, g_fwd, g_bwd


def gla_bidir(q, k, v, g_fwd, g_bwd, s_fwd, s_bwd):
    flip = lambda t: jnp.flip(t, axis=2)
    o_f, s_f = gla_chunk_scan(q, k, v, g_fwd, s_fwd)
    o_b, s_b = gla_chunk_scan(flip(q), flip(k), flip(v), flip(g_bwd), s_bwd)
    return o_f + flip(o_b), s_f, s_b


def gla_mixer(h_ctx, h_lat, ctx_out, w_in, gate_a, gate_b, gate_bias, norm_g, w_o):
    B = h_lat.shape[0]
    q_c, k_c, v_c, r_c, gf_c, gb_c = gla_project(h_ctx, w_in, gate_a, gate_b, gate_bias)
    q_l, k_l, v_l, r_l, gf_l, gb_l = gla_project(h_lat, w_in, gate_a, gate_b, gate_bias)
    s0 = jnp.zeros((B, GLA_HEADS, GLA_DK, GLA_DV), jnp.float32)
    o_c, s_f, s_b = gla_bidir(q_c, k_c, v_c, gf_c, gb_c, s0, s0)
    o_l, _, _ = gla_bidir(q_l, k_l, v_l, gf_l, gb_l, s_f, s_b)

    def out(o, r):
        L = o.shape[2]
        o = jnp.transpose(rms_norm(o, norm_g), (0, 2, 1, 3)).reshape(B, L, GLA_VALUE_DIM).astype(r.dtype)
        return (o * jax.nn.silu(r)) @ w_o

    return (out(o_c, r_c) if ctx_out else None), out(o_l, r_l)


def hier_moe(h, w_grp, b_grp, w_exp, b_exp, w1, w3, w2):
    N, D = h.shape
    grp_logits = (h @ w_grp).astype(jnp.float32) + b_grp
    grp_p = jax.nn.softmax(grp_logits, axis=-1)
    _, g_top = lax.top_k(grp_logits, 1)
    p_grp = jnp.take_along_axis(grp_p, g_top, axis=-1)
    exp_logits = ((h @ w_exp).astype(jnp.float32) + b_exp).reshape(N, MOE_GROUPS, MOE_PER_GROUP)
    in_grp = jnp.take_along_axis(exp_logits, g_top[:, :, None], axis=1)[:, 0]
    top_v, top_i = lax.top_k(jax.nn.softmax(in_grp, axis=-1), MOE_TOP_K)
    weight = p_grp * top_v / jnp.sum(top_v, axis=-1, keepdims=True)
    expert_id = g_top * MOE_PER_GROUP + top_i

    A = N * MOE_TOP_K
    e_flat = expert_id.reshape(A)
    tok_flat = jnp.repeat(jnp.arange(N, dtype=jnp.int32), MOE_TOP_K)
    w_flat = weight.reshape(A)
    order = jnp.argsort(e_flat)
    e_sorted = e_flat[order]
    counts = jnp.zeros((MOE_EXPERTS,), jnp.int32).at[e_flat].add(1)
    starts = jnp.cumsum(counts) - counts
    padded = (counts + MOE_BLOCK - 1) // MOE_BLOCK * MOE_BLOCK
    pad_end = jnp.cumsum(padded)
    pad_start = pad_end - padded
    dest = pad_start[e_sorted] + (jnp.arange(A, dtype=jnp.int32) - starts[e_sorted])
    n_blocks = -(-A // MOE_BLOCK) + MOE_EXPERTS
    R = n_blocks * MOE_BLOCK
    slot_tok = jnp.full((R,), N, jnp.int32).at[dest].set(tok_flat[order])
    slot_w = jnp.zeros((R,), jnp.float32).at[dest].set(w_flat[order])
    block_exp = jnp.minimum(jnp.searchsorted(pad_end, jnp.arange(n_blocks, dtype=jnp.int32) * MOE_BLOCK, side='right'),
                            MOE_EXPERTS - 1)
    h_pad = jnp.concatenate([h, jnp.zeros((1, D), h.dtype)], axis=0)

    def run_block(args):
        toks, ws, e = args
        xb = h_pad[toks]
        y = (jax.nn.silu(xb @ w1[e]) * (xb @ w3[e])) @ w2[e]
        return y * ws[:, None]

    out = lax.map(run_block, (slot_tok.reshape(n_blocks, MOE_BLOCK), slot_w.reshape(n_blocks, MOE_BLOCK), block_exp))
    y = jax.ops.segment_sum(out.reshape(R, D), slot_tok, num_segments=N + 1)[:N]
    return y.astype(h.dtype)


def setup_inputs(seed: int = 0) -> dict:
    key = jax.random.key(seed)
    keys = jax.random.split(key, 32)
    counter = [0]

    def nrm(shape, scale):
        k = keys[counter[0]]
        counter[0] += 1
        return jax.random.normal(k, shape, jnp.float32) * scale

    D = D_MODEL
    beta = (8.0 * DEPTH) ** -0.25
    return {
        'x': nrm((BATCH, SEQ, D), 1.0),
        'c': nrm((BATCH, D), 1.0),
        'ctx': nrm((BATCH, CTX_LEN, D), 1.0),
        'c_ctx': nrm((D,), 1.0),
        'w_mod': nrm((DEPTH, D, N_MOD * D), 0.5 * D ** -0.5),
        'b_mod': nrm((DEPTH, N_MOD * D), 0.02),
        'ln1_g': 1.0 + nrm((DEPTH, D), 0.02),
        'ln1_b': nrm((DEPTH, D), 0.02),
        'ln2_g': 1.0 + nrm((DEPTH, D), 0.02),
        'ln2_b': nrm((DEPTH, D), 0.02),
        'mla_w_in': nrm((N_MLA_LAYERS, D, MLA_Q_RANK + MLA_KV_RANK + MLA_ROPE_DIM), D ** -0.5),
        'mla_q_norm': 1.0 + nrm((N_MLA_LAYERS, MLA_Q_RANK), 0.02),
        'mla_w_uq': nrm((N_MLA_LAYERS, MLA_Q_RANK, MLA_HEADS * MLA_QK_DIM), MLA_Q_RANK ** -0.5),
        'mla_kv_norm': 1.0 + nrm((N_MLA_LAYERS, MLA_KV_RANK), 0.02),
        'mla_w_ukv': nrm((N_MLA_LAYERS, MLA_KV_RANK, MLA_HEADS * (MLA_NOPE_DIM + MLA_V_DIM)), MLA_KV_RANK ** -0.5),
        'mla_w_o': nrm((N_MLA_LAYERS, MLA_HEADS * MLA_V_DIM, D), beta * (MLA_HEADS * MLA_V_DIM) ** -0.5),
        'gla_w_in': nrm((N_GLA_LAYERS, D, 2 * GLA_KEY_DIM + 2 * GLA_VALUE_DIM), D ** -0.5),
        'gla_gate_a': nrm((N_GLA_LAYERS, 2, D, GLA_GATE_RANK), D ** -0.5),
        'gla_gate_b': nrm((N_GLA_LAYERS, 2, GLA_GATE_RANK, GLA_KEY_DIM), GLA_GATE_RANK ** -0.5),
        'gla_gate_bias': nrm((N_GLA_LAYERS, 2, GLA_KEY_DIM), 0.1),
        'gla_norm': 1.0 + nrm((N_GLA_LAYERS, GLA_DV), 0.02),
        'gla_w_o': nrm((N_GLA_LAYERS, GLA_VALUE_DIM, D), beta * GLA_VALUE_DIM ** -0.5),
        'moe_w_grp': nrm((DEPTH, D, MOE_GROUPS), D ** -0.5),
        'moe_b_grp': nrm((DEPTH, MOE_GROUPS), 0.01),
        'moe_w_exp': nrm((DEPTH, D, MOE_EXPERTS), D ** -0.5),
        'moe_b_exp': nrm((DEPTH, MOE_EXPERTS), 0.01),
        'moe_w1': nrm((DEPTH, MOE_EXPERTS, D, MOE_D_FF), D ** -0.5),
        'moe_w3': nrm((DEPTH, MOE_EXPERTS, D, MOE_D_FF), D ** -0.5),
        'moe_w2': nrm((DEPTH, MOE_EXPERTS, MOE_D_FF, D), beta * MOE_D_FF ** -0.5),
    }


def reference(x, c, ctx, c_ctx, w_mod, b_mod, ln1_g, ln1_b, ln2_g, ln2_b,
              mla_w_in, mla_q_norm, mla_w_uq, mla_kv_norm, mla_w_ukv, mla_w_o,
              gla_w_in, gla_gate_a, gla_gate_b, gla_gate_bias, gla_norm, gla_w_o,
              moe_w_grp, moe_b_grp, moe_w_exp, moe_b_exp, moe_w1, moe_w3, moe_w2):
    B, L, D = x.shape
    Lc = ctx.shape[1]
    alpha = (2.0 * DEPTH) ** 0.25
    rope = axial_rope_tables(L)
    x_lat, x_ctx = x, ctx
    for i in range(DEPTH):
        last = i == DEPTH - 1
        sh_l1, sc_l1, g_l1, sh_l2, sc_l2, g_l2 = adaln(c, w_mod[i], b_mod[i])
        sh_c1, sc_c1, g_c1, sh_c2, sc_c2, g_c2 = adaln(c_ctx, w_mod[i], b_mod[i])
        h_ctx = modulate(x_ctx, sh_c1, sc_c1)
        h_lat = modulate(x_lat, sh_l1, sc_l1)
        j = i // N_MIXERS
        if i % N_MIXERS == 0:
            y_ctx, y_lat = mla_mixer(h_ctx, h_lat, rope, not last, mla_w_in[j], mla_q_norm[j], mla_w_uq[j],
                                     mla_kv_norm[j], mla_w_ukv[j], mla_w_o[j])
        else:
            y_ctx, y_lat = gla_mixer(h_ctx, h_lat, not last, gla_w_in[j], gla_gate_a[j], gla_gate_b[j],
                                     gla_gate_bias[j], gla_norm[j], gla_w_o[j])
        x_lat = layer_norm(alpha * x_lat + g_l1 * y_lat, ln1_g[i], ln1_b[i])
        h_lat = modulate(x_lat, sh_l2, sc_l2)
        moe_args = (moe_w_grp[i], moe_b_grp[i], moe_w_exp[i], moe_b_exp[i], moe_w1[i], moe_w3[i], moe_w2[i])
        if last:
            y_lat = hier_moe(h_lat.reshape(B * L, D), *moe_args).reshape(B, L, D)
        else:
            x_ctx = layer_norm(alpha * x_ctx + g_c1 * y_ctx, ln1_g[i], ln1_b[i])
            h_ctx = modulate(x_ctx, sh_c2, sc_c2)
            y = hier_moe(jnp.concatenate([h_ctx.reshape(B * Lc, D), h_lat.reshape(B * L, D)], axis=0), *moe_args)
            y_ctx = y[:B * Lc].reshape(B, Lc, D)
            y_lat = y[B * Lc:].reshape(B, L, D)
            x_ctx = layer_norm(alpha * x_ctx + g_c2 * y_ctx, ln2_g[i], ln2_b[i])
        x_lat = layer_norm(alpha * x_lat + g_l2 * y_lat, ln2_g[i], ln2_b[i])
    return x_lat
```

```python
import functools
import math

import jax
import jax.numpy as jnp
import numpy as np
from jax import lax
from jax.experimental import pallas as pl
from jax.experimental.pallas import tpu as pltpu

F32 = jnp.float32
BF16 = jnp.bfloat16

GRID_W = 64
N_MOD = 6
NORM_EPS = 1e-6
MLA_HEADS = 16
MLA_Q_RANK = 768
MLA_KV_RANK = 256
MLA_NOPE = 128
MLA_ROPE = 64
MLA_V = 128
ROPE_THETA = 10000.0
GLA_HEADS = 4
GLA_GATE_RANK = 16
GLA_TAU = 16.0
GLA_CHUNK = 64
GLA_SUB = 16
MOE_GROUPS = 4
MOE_PER_GROUP = 8
MOE_EXPERTS = MOE_GROUPS * MOE_PER_GROUP
MOE_BLOCK = 256

LANES = 128
VMEM_LIMIT = 56 * 1024 * 1024

ROW_TILE = 256
ATTN_TQ = 256
ATTN_TK = 512


def _cparams(sem):
    return pltpu.CompilerParams(dimension_semantics=sem, vmem_limit_bytes=VMEM_LIMIT)


def _sigmoid(x):
    return 1.0 / (1.0 + jnp.exp(-x))


def _layer_norm(z, g, b):
    mu = jnp.mean(z, axis=-1, keepdims=True)
    zc = z - mu
    var = jnp.mean(zc * zc, axis=-1, keepdims=True)
    return zc * lax.rsqrt(var + NORM_EPS) * g + b


def _rms_norm(z, g):
    return z * lax.rsqrt(jnp.mean(z * z, axis=-1, keepdims=True) + NORM_EPS) * g


def _mods_kernel(cond_ref, w_ref, b_ref, o_ref):
    c = cond_ref[...]
    s = c * _sigmoid(c)
    o_ref[...] = jnp.dot(s, w_ref[...], preferred_element_type=F32,
                         precision=lax.Precision.HIGHEST) + b_ref[...]


def _adaln_tables(cond, w_mod, b_mod):
    depth, d, n6 = w_mod.shape
    tn = 1024
    return pl.pallas_call(
        _mods_kernel,
        grid=(depth, n6 // tn),
        in_specs=[
            pl.BlockSpec((8, d), lambda l, j: (0, 0)),
            pl.BlockSpec((None, d, tn), lambda l, j: (l, 0, j)),
            pl.BlockSpec((None, 1, tn), lambda l, j: (l, 0, j)),
        ],
        out_specs=pl.BlockSpec((None, 8, tn), lambda l, j: (l, 0, j)),
        out_shape=jax.ShapeDtypeStruct((depth, 8, n6), F32),
        compiler_params=_cparams(("parallel", "parallel")),
        name="adaln_tables",
    )(cond, w_mod, b_mod.reshape(depth, 1, n6))


class _Stream:
    def __init__(self, batch, seq, ctx_len, d):
        self.batch, self.seq, self.ctx_len, self.d = batch, seq, ctx_len, d
        self.n_lat = batch * seq
        self.n_ctx = batch * ctx_len
        self.n = self.n_lat + self.n_ctx
        assert seq % ROW_TILE == 0 and ctx_len % ROW_TILE == 0
        self.lat_tiles = self.n_lat // ROW_TILE
        self.tiles = self.n // ROW_TILE
        self.tiles_per_batch = seq // ROW_TILE

    def mod_row(self, t):
        return jnp.where(t < self.lat_tiles, t // self.tiles_per_batch, self.batch)


def _mod_spec(st, chunk):
    d = st.d
    return pl.BlockSpec((None, 1, d), lambda t: (st.mod_row(t), 0, chunk))


def _row_spec(width, tm=ROW_TILE):
    return pl.BlockSpec((tm, width), lambda t: (t, 0))


def _const_spec(shape):
    nd = len(shape)
    return pl.BlockSpec(shape, lambda t: (0,) * nd, pipeline_mode=pl.Buffered(1))


def _half_sum(x):
    return x + pltpu.roll(x, LANES // 2, axis=1)


def _mla_proj_kernel(x_ref, sh_ref, sc_ref, cs_ref, w_in_ref, qg_ref, kvg_ref, w_q_ref, w_kv_ref,
                     q_ref, k_ref, v_ref, *, scale):
    h = x_ref[...] * (1.0 + sc_ref[...]) + sh_ref[...]
    lat = jnp.dot(h.astype(BF16), w_in_ref[...], preferred_element_type=F32)
    cq = _rms_norm(lat[:, :MLA_Q_RANK], qg_ref[...]).astype(BF16)
    ckv = _rms_norm(lat[:, MLA_Q_RANK:MLA_Q_RANK + MLA_KV_RANK], kvg_ref[...]).astype(BF16)
    cs = cs_ref[...]
    lane = lax.broadcasted_iota(jnp.int32, cs.shape, 1)
    kpe = _half_sum(lat[:, MLA_Q_RANK + MLA_KV_RANK:] * cs)
    kpe = jnp.where(lane < MLA_ROPE, kpe, 0.0).astype(BF16)
    qall = jnp.dot(cq, w_q_ref[...], preferred_element_type=F32)
    kvall = jnp.dot(ckv, w_kv_ref[...], preferred_element_type=F32)
    hn = MLA_HEADS * MLA_NOPE
    for hd in range(MLA_HEADS):
        lo = hd * LANES
        q_ref[hd, :, :MLA_NOPE] = (qall[:, lo:lo + LANES] * scale).astype(BF16)
        q_ref[hd, :, MLA_NOPE:] = (_half_sum(qall[:, hn + lo:hn + lo + LANES] * cs) * scale).astype(BF16)
        k_ref[hd, :, :MLA_NOPE] = kvall[:, lo:lo + LANES].astype(BF16)
        k_ref[hd, :, MLA_NOPE:] = kpe
        v_ref[hd] = kvall[:, hn + lo:hn + lo + LANES].astype(BF16)


def _mla_proj(st, x, mods, cs_tab, w_in_ext, q_norm, kv_norm, w_q, w_kv):
    n, d = st.n, st.d
    tm = ROW_TILE
    lat_w = w_in_ext.shape[1]
    qk_w = MLA_NOPE + LANES
    scale = (MLA_NOPE + MLA_ROPE) ** -0.5
    seq_tiles = st.tiles_per_batch

    def cs_map(t):
        return (jnp.where(t < st.lat_tiles, t % seq_tiles, seq_tiles), 0)

    head_spec = lambda w: pl.BlockSpec((MLA_HEADS, tm, w), lambda t: (0, t, 0))
    return pl.pallas_call(
        functools.partial(_mla_proj_kernel, scale=scale),
        grid=(st.tiles,),
        in_specs=[
            _row_spec(d), _mod_spec(st, 0), _mod_spec(st, 1),
            pl.BlockSpec((tm, LANES), cs_map),
            _const_spec((d, lat_w)), _const_spec((1, MLA_Q_RANK)), _const_spec((1, MLA_KV_RANK)),
            _const_spec(w_q.shape), _const_spec(w_kv.shape),
        ],
        out_specs=[head_spec(qk_w), head_spec(qk_w), head_spec(MLA_V)],
        out_shape=[
            jax.ShapeDtypeStruct((MLA_HEADS, n, qk_w), BF16),
            jax.ShapeDtypeStruct((MLA_HEADS, n, qk_w), BF16),
            jax.ShapeDtypeStruct((MLA_HEADS, n, MLA_V), BF16),
        ],
        compiler_params=_cparams(("parallel",)),
        name="mla_proj",
    )(x, mods, mods, cs_tab, w_in_ext, q_norm, kv_norm, w_q, w_kv)


def _attn_kernel(q_ref, kc_ref, vc_ref, *rest, n_chunks):
    if n_chunks:
        kl_ref, vl_ref, o_ref, m_ref, l_ref, acc_ref = rest
    else:
        o_ref, m_ref, l_ref, acc_ref = rest
    q = q_ref[...]
    nt = (((1,), (1,)), ((), ()))
    s = lax.dot_general(q, kc_ref[...], nt, preferred_element_type=F32)
    m0 = jnp.max(s, axis=-1, keepdims=True)
    p = jnp.exp(s - m0)
    m_ref[...] = m0
    l_ref[...] = jnp.sum(p, axis=-1, keepdims=True)
    acc_ref[...] = jnp.dot(p.astype(BF16), vc_ref[...], preferred_element_type=F32)

    def body(j, carry):
        off = pl.multiple_of(j * ATTN_TK, ATTN_TK)
        s = lax.dot_general(q, kl_ref[pl.ds(off, ATTN_TK), :], nt, preferred_element_type=F32)
        m_old = m_ref[...]
        m_new = jnp.maximum(m_old, jnp.max(s, axis=-1, keepdims=True))
        alpha = jnp.exp(m_old - m_new)
        p = jnp.exp(s - m_new)
        l_ref[...] = alpha * l_ref[...] + jnp.sum(p, axis=-1, keepdims=True)
        acc_ref[...] = alpha * acc_ref[...] + jnp.dot(p.astype(BF16), vl_ref[pl.ds(off, ATTN_TK), :],
                                                      preferred_element_type=F32)
        m_ref[...] = m_new
        return carry

    if n_chunks:
        lax.fori_loop(0, n_chunks, body, 0)
    o_ref[...] = (acc_ref[...] / l_ref[...]).astype(o_ref.dtype)


def _attention(st, q, k, v, latent):
    b, seq, lc = st.batch, st.seq, st.ctx_len
    qk_w = q.shape[-1]
    tq = ATTN_TQ
    ctx_blk0 = st.n_lat // lc
    if latent:
        nq = seq // tq
        q_map = lambda bi, h, qi: (h, bi * nq + qi, 0)
        o_map = lambda bi, h, qi: (bi * nq + qi, h)
        rows = st.n_lat
        n_chunks = seq // ATTN_TK
    else:
        nq = lc // tq
        q_map = lambda bi, h, qi: (h, st.n_lat // tq + bi * nq + qi, 0)
        o_map = lambda bi, h, qi: (bi * nq + qi, h)
        rows = st.n_ctx
        n_chunks = 0
    kc_map = lambda bi, h, qi: (h, ctx_blk0 + bi, 0)
    kl_map = lambda bi, h, qi: (h, bi, 0)
    in_specs = [
        pl.BlockSpec((None, tq, qk_w), q_map),
        pl.BlockSpec((None, lc, qk_w), kc_map),
        pl.BlockSpec((None, lc, MLA_V), kc_map),
    ]
    args = [q, k, v]
    if latent:
        in_specs += [pl.BlockSpec((None, seq, qk_w), kl_map), pl.BlockSpec((None, seq, MLA_V), kl_map)]
        args += [k, v]
    return pl.pallas_call(
        functools.partial(_attn_kernel, n_chunks=n_chunks),
        grid=(b, MLA_HEADS, nq),
        in_specs=in_specs,
        out_specs=pl.BlockSpec((tq, MLA_V), o_map),
        out_shape=jax.ShapeDtypeStruct((rows, MLA_HEADS * MLA_V), BF16),
        scratch_shapes=[pltpu.VMEM((tq, 1), F32), pltpu.VMEM((tq, 1), F32), pltpu.VMEM((tq, MLA_V), F32)],
        compiler_params=_cparams(("parallel", "parallel", "arbitrary")),
        name="mla_attention_latent" if latent else "mla_attention_context",
    )(*args)


def _gla_qkg_kernel(x_ref, sh_ref, sc_ref, w_ref, gb_ref, bias_ref, q_ref, k_ref, gf_ref, gb_out_ref,
                    *, key_dim, q_scale):
    h = (x_ref[...] * (1.0 + sc_ref[...]) + sh_ref[...]).astype(BF16)
    y = jnp.dot(h, w_ref[...], preferred_element_type=F32)
    q_ref[...] = (y[:, :key_dim] * q_scale).astype(BF16)
    k_ref[...] = y[:, key_dim:2 * key_dim].astype(BF16)
    z = jnp.dot(y[:, 2 * key_dim:].astype(BF16), gb_ref[...], preferred_element_type=F32) + bias_ref[...]
    g = (jnp.minimum(z, 0.0) - jnp.log(1.0 + jnp.exp(-jnp.abs(z)))) * (1.0 / GLA_TAU)
    gf_ref[...] = g[:, :key_dim]
    gb_out_ref[...] = g[:, key_dim:]


def _gla_vr_kernel(x_ref, sh_ref, sc_ref, w_ref, v_ref, r_ref, *, value_dim):
    h = (x_ref[...] * (1.0 + sc_ref[...]) + sh_ref[...]).astype(BF16)
    y = jnp.dot(h, w_ref[...], preferred_element_type=F32)
    v_ref[...] = y[:, :value_dim].astype(BF16)
    r_ref[...] = y[:, value_dim:]


def _gla_proj(st, x, mods, w_qkg, gate_b_ext, gate_bias, w_vr):
    n, d = st.n, st.d
    key_dim = (w_qkg.shape[1] - LANES) // 2
    value_dim = w_vr.shape[1] // 2
    dk = key_dim // GLA_HEADS
    q, k, gf, gb = pl.pallas_call(
        functools.partial(_gla_qkg_kernel, key_dim=key_dim, q_scale=dk ** -0.5),
        grid=(st.tiles,),
        in_specs=[_row_spec(d), _mod_spec(st, 0), _mod_spec(st, 1), _const_spec(w_qkg.shape),
                  _const_spec(gate_b_ext.shape), _const_spec(gate_bias.shape)],
        out_specs=[_row_spec(key_dim)] * 4,
        out_shape=[jax.ShapeDtypeStruct((n, key_dim), BF16), jax.ShapeDtypeStruct((n, key_dim), BF16),
                   jax.ShapeDtypeStruct((n, key_dim), F32), jax.ShapeDtypeStruct((n, key_dim), F32)],
        compiler_params=_cparams(("parallel",)),
        name="gla_proj_qkg",
    )(x, mods, mods, w_qkg, gate_b_ext, gate_bias)
    v, r = pl.pallas_call(
        functools.partial(_gla_vr_kernel, value_dim=value_dim),
        grid=(st.tiles,),
        in_specs=[_row_spec(d), _mod_spec(st, 0), _mod_spec(st, 1), _const_spec(w_vr.shape)],
        out_specs=[_row_spec(value_dim)] * 2,
        out_shape=[jax.ShapeDtypeStruct((n, value_dim), BF16), jax.ShapeDtypeStruct((n, value_dim), F32)],
        compiler_params=_cparams(("parallel",)),
        name="gla_proj_vr",
    )(x, mods, mods, w_vr)
    return q, k, v, r, gf, gb


def _cumsum_rows(x, reverse):
    n = x.shape[0]
    row = lax.broadcasted_iota(jnp.int32, x.shape, 0)
    s = 1
    while s < n:
        if reverse:
            x = x + jnp.where(row < n - s, pltpu.roll(x, n - s, axis=0), 0.0)
        else:
            x = x + jnp.where(row >= s, pltpu.roll(x, s, axis=0), 0.0)
        s *= 2
    return x


def _gla_chunk(q, k, v, g, state_t, reverse):
    c, dk = q.shape
    nsub = c // GLA_SUB
    b = _cumsum_rows(g, reverse)
    b_last = jnp.sum(g, axis=0, keepdims=True)
    nt = (((1,), (1,)), ((), ()))
    qe = (q * jnp.exp(b)).astype(BF16)
    o = lax.dot_general(qe, state_t.astype(BF16), nt, preferred_element_type=F32)

    ri = lax.broadcasted_iota(jnp.int32, (c, LANES), 0)
    ci = lax.broadcasted_iota(jnp.int32, (c, LANES), 1)
    a_mat = jnp.zeros((c, LANES), F32)
    for sidx in range(nsub - 1):
        if reverse:
            src = sidx + 1
            ref = b[src * GLA_SUB:src * GLA_SUB + 1, :]
            q_rows = ri < src * GLA_SUB
        else:
            src = sidx
            ref = b[src * GLA_SUB + GLA_SUB - 1:src * GLA_SUB + GLA_SUB, :]
            q_rows = ri >= (src + 1) * GLA_SUB
        qd = (q * jnp.exp(jnp.minimum(b - ref, 0.0))).astype(BF16)
        kd = (k * jnp.exp(jnp.minimum(ref - b, 0.0))).astype(BF16)
        blk = lax.dot_general(qd, kd, nt, preferred_element_type=F32)
        blk = jnp.concatenate([blk, jnp.zeros((c, LANES - c), F32)], axis=1)
        k_cols = (ci >= src * GLA_SUB) & (ci < (src + 1) * GLA_SUB)
        a_mat = a_mat + jnp.where(q_rows & k_cols, blk, 0.0)
    ones = jnp.ones((dk, LANES), BF16)
    sub_r = lax.broadcasted_iota(jnp.int32, (GLA_SUB, LANES), 0)
    sub_c = lax.broadcasted_iota(jnp.int32, (GLA_SUB, LANES), 1)
    diag_rows = []
    for a in range(nsub):
        lo = a * GLA_SUB
        qa = q[lo:lo + GLA_SUB, :]
        ba = b[lo:lo + GLA_SUB, :]
        terms = []
        for jj in range(GLA_SUB):
            kj = k[lo + jj:lo + jj + 1, :]
            bj = b[lo + jj:lo + jj + 1, :]
            terms.append((qa * kj * jnp.exp(jnp.minimum(ba - bj, 0.0))).astype(BF16))
        sums = jnp.dot(jnp.concatenate(terms, axis=0), ones, preferred_element_type=F32)
        blk = jnp.zeros((GLA_SUB, LANES), F32)
        for jj in range(GLA_SUB):
            keep = (sub_c == lo + jj) & ((sub_r <= jj) if reverse else (sub_r >= jj))
            blk = blk + jnp.where(keep, sums[jj * GLA_SUB:(jj + 1) * GLA_SUB, :], 0.0)
        diag_rows.append(blk)
    a_mat = a_mat + jnp.concatenate(diag_rows, axis=0)
    o = o + jnp.dot(a_mat[:, :c].astype(BF16), v, preferred_element_type=F32)
    kd = (k * jnp.exp(b_last - b)).astype(BF16)
    tn = (((0,), (0,)), ((), ()))
    new_state = state_t * jnp.exp(b_last) + lax.dot_general(v, kd, tn, preferred_element_type=F32)
    return o, new_state


def _gla_scan_kernel(q_ref, k_ref, v_ref, g_ref, o_ref, state_ref, *, reverse, n_chunks):
    @pl.when(pl.program_id(2) == 0)
    def _():
        state_ref[...] = jnp.zeros_like(state_ref)

    order = range(n_chunks - 1, -1, -1) if reverse else range(n_chunks)
    for cidx in order:
        rows = pl.ds(cidx * GLA_CHUNK, GLA_CHUNK)
        o, new_state = _gla_chunk(q_ref[rows, :].astype(F32), k_ref[rows, :].astype(F32), v_ref[rows, :],
                                  g_ref[rows, :], state_ref[...], reverse)
        o_ref[rows, :] = o
        state_ref[...] = new_state


def _gla_scan(st, q, k, v, g, reverse):
    b = st.batch
    key_dim, value_dim = q.shape[1], v.shape[1]
    dk, dv = key_dim // GLA_HEADS, value_dim // GLA_HEADS
    t_rows = ROW_TILE
    nc, nl = st.ctx_len // t_rows, st.seq // t_rows

    def row_block(bi, s):
        if reverse:
            ctx = st.lat_tiles + bi * nc + (nc - 1 - s)
            lat = bi * nl + (nl - 1 - (s - nc))
        else:
            ctx = st.lat_tiles + bi * nc + s
            lat = bi * nl + (s - nc)
        return jnp.where(s < nc, ctx, lat)

    spec = lambda w: pl.BlockSpec((t_rows, w), lambda bi, h, s: (row_block(bi, s), h))
    return pl.pallas_call(
        functools.partial(_gla_scan_kernel, reverse=reverse, n_chunks=t_rows // GLA_CHUNK),
        grid=(b, GLA_HEADS, nc + nl),
        in_specs=[spec(dk), spec(dk), spec(dv), spec(dk)],
        out_specs=spec(dv),
        out_shape=jax.ShapeDtypeStruct((st.n, value_dim), F32),
        scratch_shapes=[pltpu.VMEM((dv, dk), F32)],
        compiler_params=_cparams(("parallel", "parallel", "arbitrary")),
        name="gla_scan_bwd" if reverse else "gla_scan_fwd",
    )(q, k, v, g)


def _route(logits, n_real):
    lane = lax.broadcasted_iota(jnp.int32, logits.shape, 1)
    neg = jnp.float32(-jnp.inf)
    big = jnp.int32(2 ** 30)
    is_grp = lane < MOE_GROUPS
    gl = jnp.where(is_grp, logits, neg)
    gmax = jnp.max(gl, axis=-1, keepdims=True)
    g_top = jnp.min(jnp.where(gl == gmax, lane, big), axis=-1, keepdims=True)
    p_grp = 1.0 / jnp.sum(jnp.exp(gl - gmax), axis=-1, keepdims=True)
    first = MOE_GROUPS + g_top * MOE_PER_GROUP
    in_grp = (lane >= first) & (lane < first + MOE_PER_GROUP)
    el = jnp.where(in_grp, logits, neg)
    emax = jnp.max(el, axis=-1, keepdims=True)
    pe = jnp.exp(el - emax)
    pe = pe / jnp.sum(pe, axis=-1, keepdims=True)
    v1 = jnp.max(pe, axis=-1, keepdims=True)
    i1 = jnp.min(jnp.where(in_grp & (pe == v1), lane, big), axis=-1, keepdims=True)
    rest = jnp.where(in_grp & (lane != i1), pe, -1.0)
    v2 = jnp.max(rest, axis=-1, keepdims=True)
    i2 = jnp.min(jnp.where(rest == v2, lane, big), axis=-1, keepdims=True)
    denom = v1 + v2
    w1 = p_grp * v1 / denom
    w2 = p_grp * v2 / denom
    e1 = (i1 - MOE_GROUPS).astype(F32)
    e2 = (i2 - MOE_GROUPS).astype(F32)
    del n_real
    return jnp.where(lane == 0, e1, jnp.where(lane == 1, e2, jnp.where(lane == 2, w1, jnp.where(lane == 3, w2, 0.0))))


def _post_mix(y, x_ref, g1_ref, lng_ref, lnb_ref, sh2_ref, sc2_ref, wr_hi_ref, wr_lo_ref, rb_ref,
              x1_ref, h2_ref, route_ref, alpha):
    x1 = _layer_norm(alpha * x_ref[...] + g1_ref[...] * y, lng_ref[...], lnb_ref[...])
    x1_ref[...] = x1
    h2 = x1 * (1.0 + sc2_ref[...]) + sh2_ref[...]
    h2_ref[...] = h2
    hi = h2.astype(BF16)
    lo = (h2 - hi.astype(F32)).astype(BF16)
    w_hi = wr_hi_ref[...]
    logits = (jnp.dot(hi, w_hi, preferred_element_type=F32) + jnp.dot(lo, w_hi, preferred_element_type=F32)
              + jnp.dot(hi, wr_lo_ref[...], preferred_element_type=F32)) + rb_ref[...]
    route_ref[...] = _route(logits, None)


def _mla_out_kernel(o_ref, w_o_ref, *rest, alpha):
    y = jnp.dot(o_ref[...], w_o_ref[...], preferred_element_type=F32)
    _post_mix(y, *rest, alpha=alpha)


def _gla_out_kernel(of_ref, ob_ref, r_ref, ng_ref, w_o_ref, *rest, alpha, dv):
    o = of_ref[...] + ob_ref[...]
    r = r_ref[...]
    gate = r * _sigmoid(r)
    ng = ng_ref[...]
    parts = []
    for hd in range(GLA_HEADS):
        parts.append(_rms_norm(o[:, hd * dv:(hd + 1) * dv], ng) * gate[:, hd * dv:(hd + 1) * dv])
    u = jnp.concatenate(parts, axis=1).astype(BF16)
    y = jnp.dot(u, w_o_ref[...], preferred_element_type=F32)
    _post_mix(y, *rest, alpha=alpha)


def _mix_out(st, n_tiles, mixer_inputs, mixer_specs, kernel, x, mods, ln_g, ln_b, w_o, wr_hi, wr_lo, rbias):
    d = st.d
    rows = n_tiles * ROW_TILE
    common_specs = [
        _row_spec(d), _mod_spec(st, 2), _const_spec((1, d)), _const_spec((1, d)),
        _mod_spec(st, 3), _mod_spec(st, 4), _const_spec(wr_hi.shape), _const_spec(wr_lo.shape),
        _const_spec((1, LANES)),
    ]
    return pl.pallas_call(
        kernel,
        grid=(n_tiles,),
        in_specs=mixer_specs + [_const_spec(w_o.shape)] + common_specs,
        out_specs=[_row_spec(d), _row_spec(d), _row_spec(LANES)],
        out_shape=[jax.ShapeDtypeStruct((rows, d), F32), jax.ShapeDtypeStruct((rows, d), F32),
                   jax.ShapeDtypeStruct((rows, LANES), F32)],
        compiler_params=_cparams(("parallel",)),
        name="mixer_out_ln_route",
    )(*mixer_inputs, w_o, x, mods, ln_g, ln_b, mods, mods, wr_hi, wr_lo, rbias)


def _rank_kernel(route_ref, rank_ref, counts_ref, run_ref):
    t = pl.program_id(0)

    @pl.when(t == 0)
    def _():
        run_ref[...] = jnp.zeros_like(run_ref)

    route = route_ref[...]
    tm = route.shape[0]
    lane = lax.broadcasted_iota(jnp.int32, route.shape, 1)
    e0 = route[:, 0:1].astype(jnp.int32)
    e1 = route[:, 1:2].astype(jnp.int32)
    oh0 = (lane == e0).astype(F32)
    oh1 = (lane == e1).astype(F32)
    both = oh0 + oh1
    ri = lax.broadcasted_iota(jnp.int32, (tm, tm), 0)
    ci = lax.broadcasted_iota(jnp.int32, (tm, tm), 1)
    tri = (ci < ri).astype(BF16)
    before = jnp.dot(tri, both.astype(BF16), preferred_element_type=F32) + run_ref[0:1, :]
    r0 = jnp.sum(oh0 * before, axis=-1, keepdims=True)
    r1 = jnp.sum(oh1 * before, axis=-1, keepdims=True)
    rank_ref[...] = jnp.where(lane == 0, r0, jnp.where(lane == 1, r1, 0.0)).astype(jnp.int32)
    run_ref[...] = run_ref[...] + jnp.sum(both, axis=0, keepdims=True)
    counts_ref[...] = run_ref[...].astype(jnp.int32)


def _moe_ranks(route):
    n = route.shape[0]
    tm = ROW_TILE
    return pl.pallas_call(
        _rank_kernel,
        grid=(n // tm,),
        in_specs=[_row_spec(LANES)],
        out_specs=[_row_spec(LANES), _const_spec((8, LANES))],
        out_shape=[jax.ShapeDtypeStruct((n, LANES), jnp.int32), jax.ShapeDtypeStruct((8, LANES), jnp.int32)],
        scratch_shapes=[pltpu.VMEM((8, LANES), F32)],
        compiler_params=_cparams(("arbitrary",)),
        name="moe_ranks",
    )(route)


def _dispatch_kernel(slots_ref, pad_start_ref, pad_end_ref, h_ref, xs_ref, zero_ref, sem_ref, zsem_ref):
    t = pl.program_id(0)
    tm = h_ref.shape[0]

    @pl.when(t == 0)
    def _():
        zero_ref[...] = jnp.zeros_like(zero_ref)
        for e in range(MOE_EXPERTS):
            @pl.when(pad_end_ref[e] > pad_start_ref[e])
            def _():
                start = pl.multiple_of(pad_end_ref[e] - MOE_BLOCK, MOE_BLOCK)
                pltpu.make_async_copy(zero_ref, xs_ref.at[pl.ds(start, MOE_BLOCK)], zsem_ref).start()
        n_used = pad_end_ref[MOE_EXPERTS - 1] // MOE_BLOCK
        n_blocks = xs_ref.shape[0] // MOE_BLOCK

        def fill_tail(i, carry):
            start = pl.multiple_of(i * MOE_BLOCK, MOE_BLOCK)
            pltpu.make_async_copy(zero_ref, xs_ref.at[pl.ds(start, MOE_BLOCK)], zsem_ref).start()
            return carry

        def wait_tail(i, carry):
            pltpu.make_async_copy(zero_ref, xs_ref.at[pl.ds(0, MOE_BLOCK)], zsem_ref).wait()
            return carry

        lax.fori_loop(n_used, n_blocks, fill_tail, 0)
        for e in range(MOE_EXPERTS):
            @pl.when(pad_end_ref[e] > pad_start_ref[e])
            def _():
                pltpu.make_async_copy(zero_ref, xs_ref.at[pl.ds(0, MOE_BLOCK)], zsem_ref).wait()
        lax.fori_loop(n_used, n_blocks, wait_tail, 0)

    base = t * (2 * tm)

    def issue(r, carry):
        s0 = slots_ref[base + 2 * r]
        s1 = slots_ref[base + 2 * r + 1]
        pltpu.make_async_copy(h_ref.at[pl.ds(r, 1)], xs_ref.at[pl.ds(s0, 1)], sem_ref.at[0]).start()
        pltpu.make_async_copy(h_ref.at[pl.ds(r, 1)], xs_ref.at[pl.ds(s1, 1)], sem_ref.at[1]).start()
        return carry

    lax.fori_loop(0, tm, issue, 0)
    pltpu.make_async_copy(h_ref, xs_ref.at[pl.ds(0, tm)], sem_ref.at[0]).wait()
    pltpu.make_async_copy(h_ref, xs_ref.at[pl.ds(0, tm)], sem_ref.at[1]).wait()


def _moe_dispatch(h2, slots, pad_start, pad_end, n_rows):
    n, d = h2.shape
    tm = ROW_TILE
    grid_spec = pltpu.PrefetchScalarGridSpec(
        num_scalar_prefetch=3,
        grid=(n // tm,),
        in_specs=[pl.BlockSpec((tm, d), lambda t, *_: (t, 0))],
        out_specs=pl.BlockSpec(memory_space=pl.ANY),
        scratch_shapes=[pltpu.VMEM((MOE_BLOCK, d), F32), pltpu.SemaphoreType.DMA((2,)),
                        pltpu.SemaphoreType.DMA(())],
    )
    return pl.pallas_call(
        _dispatch_kernel,
        grid_spec=grid_spec,
        out_shape=jax.ShapeDtypeStruct((n_rows, d), F32),
        compiler_params=_cparams(("arbitrary",)),
        name="moe_dispatch",
    )(slots, pad_start, pad_end, h2)


def _experts_kernel(block_exp_ref, n_used_ref, xs_ref, w1_ref, w3_ref, w2_ref, o_ref, w1b_ref, w3b_ref, w2b_ref):
    i = pl.program_id(0)

    @pl.when(i < n_used_ref[0])
    def _():
        prev = block_exp_ref[jnp.maximum(i - 1, 0)]

        @pl.when((i == 0) | (block_exp_ref[i] != prev))
        def _():
            w1b_ref[...] = w1_ref[...].astype(BF16)
            w3b_ref[...] = w3_ref[...].astype(BF16)
            w2b_ref[...] = w2_ref[...].astype(BF16)

        xb = xs_ref[...].astype(BF16)
        a = jnp.dot(xb, w1b_ref[...], preferred_element_type=F32)
        g = jnp.dot(xb, w3b_ref[...], preferred_element_type=F32)
        hid = (a * _sigmoid(a) * g).astype(BF16)
        o_ref[...] = jnp.dot(hid, w2b_ref[...], preferred_element_type=F32)

    @pl.when(i >= n_used_ref[0])
    def _():
        o_ref[...] = jnp.zeros_like(o_ref)


def _moe_experts(xs, block_exp, n_used, w1, w3, w2):
    n_rows, d = xs.shape
    n_blocks = n_rows // MOE_BLOCK
    f = w1.shape[-1]

    def blk(i, be, nu):
        return jnp.minimum(i, nu[0] - 1)

    grid_spec = pltpu.PrefetchScalarGridSpec(
        num_scalar_prefetch=2,
        grid=(n_blocks,),
        in_specs=[
            pl.BlockSpec((MOE_BLOCK, d), lambda i, be, nu: (blk(i, be, nu), 0)),
            pl.BlockSpec((None, d, f), lambda i, be, nu: (be[blk(i, be, nu)], 0, 0)),
            pl.BlockSpec((None, d, f), lambda i, be, nu: (be[blk(i, be, nu)], 0, 0)),
            pl.BlockSpec((None, f, d), lambda i, be, nu: (be[blk(i, be, nu)], 0, 0)),
        ],
        out_specs=pl.BlockSpec((MOE_BLOCK, d), lambda i, be, nu: (i, 0)),
        scratch_shapes=[pltpu.VMEM((d, f), BF16), pltpu.VMEM((d, f), BF16), pltpu.VMEM((f, d), BF16)],
    )
    return pl.pallas_call(
        _experts_kernel,
        grid_spec=grid_spec,
        out_shape=jax.ShapeDtypeStruct((n_rows, d), F32),
        compiler_params=_cparams(("arbitrary",)),
        name="moe_experts",
    )(block_exp, n_used, xs, w1, w3, w2)


def _combine_kernel(slots_ref, ys_ref, x1_ref, route_ref, g2_ref, lng_ref, lnb_ref, out_ref, buf_ref, sem_ref,
                    *, alpha):
    t = pl.program_id(0)
    tm = x1_ref.shape[0]
    base = t * (2 * tm)

    def issue(r, carry):
        s0 = slots_ref[base + 2 * r]
        s1 = slots_ref[base + 2 * r + 1]
        pltpu.make_async_copy(ys_ref.at[pl.ds(s0, 1)], buf_ref.at[0, pl.ds(r, 1)], sem_ref.at[0]).start()
        pltpu.make_async_copy(ys_ref.at[pl.ds(s1, 1)], buf_ref.at[1, pl.ds(r, 1)], sem_ref.at[1]).start()
        return carry

    lax.fori_loop(0, tm, issue, 0)
    pltpu.make_async_copy(ys_ref.at[pl.ds(0, tm)], buf_ref.at[0], sem_ref.at[0]).wait()
    pltpu.make_async_copy(ys_ref.at[pl.ds(0, tm)], buf_ref.at[1], sem_ref.at[1]).wait()
    route = route_ref[...]
    y = route[:, 2:3] * buf_ref[0] + route[:, 3:4] * buf_ref[1]
    out_ref[...] = _layer_norm(alpha * x1_ref[...] + g2_ref[...] * y, lng_ref[...], lnb_ref[...])


def _moe_combine(st, ys, slots, x1, route, mods, ln_g, ln_b, alpha):
    n, d = x1.shape
    tm = ROW_TILE
    grid_spec = pltpu.PrefetchScalarGridSpec(
        num_scalar_prefetch=1,
        grid=(n // tm,),
        in_specs=[
            pl.BlockSpec(memory_space=pl.ANY),
            pl.BlockSpec((tm, d), lambda t, s: (t, 0)),
            pl.BlockSpec((tm, LANES), lambda t, s: (t, 0)),
            pl.BlockSpec((None, 1, d), lambda t, s: (st.mod_row(t), 0, 5)),
            pl.BlockSpec((1, d), lambda t, s: (0, 0)),
            pl.BlockSpec((1, d), lambda t, s: (0, 0)),
        ],
        out_specs=pl.BlockSpec((tm, d), lambda t, s: (t, 0)),
        scratch_shapes=[pltpu.VMEM((2, tm, d), F32), pltpu.SemaphoreType.DMA((2,))],
    )
    return pl.pallas_call(
        functools.partial(_combine_kernel, alpha=alpha),
        grid_spec=grid_spec,
        out_shape=jax.ShapeDtypeStruct((n, d), F32),
        compiler_params=_cparams(("arbitrary",)),
        name="moe_combine_ln",
    )(slots, ys, x1, route, mods, ln_g, ln_b)


def _hier_moe(st, h2, route, x1, mods, ln_g, ln_b, w1, w3, w2, alpha):
    n = h2.shape[0]
    ranks, counts = _moe_ranks(route)
    counts = counts[0, :MOE_EXPERTS]
    padded = (counts + MOE_BLOCK - 1) // MOE_BLOCK * MOE_BLOCK
    pad_end = jnp.cumsum(padded).astype(jnp.int32)
    pad_start = pad_end - padded
    expert = route[:, :2].astype(jnp.int32)
    slots = (jnp.take(pad_start, expert, axis=0) + ranks[:, :2]).reshape(2 * n)
    n_blocks = -(-(2 * n) // MOE_BLOCK) + MOE_EXPERTS
    block_exp = jnp.minimum(
        jnp.searchsorted(pad_end, jnp.arange(n_blocks, dtype=jnp.int32) * MOE_BLOCK, side="right"),
        MOE_EXPERTS - 1).astype(jnp.int32)
    n_used = (pad_end[-1:] // MOE_BLOCK).astype(jnp.int32)
    xs = _moe_dispatch(h2, slots, pad_start, pad_end, n_blocks * MOE_BLOCK)
    ys = _moe_experts(xs, block_exp, n_used, w1, w3, w2)
    return _moe_combine(st, ys, slots, x1, route, mods, ln_g, ln_b, alpha)


def _swap_halves(w):
    half = w.shape[-1] // 2
    return jnp.concatenate([w[..., half:], w[..., :half]], axis=-1)


def _mla_weights(w_in, w_uq, w_ukv, w_o):
    qr, kvr = MLA_Q_RANK, MLA_KV_RANK
    w_pe = w_in[:, qr + kvr:]
    w_in_ext = jnp.concatenate([w_in, _swap_halves(w_pe)], axis=1).astype(BF16)
    uq = w_uq.reshape(qr, MLA_HEADS, MLA_NOPE + MLA_ROPE)
    q_nope = uq[:, :, :MLA_NOPE].reshape(qr, MLA_HEADS * MLA_NOPE)
    q_pe = uq[:, :, MLA_NOPE:]
    q_pe2 = jnp.concatenate([q_pe, _swap_halves(q_pe)], axis=-1).reshape(qr, MLA_HEADS * 2 * MLA_ROPE)
    w_q = jnp.concatenate([q_nope, q_pe2], axis=1).astype(BF16)
    ukv = w_ukv.reshape(kvr, MLA_HEADS, MLA_NOPE + MLA_V)
    w_kv = jnp.concatenate([ukv[:, :, :MLA_NOPE].reshape(kvr, -1), ukv[:, :, MLA_NOPE:].reshape(kvr, -1)],
                           axis=1).astype(BF16)
    return w_in_ext, w_q, w_kv, w_o.astype(BF16)


def _gla_weights(w_in, gate_a, gate_b, gate_bias, w_o):
    d = w_in.shape[0]
    key_dim = gate_b.shape[-1]
    rank = gate_a.shape[-1]
    ga = jnp.concatenate([gate_a[0], gate_a[1], jnp.zeros((d, LANES - 2 * rank), F32)], axis=1)
    w_qkg = jnp.concatenate([w_in[:, :2 * key_dim], ga], axis=1).astype(BF16)
    w_vr = w_in[:, 2 * key_dim:].astype(BF16)
    gb = jnp.zeros((LANES, 2 * key_dim), F32)
    gb = gb.at[:rank, :key_dim].set(gate_b[0]).at[rank:2 * rank, key_dim:].set(gate_b[1])
    bias = jnp.concatenate([gate_bias[0], gate_bias[1]])[None, :]
    return w_qkg, gb.astype(BF16), bias, w_vr, w_o.astype(BF16)


def _rope_table(seq):
    n_rows = seq // GRID_W
    row = jnp.repeat(jnp.arange(n_rows, dtype=F32), GRID_W)
    col = jnp.tile(jnp.arange(GRID_W, dtype=F32), n_rows)
    n_freq = MLA_ROPE // 4
    inv_freq = jnp.power(ROPE_THETA, -jnp.arange(n_freq, dtype=F32) / n_freq)
    ang = jnp.concatenate([row[:, None] * inv_freq, col[:, None] * inv_freq], axis=-1)
    cos, sin = jnp.cos(ang), jnp.sin(ang)
    lat = jnp.concatenate([cos, cos, -sin, sin], axis=-1)
    ident = jnp.concatenate([jnp.ones((ROW_TILE, MLA_ROPE), F32), jnp.zeros((ROW_TILE, MLA_ROPE), F32)], axis=-1)
    return jnp.concatenate([lat, ident], axis=0)


def kernel(x, c, ctx, c_ctx, w_mod, b_mod, ln1_g, ln1_b, ln2_g, ln2_b, mla_w_in, mla_q_norm, mla_w_uq, mla_kv_norm, mla_w_ukv, mla_w_o, gla_w_in, gla_gate_a, gla_gate_b, gla_gate_bias, gla_norm, gla_w_o, moe_w_grp, moe_b_grp, moe_w_exp, moe_b_exp, moe_w1, moe_w3, moe_w2):
    batch, seq, d = x.shape
    ctx_len = ctx.shape[1]
    depth = w_mod.shape[0]
    assert batch + 1 <= 8
    alpha = (2.0 * depth) ** 0.25
    full = _Stream(batch, seq, ctx_len, d)

    cond = jnp.concatenate([c, c_ctx[None, :], jnp.zeros((8 - batch - 1, d), F32)], axis=0)
    mods_all = _adaln_tables(cond, w_mod, b_mod)
    cs_tab = _rope_table(seq)
    xs = jnp.concatenate([x.reshape(batch * seq, d), ctx.reshape(batch * ctx_len, d)], axis=0)

    for i in range(depth):
        last = i == depth - 1
        mods = mods_all[i].reshape(8, 1, N_MOD * d)
        j = i // 2
        n_tiles = full.lat_tiles if last else full.tiles
        wr = jnp.concatenate([moe_w_grp[i], moe_w_exp[i],
                              jnp.zeros((d, LANES - MOE_GROUPS - MOE_EXPERTS), F32)], axis=1)
        wr_hi = wr.astype(BF16)
        wr_lo = (wr - wr_hi.astype(F32)).astype(BF16)
        rbias = jnp.concatenate([moe_b_grp[i], moe_b_exp[i],
                                 jnp.zeros((LANES - MOE_GROUPS - MOE_EXPERTS,), F32)])[None, :]
        ln1 = (ln1_g[i][None, :], ln1_b[i][None, :])
        if i % 2 == 0:
            w_in_ext, w_q, w_kv, w_o = _mla_weights(mla_w_in[j], mla_w_uq[j], mla_w_ukv[j], mla_w_o[j])
            q, k, v = _mla_proj(full, xs, mods, cs_tab, w_in_ext, mla_q_norm[j][None, :],
                                mla_kv_norm[j][None, :], w_q, w_kv)
            o = _attention(full, q, k, v, latent=True)
            if not last:
                o = jnp.concatenate([o, _attention(full, q, k, v, latent=False)], axis=0)
            x1, h2, route = _mix_out(full, n_tiles, [o], [_row_spec(o.shape[1])],
                                     functools.partial(_mla_out_kernel, alpha=alpha),
                                     xs, mods, ln1[0], ln1[1], w_o, wr_hi, wr_lo, rbias)
        else:
            w_qkg, gb_ext, gbias, w_vr, w_o = _gla_weights(gla_w_in[j], gla_gate_a[j], gla_gate_b[j],
                                                          gla_gate_bias[j], gla_w_o[j])
            q, k, v, r, gf, gb = _gla_proj(full, xs, mods, w_qkg, gb_ext, gbias, w_vr)
            o_f = _gla_scan(full, q, k, v, gf, reverse=False)
            o_b = _gla_scan(full, q, k, v, gb, reverse=True)
            vd = v.shape[1]
            x1, h2, route = _mix_out(full, n_tiles, [o_f, o_b, r, gla_norm[j][None, :]],
                                     [_row_spec(vd), _row_spec(vd), _row_spec(vd),
                                      _const_spec((1, vd // GLA_HEADS))],
                                     functools.partial(_gla_out_kernel, alpha=alpha, dv=vd // GLA_HEADS),
                                     xs, mods, ln1[0], ln1[1], w_o, wr_hi, wr_lo, rbias)
        xs = _hier_moe(full, h2, route, x1, mods, ln2_g[i][None, :], ln2_b[i][None, :],
                       moe_w1[i], moe_w3[i], moe_w2[i], alpha)
    return xs[:batch * seq].reshape(batch, seq, d)
```

```python
import functools
import math

import jax
import jax.numpy as jnp
import numpy as np
from jax import lax
from jax.experimental import pallas as pl
from jax.experimental.pallas import tpu as pltpu

F32 = jnp.float32
BF16 = jnp.bfloat16

GRID_W = 64
N_MOD = 6
NORM_EPS = 1e-6
MLA_HEADS = 16
MLA_Q_RANK = 768
MLA_KV_RANK = 256
MLA_NOPE = 128
MLA_ROPE = 64
MLA_V = 128
ROPE_THETA = 10000.0
GLA_HEADS = 4
GLA_GATE_RANK = 16
GLA_TAU = 16.0
GLA_CHUNK = 64
GLA_SUB = 16
MOE_GROUPS = 4
MOE_PER_GROUP = 8
MOE_EXPERTS = MOE_GROUPS * MOE_PER_GROUP
MOE_BLOCK = 256

LANES = 128
VMEM_LIMIT = 56 * 1024 * 1024

ROW_TILE = 256
ATTN_TQ = 256
ATTN_TK = 512


def _cparams(sem):
    return pltpu.CompilerParams(dimension_semantics=sem, vmem_limit_bytes=VMEM_LIMIT)


def _sigmoid(x):
    return 1.0 / (1.0 + jnp.exp(-x))


def _layer_norm(z, g, b):
    mu = jnp.mean(z, axis=-1, keepdims=True)
    zc = z - mu
    var = jnp.mean(zc * zc, axis=-1, keepdims=True)
    return zc * lax.rsqrt(var + NORM_EPS) * g + b


def _rms_norm(z, g):
    return z * lax.rsqrt(jnp.mean(z * z, axis=-1, keepdims=True) + NORM_EPS) * g


def _mods_kernel(cond_ref, w_ref, b_ref, o_ref):
    c = cond_ref[...]
    s = c * _sigmoid(c)
    o_ref[...] = jnp.dot(s, w_ref[...], preferred_element_type=F32,
                         precision=lax.Precision.HIGHEST) + b_ref[...]


def _adaln_tables(cond, w_mod, b_mod):
    depth, d, n6 = w_mod.shape
    tn = 1024
    return pl.pallas_call(
        _mods_kernel,
        grid=(depth, n6 // tn),
        in_specs=[
            pl.BlockSpec((8, d), lambda l, j: (0, 0)),
            pl.BlockSpec((None, d, tn), lambda l, j: (l, 0, j)),
            pl.BlockSpec((None, 1, tn), lambda l, j: (l, 0, j)),
        ],
        out_specs=pl.BlockSpec((None, 8, tn), lambda l, j: (l, 0, j)),
        out_shape=jax.ShapeDtypeStruct((depth, 8, n6), F32),
        compiler_params=_cparams(("parallel", "parallel")),
        name="adaln_tables",
    )(cond, w_mod, b_mod.reshape(depth, 1, n6))


class _Stream:
    def __init__(self, batch, seq, ctx_len, d):
        self.batch, self.seq, self.ctx_len, self.d = batch, seq, ctx_len, d
        self.n_lat = batch * seq
        self.n_ctx = batch * ctx_len
        self.n = self.n_lat + self.n_ctx
        assert seq % ROW_TILE == 0 and ctx_len % ROW_TILE == 0
        self.lat_tiles = self.n_lat // ROW_TILE
        self.tiles = self.n // ROW_TILE
        self.tiles_per_batch = seq // ROW_TILE

    def mod_row(self, t):
        return jnp.where(t < self.lat_tiles, t // self.tiles_per_batch, self.batch)


def _mod_spec(st, chunk):
    d = st.d
    return pl.BlockSpec((None, 1, d), lambda t: (st.mod_row(t), 0, chunk))


def _row_spec(width, tm=ROW_TILE):
    return pl.BlockSpec((tm, width), lambda t: (t, 0))


def _const_spec(shape):
    nd = len(shape)
    return pl.BlockSpec(shape, lambda t: (0,) * nd, pipeline_mode=pl.Buffered(1))


def _half_sum(x):
    return x + pltpu.roll(x, LANES // 2, axis=1)


def _mla_proj_kernel(x_ref, sh_ref, sc_ref, cs_ref, w_in_ref, qg_ref, kvg_ref, w_q_ref, w_kv_ref,
                     q_ref, k_ref, v_ref, *, scale):
    h = x_ref[...] * (1.0 + sc_ref[...]) + sh_ref[...]
    lat = jnp.dot(h.astype(BF16), w_in_ref[...], preferred_element_type=F32)
    cq = _rms_norm(lat[:, :MLA_Q_RANK], qg_ref[...]).astype(BF16)
    ckv = _rms_norm(lat[:, MLA_Q_RANK:MLA_Q_RANK + MLA_KV_RANK], kvg_ref[...]).astype(BF16)
    cs = cs_ref[...]
    lane = lax.broadcasted_iota(jnp.int32, cs.shape, 1)
    kpe = _half_sum(lat[:, MLA_Q_RANK + MLA_KV_RANK:] * cs)
    kpe = jnp.where(lane < MLA_ROPE, kpe, 0.0).astype(BF16)
    qall = jnp.dot(cq, w_q_ref[...], preferred_element_type=F32)
    kvall = jnp.dot(ckv, w_kv_ref[...], preferred_element_type=F32)
    hn = MLA_HEADS * MLA_NOPE
    for hd in range(MLA_HEADS):
        lo = hd * LANES
        q_ref[hd, :, :MLA_NOPE] = (qall[:, lo:lo + LANES] * scale).astype(BF16)
        q_ref[hd, :, MLA_NOPE:] = (_half_sum(qall[:, hn + lo:hn + lo + LANES] * cs) * scale).astype(BF16)
        k_ref[hd, :, :MLA_NOPE] = kvall[:, lo:lo + LANES].astype(BF16)
        k_ref[hd, :, MLA_NOPE:] = kpe
        v_ref[hd] = jnp.transpose(kvall[:, hn + lo:hn + lo + LANES]).astype(BF16)


def _mla_proj(st, x, mods, cs_tab, w_in_ext, q_norm, kv_norm, w_q, w_kv):
    n, d = st.n, st.d
    tm = ROW_TILE
    lat_w = w_in_ext.shape[1]
    qk_w = MLA_NOPE + LANES
    scale = (MLA_NOPE + MLA_ROPE) ** -0.5 * math.log2(math.e)
    seq_tiles = st.tiles_per_batch

    def cs_map(t):
        return (jnp.where(t < st.lat_tiles, t % seq_tiles, seq_tiles), 0)

    head_spec = lambda w: pl.BlockSpec((MLA_HEADS, tm, w), lambda t: (0, t, 0))
    return pl.pallas_call(
        functools.partial(_mla_proj_kernel, scale=scale),
        grid=(st.tiles,),
        in_specs=[
            _row_spec(d), _mod_spec(st, 0), _mod_spec(st, 1),
            pl.BlockSpec((tm, LANES), cs_map),
            _const_spec((d, lat_w)), _const_spec((1, MLA_Q_RANK)), _const_spec((1, MLA_KV_RANK)),
            _const_spec(w_q.shape), _const_spec(w_kv.shape),
        ],
        out_specs=[head_spec(qk_w), head_spec(qk_w),
                   pl.BlockSpec((MLA_HEADS, MLA_V, tm), lambda t: (0, 0, t))],
        out_shape=[
            jax.ShapeDtypeStruct((MLA_HEADS, n, qk_w), BF16),
            jax.ShapeDtypeStruct((MLA_HEADS, n, qk_w), BF16),
            jax.ShapeDtypeStruct((MLA_HEADS, MLA_V, n), BF16),
        ],
        compiler_params=_cparams(("parallel",)),
        name="mla_proj",
    )(x, mods, mods, cs_tab, w_in_ext, q_norm, kv_norm, w_q, w_kv)


def _attn_kernel(q_ref, kc_ref, vct_ref, *rest, n_chunks):
    if n_chunks:
        kl_ref, vlt_ref, o_ref = rest
    else:
        (o_ref,) = rest
    q = q_ref[...]
    nt = (((1,), (1,)), ((), ()))

    def scores(k):
        s = lax.dot_general(k, q, nt, preferred_element_type=F32)
        return s, jnp.max(s, axis=0, keepdims=True)

    def chunk_rows(j):
        return pl.ds(j * ATTN_TK, ATTN_TK)

    s, m = scores(kc_ref[...])
    nxt = scores(kl_ref[chunk_rows(0), :]) if n_chunks else None
    p = jnp.exp2(s - m)
    l = jnp.sum(p, axis=0, keepdims=True)
    acc = jnp.dot(vct_ref[...], p.astype(BF16), preferred_element_type=F32)
    for j in range(n_chunks):
        s, m_blk = nxt
        if j + 1 < n_chunks:
            nxt = scores(kl_ref[chunk_rows(j + 1), :])
        m_new = jnp.maximum(m, m_blk)
        alpha = jnp.exp2(m - m_new)
        p = jnp.exp2(s - m_new)
        l = alpha * l + jnp.sum(p, axis=0, keepdims=True)
        acc = alpha * acc + jnp.dot(vlt_ref[:, chunk_rows(j)], p.astype(BF16), preferred_element_type=F32)
        m = m_new
    o_ref[...] = jnp.transpose(acc / l).astype(o_ref.dtype)


def _attention(st, q, k, vt, latent):
    b, seq, lc = st.batch, st.seq, st.ctx_len
    qk_w = q.shape[-1]
    tq = ATTN_TQ
    ctx_blk0 = st.n_lat // lc
    if latent:
        nq = seq // tq
        q_map = lambda bi, h, qi: (h, bi * nq + qi, 0)
        o_map = lambda bi, h, qi: (bi * nq + qi, h)
        rows = st.n_lat
        n_chunks = seq // ATTN_TK
    else:
        nq = lc // tq
        q_map = lambda bi, h, qi: (h, st.n_lat // tq + bi * nq + qi, 0)
        o_map = lambda bi, h, qi: (bi * nq + qi, h)
        rows = st.n_ctx
        n_chunks = 0
    kc_map = lambda bi, h, qi: (h, ctx_blk0 + bi, 0)
    kl_map = lambda bi, h, qi: (h, bi, 0)
    vc_map = lambda bi, h, qi: (h, 0, ctx_blk0 + bi)
    vl_map = lambda bi, h, qi: (h, 0, bi)
    in_specs = [
        pl.BlockSpec((None, tq, qk_w), q_map),
        pl.BlockSpec((None, lc, qk_w), kc_map),
        pl.BlockSpec((None, MLA_V, lc), vc_map),
    ]
    args = [q, k, vt]
    if latent:
        in_specs += [pl.BlockSpec((None, seq, qk_w), kl_map), pl.BlockSpec((None, MLA_V, seq), vl_map)]
        args += [k, vt]
    return pl.pallas_call(
        functools.partial(_attn_kernel, n_chunks=n_chunks),
        grid=(b, MLA_HEADS, nq),
        in_specs=in_specs,
        out_specs=pl.BlockSpec((tq, MLA_V), o_map),
        out_shape=jax.ShapeDtypeStruct((rows, MLA_HEADS * MLA_V), BF16),
        compiler_params=_cparams(("parallel", "parallel", "arbitrary")),
        name="mla_attention_latent" if latent else "mla_attention_context",
    )(*args)


def _gla_qkg_kernel(x_ref, sh_ref, sc_ref, w_ref, gb_ref, bias_ref, q_ref, k_ref, gf_ref, gb_out_ref,
                    *, key_dim, q_scale):
    h = (x_ref[...] * (1.0 + sc_ref[...]) + sh_ref[...]).astype(BF16)
    y = jnp.dot(h, w_ref[...], preferred_element_type=F32)
    q_ref[...] = (y[:, :key_dim] * q_scale).astype(BF16)
    k_ref[...] = y[:, key_dim:2 * key_dim].astype(BF16)
    z = jnp.dot(y[:, 2 * key_dim:].astype(BF16), gb_ref[...], preferred_element_type=F32) + bias_ref[...]
    g = (jnp.minimum(z, 0.0) - jnp.log(1.0 + jnp.exp(-jnp.abs(z)))) * (1.0 / GLA_TAU)
    gf_ref[...] = g[:, :key_dim]
    gb_out_ref[...] = g[:, key_dim:]


def _gla_vr_kernel(x_ref, sh_ref, sc_ref, w_ref, v_ref, r_ref, *, value_dim):
    h = (x_ref[...] * (1.0 + sc_ref[...]) + sh_ref[...]).astype(BF16)
    y = jnp.dot(h, w_ref[...], preferred_element_type=F32)
    v_ref[...] = y[:, :value_dim].astype(BF16)
    r_ref[...] = y[:, value_dim:]


def _gla_proj(st, x, mods, w_qkg, gate_b_ext, gate_bias, w_vr):
    n, d = st.n, st.d
    key_dim = (w_qkg.shape[1] - LANES) // 2
    value_dim = w_vr.shape[1] // 2
    dk = key_dim // GLA_HEADS
    q, k, gf, gb = pl.pallas_call(
        functools.partial(_gla_qkg_kernel, key_dim=key_dim, q_scale=dk ** -0.5),
        grid=(st.tiles,),
        in_specs=[_row_spec(d), _mod_spec(st, 0), _mod_spec(st, 1), _const_spec(w_qkg.shape),
                  _const_spec(gate_b_ext.shape), _const_spec(gate_bias.shape)],
        out_specs=[_row_spec(key_dim)] * 4,
        out_shape=[jax.ShapeDtypeStruct((n, key_dim), BF16), jax.ShapeDtypeStruct((n, key_dim), BF16),
                   jax.ShapeDtypeStruct((n, key_dim), F32), jax.ShapeDtypeStruct((n, key_dim), F32)],
        compiler_params=_cparams(("parallel",)),
        name="gla_proj_qkg",
    )(x, mods, mods, w_qkg, gate_b_ext, gate_bias)
    v, r = pl.pallas_call(
        functools.partial(_gla_vr_kernel, value_dim=value_dim),
        grid=(st.tiles,),
        in_specs=[_row_spec(d), _mod_spec(st, 0), _mod_spec(st, 1), _const_spec(w_vr.shape)],
        out_specs=[_row_spec(value_dim)] * 2,
        out_shape=[jax.ShapeDtypeStruct((n, value_dim), BF16), jax.ShapeDtypeStruct((n, value_dim), F32)],
        compiler_params=_cparams(("parallel",)),
        name="gla_proj_vr",
    )(x, mods, mods, w_vr)
    return q, k, v, r, gf, gb


def _cumsum_rows(x, reverse):
    n = x.shape[0]
    row = lax.broadcasted_iota(jnp.int32, x.shape, 0)
    s = 1
    while s < n:
        if reverse:
            x = x + jnp.where(row < n - s, pltpu.roll(x, n - s, axis=0), 0.0)
        else:
            x = x + jnp.where(row >= s, pltpu.roll(x, s, axis=0), 0.0)
        s *= 2
    return x


def _gla_chunk(q, k, v, g, state_t, reverse):
    c, dk = q.shape
    nsub = c // GLA_SUB
    b = _cumsum_rows(g, reverse)
    b_last = jnp.sum(g, axis=0, keepdims=True)
    nt = (((1,), (1,)), ((), ()))
    qe = (q * jnp.exp(b)).astype(BF16)
    o = lax.dot_general(qe, state_t.astype(BF16), nt, preferred_element_type=F32)

    ri = lax.broadcasted_iota(jnp.int32, (c, LANES), 0)
    ci = lax.broadcasted_iota(jnp.int32, (c, LANES), 1)
    a_mat = jnp.zeros((c, LANES), F32)
    for sidx in range(nsub - 1):
        if reverse:
            src = sidx + 1
            ref = b[src * GLA_SUB:src * GLA_SUB + 1, :]
            q_rows = ri < src * GLA_SUB
        else:
            src = sidx
            ref = b[src * GLA_SUB + GLA_SUB - 1:src * GLA_SUB + GLA_SUB, :]
            q_rows = ri >= (src + 1) * GLA_SUB
        qd = (q * jnp.exp(jnp.minimum(b - ref, 0.0))).astype(BF16)
        kd = (k * jnp.exp(jnp.minimum(ref - b, 0.0))).astype(BF16)
        blk = lax.dot_general(qd, kd, nt, preferred_element_type=F32)
        blk = jnp.concatenate([blk, jnp.zeros((c, LANES - c), F32)], axis=1)
        k_cols = (ci >= src * GLA_SUB) & (ci < (src + 1) * GLA_SUB)
        a_mat = a_mat + jnp.where(q_rows & k_cols, blk, 0.0)
    ones = jnp.ones((dk, LANES), BF16)
    sub_r = lax.broadcasted_iota(jnp.int32, (GLA_SUB, LANES), 0)
    sub_c = lax.broadcasted_iota(jnp.int32, (GLA_SUB, LANES), 1)
    diag_rows = []
    for a in range(nsub):
        lo = a * GLA_SUB
        qa = q[lo:lo + GLA_SUB, :]
        ba = b[lo:lo + GLA_SUB, :]
        terms = []
        for jj in range(GLA_SUB):
            kj = k[lo + jj:lo + jj + 1, :]
            bj = b[lo + jj:lo + jj + 1, :]
            terms.append((qa * kj * jnp.exp(jnp.minimum(ba - bj, 0.0))).astype(BF16))
        sums = jnp.dot(jnp.concatenate(terms, axis=0), ones, preferred_element_type=F32)
        blk = jnp.zeros((GLA_SUB, LANES), F32)
        for jj in range(GLA_SUB):
            keep = (sub_c == lo + jj) & ((sub_r <= jj) if reverse else (sub_r >= jj))
            blk = blk + jnp.where(keep, sums[jj * GLA_SUB:(jj + 1) * GLA_SUB, :], 0.0)
        diag_rows.append(blk)
    a_mat = a_mat + jnp.concatenate(diag_rows, axis=0)
    o = o + jnp.dot(a_mat[:, :c].astype(BF16), v, preferred_element_type=F32)
    kd = (k * jnp.exp(b_last - b)).astype(BF16)
    tn = (((0,), (0,)), ((), ()))
    new_state = state_t * jnp.exp(b_last) + lax.dot_general(v, kd, tn, preferred_element_type=F32)
    return o, new_state


def _gla_scan_kernel(q_ref, k_ref, v_ref, g_ref, o_ref, state_ref, *, reverse, n_chunks):
    @pl.when(pl.program_id(2) == 0)
    def _():
        state_ref[...] = jnp.zeros_like(state_ref)

    order = range(n_chunks - 1, -1, -1) if reverse else range(n_chunks)
    for cidx in order:
        rows = pl.ds(cidx * GLA_CHUNK, GLA_CHUNK)
        o, new_state = _gla_chunk(q_ref[rows, :].astype(F32), k_ref[rows, :].astype(F32), v_ref[rows, :],
                                  g_ref[rows, :], state_ref[...], reverse)
        o_ref[rows, :] = o
        state_ref[...] = new_state


def _gla_scan(st, q, k, v, g, reverse):
    b = st.batch
    key_dim, value_dim = q.shape[1], v.shape[1]
    dk, dv = key_dim // GLA_HEADS, value_dim // GLA_HEADS
    t_rows = ROW_TILE
    nc, nl = st.ctx_len // t_rows, st.seq // t_rows

    def row_block(bi, s):
        if reverse:
            ctx = st.lat_tiles + bi * nc + (nc - 1 - s)
            lat = bi * nl + (nl - 1 - (s - nc))
        else:
            ctx = st.lat_tiles + bi * nc + s
            lat = bi * nl + (s - nc)
        return jnp.where(s < nc, ctx, lat)

    spec = lambda w: pl.BlockSpec((t_rows, w), lambda bi, h, s: (row_block(bi, s), h))
    return pl.pallas_call(
        functools.partial(_gla_scan_kernel, reverse=reverse, n_chunks=t_rows // GLA_CHUNK),
        grid=(b, GLA_HEADS, nc + nl),
        in_specs=[spec(dk), spec(dk), spec(dv), spec(dk)],
        out_specs=spec(dv),
        out_shape=jax.ShapeDtypeStruct((st.n, value_dim), F32),
        scratch_shapes=[pltpu.VMEM((dv, dk), F32)],
        compiler_params=_cparams(("parallel", "parallel", "arbitrary")),
        name="gla_scan_bwd" if reverse else "gla_scan_fwd",
    )(q, k, v, g)


def _route(logits, n_real):
    lane = lax.broadcasted_iota(jnp.int32, logits.shape, 1)
    neg = jnp.float32(-jnp.inf)
    big = jnp.int32(2 ** 30)
    is_grp = lane < MOE_GROUPS
    gl = jnp.where(is_grp, logits, neg)
    gmax = jnp.max(gl, axis=-1, keepdims=True)
    g_top = jnp.min(jnp.where(gl == gmax, lane, big), axis=-1, keepdims=True)
    p_grp = 1.0 / jnp.sum(jnp.exp(gl - gmax), axis=-1, keepdims=True)
    first = MOE_GROUPS + g_top * MOE_PER_GROUP
    in_grp = (lane >= first) & (lane < first + MOE_PER_GROUP)
    el = jnp.where(in_grp, logits, neg)
    emax = jnp.max(el, axis=-1, keepdims=True)
    pe = jnp.exp(el - emax)
    pe = pe / jnp.sum(pe, axis=-1, keepdims=True)
    v1 = jnp.max(pe, axis=-1, keepdims=True)
    i1 = jnp.min(jnp.where(in_grp & (pe == v1), lane, big), axis=-1, keepdims=True)
    rest = jnp.where(in_grp & (lane != i1), pe, -1.0)
    v2 = jnp.max(rest, axis=-1, keepdims=True)
    i2 = jnp.min(jnp.where(rest == v2, lane, big), axis=-1, keepdims=True)
    denom = v1 + v2
    w1 = p_grp * v1 / denom
    w2 = p_grp * v2 / denom
    e1 = (i1 - MOE_GROUPS).astype(F32)
    e2 = (i2 - MOE_GROUPS).astype(F32)
    del n_real
    return jnp.where(lane == 0, e1, jnp.where(lane == 1, e2, jnp.where(lane == 2, w1, jnp.where(lane == 3, w2, 0.0))))


def _post_mix(y, x_ref, g1_ref, lng_ref, lnb_ref, sh2_ref, sc2_ref, wr_hi_ref, wr_lo_ref, rb_ref,
              x1_ref, h2_ref, route_ref, alpha):
    x1 = _layer_norm(alpha * x_ref[...] + g1_ref[...] * y, lng_ref[...], lnb_ref[...])
    x1_ref[...] = x1
    h2 = x1 * (1.0 + sc2_ref[...]) + sh2_ref[...]
    h2_ref[...] = h2
    hi = h2.astype(BF16)
    lo = (h2 - hi.astype(F32)).astype(BF16)
    w_hi = wr_hi_ref[...]
    logits = (jnp.dot(hi, w_hi, preferred_element_type=F32) + jnp.dot(lo, w_hi, preferred_element_type=F32)
              + jnp.dot(hi, wr_lo_ref[...], preferred_element_type=F32)) + rb_ref[...]
    route_ref[...] = _route(logits, None)


def _mla_out_kernel(o_ref, w_o_ref, *rest, alpha):
    y = jnp.dot(o_ref[...], w_o_ref[...], preferred_element_type=F32)
    _post_mix(y, *rest, alpha=alpha)


def _gla_out_kernel(of_ref, ob_ref, r_ref, ng_ref, w_o_ref, *rest, alpha, dv):
    o = of_ref[...] + ob_ref[...]
    r = r_ref[...]
    gate = r * _sigmoid(r)
    ng = ng_ref[...]
    parts = []
    for hd in range(GLA_HEADS):
        parts.append(_rms_norm(o[:, hd * dv:(hd + 1) * dv], ng) * gate[:, hd * dv:(hd + 1) * dv])
    u = jnp.concatenate(parts, axis=1).astype(BF16)
    y = jnp.dot(u, w_o_ref[...], preferred_element_type=F32)
    _post_mix(y, *rest, alpha=alpha)


def _mix_out(st, n_tiles, mixer_inputs, mixer_specs, kernel, x, mods, ln_g, ln_b, w_o, wr_hi, wr_lo, rbias):
    d = st.d
    rows = n_tiles * ROW_TILE
    common_specs = [
        _row_spec(d), _mod_spec(st, 2), _const_spec((1, d)), _const_spec((1, d)),
        _mod_spec(st, 3), _mod_spec(st, 4), _const_spec(wr_hi.shape), _const_spec(wr_lo.shape),
        _const_spec((1, LANES)),
    ]
    return pl.pallas_call(
        kernel,
        grid=(n_tiles,),
        in_specs=mixer_specs + [_const_spec(w_o.shape)] + common_specs,
        out_specs=[_row_spec(d), _row_spec(d), _row_spec(LANES)],
        out_shape=[jax.ShapeDtypeStruct((rows, d), F32), jax.ShapeDtypeStruct((rows, d), F32),
                   jax.ShapeDtypeStruct((rows, LANES), F32)],
        compiler_params=_cparams(("parallel",)),
        name="mixer_out_ln_route",
    )(*mixer_inputs, w_o, x, mods, ln_g, ln_b, mods, mods, wr_hi, wr_lo, rbias)


def _rank_kernel(route_ref, rank_ref, counts_ref, run_ref):
    t = pl.program_id(0)

    @pl.when(t == 0)
    def _():
        run_ref[...] = jnp.zeros_like(run_ref)

    route = route_ref[...]
    tm = route.shape[0]
    lane = lax.broadcasted_iota(jnp.int32, route.shape, 1)
    e0 = route[:, 0:1].astype(jnp.int32)
    e1 = route[:, 1:2].astype(jnp.int32)
    oh0 = (lane == e0).astype(F32)
    oh1 = (lane == e1).astype(F32)
    both = oh0 + oh1
    ri = lax.broadcasted_iota(jnp.int32, (tm, tm), 0)
    ci = lax.broadcasted_iota(jnp.int32, (tm, tm), 1)
    tri = (ci < ri).astype(BF16)
    before = jnp.dot(tri, both.astype(BF16), preferred_element_type=F32) + run_ref[0:1, :]
    r0 = jnp.sum(oh0 * before, axis=-1, keepdims=True)
    r1 = jnp.sum(oh1 * before, axis=-1, keepdims=True)
    rank_ref[...] = jnp.where(lane == 0, r0, jnp.where(lane == 1, r1, 0.0)).astype(jnp.int32)
    run_ref[...] = run_ref[...] + jnp.sum(both, axis=0, keepdims=True)
    counts_ref[...] = run_ref[...].astype(jnp.int32)


def _moe_ranks(route):
    n = route.shape[0]
    tm = ROW_TILE
    return pl.pallas_call(
        _rank_kernel,
        grid=(n // tm,),
        in_specs=[_row_spec(LANES)],
        out_specs=[_row_spec(LANES), _const_spec((8, LANES))],
        out_shape=[jax.ShapeDtypeStruct((n, LANES), jnp.int32), jax.ShapeDtypeStruct((8, LANES), jnp.int32)],
        scratch_shapes=[pltpu.VMEM((8, LANES), F32)],
        compiler_params=_cparams(("arbitrary",)),
        name="moe_ranks",
    )(route)


def _dispatch_kernel(slots_ref, pad_start_ref, pad_end_ref, h_ref, xs_ref, zero_ref, sem_ref, zsem_ref):
    t = pl.program_id(0)
    tm = h_ref.shape[0]

    @pl.when(t == 0)
    def _():
        zero_ref[...] = jnp.zeros_like(zero_ref)
        for e in range(MOE_EXPERTS):
            @pl.when(pad_end_ref[e] > pad_start_ref[e])
            def _():
                start = pl.multiple_of(pad_end_ref[e] - MOE_BLOCK, MOE_BLOCK)
                pltpu.make_async_copy(zero_ref, xs_ref.at[pl.ds(start, MOE_BLOCK)], zsem_ref).start()
        n_used = pad_end_ref[MOE_EXPERTS - 1] // MOE_BLOCK
        n_blocks = xs_ref.shape[0] // MOE_BLOCK

        def fill_tail(i, carry):
            start = pl.multiple_of(i * MOE_BLOCK, MOE_BLOCK)
            pltpu.make_async_copy(zero_ref, xs_ref.at[pl.ds(start, MOE_BLOCK)], zsem_ref).start()
            return carry

        def wait_tail(i, carry):
            pltpu.make_async_copy(zero_ref, xs_ref.at[pl.ds(0, MOE_BLOCK)], zsem_ref).wait()
            return carry

        lax.fori_loop(n_used, n_blocks, fill_tail, 0)
        for e in range(MOE_EXPERTS):
            @pl.when(pad_end_ref[e] > pad_start_ref[e])
            def _():
                pltpu.make_async_copy(zero_ref, xs_ref.at[pl.ds(0, MOE_BLOCK)], zsem_ref).wait()
        lax.fori_loop(n_used, n_blocks, wait_tail, 0)

    base = t * (2 * tm)

    def issue(r, carry):
        s0 = slots_ref[base + 2 * r]
        s1 = slots_ref[base + 2 * r + 1]
        pltpu.make_async_copy(h_ref.at[pl.ds(r, 1)], xs_ref.at[pl.ds(s0, 1)], sem_ref.at[0]).start()
        pltpu.make_async_copy(h_ref.at[pl.ds(r, 1)], xs_ref.at[pl.ds(s1, 1)], sem_ref.at[1]).start()
        return carry

    lax.fori_loop(0, tm, issue, 0)
    pltpu.make_async_copy(h_ref, xs_ref.at[pl.ds(0, tm)], sem_ref.at[0]).wait()
    pltpu.make_async_copy(h_ref, xs_ref.at[pl.ds(0, tm)], sem_ref.at[1]).wait()


def _moe_dispatch(h2, slots, pad_start, pad_end, n_rows):
    n, d = h2.shape
    tm = ROW_TILE
    grid_spec = pltpu.PrefetchScalarGridSpec(
        num_scalar_prefetch=3,
        grid=(n // tm,),
        in_specs=[pl.BlockSpec((tm, d), lambda t, *_: (t, 0))],
        out_specs=pl.BlockSpec(memory_space=pl.ANY),
        scratch_shapes=[pltpu.VMEM((MOE_BLOCK, d), F32), pltpu.SemaphoreType.DMA((2,)),
                        pltpu.SemaphoreType.DMA(())],
    )
    return pl.pallas_call(
        _dispatch_kernel,
        grid_spec=grid_spec,
        out_shape=jax.ShapeDtypeStruct((n_rows, d), F32),
        compiler_params=_cparams(("arbitrary",)),
        name="moe_dispatch",
    )(slots, pad_start, pad_end, h2)


def _experts_kernel(block_exp_ref, n_used_ref, xs_ref, w1_ref, w3_ref, w2_ref, o_ref, w1b_ref, w3b_ref, w2b_ref):
    i = pl.program_id(0)

    @pl.when(i < n_used_ref[0])
    def _():
        prev = block_exp_ref[jnp.maximum(i - 1, 0)]

        @pl.when((i == 0) | (block_exp_ref[i] != prev))
        def _():
            w1b_ref[...] = w1_ref[...].astype(BF16)
            w3b_ref[...] = w3_ref[...].astype(BF16)
            w2b_ref[...] = w2_ref[...].astype(BF16)

        xb = xs_ref[...].astype(BF16)
        a = jnp.dot(xb, w1b_ref[...], preferred_element_type=F32)
        g = jnp.dot(xb, w3b_ref[...], preferred_element_type=F32)
        hid = (a * _sigmoid(a) * g).astype(BF16)
        o_ref[...] = jnp.dot(hid, w2b_ref[...], preferred_element_type=F32)

    @pl.when(i >= n_used_ref[0])
    def _():
        o_ref[...] = jnp.zeros_like(o_ref)


def _moe_experts(xs, block_exp, n_used, w1, w3, w2, layer):
    n_rows, d = xs.shape
    n_blocks = n_rows // MOE_BLOCK
    f = w1.shape[-1]

    def blk(i, be, nu):
        return jnp.minimum(i, nu[0] - 1)

    grid_spec = pltpu.PrefetchScalarGridSpec(
        num_scalar_prefetch=2,
        grid=(n_blocks,),
        in_specs=[
            pl.BlockSpec((MOE_BLOCK, d), lambda i, be, nu: (blk(i, be, nu), 0)),
            pl.BlockSpec((None, None, d, f), lambda i, be, nu: (layer, be[blk(i, be, nu)], 0, 0)),
            pl.BlockSpec((None, None, d, f), lambda i, be, nu: (layer, be[blk(i, be, nu)], 0, 0)),
            pl.BlockSpec((None, None, f, d), lambda i, be, nu: (layer, be[blk(i, be, nu)], 0, 0)),
        ],
        out_specs=pl.BlockSpec((MOE_BLOCK, d), lambda i, be, nu: (i, 0)),
        scratch_shapes=[pltpu.VMEM((d, f), BF16), pltpu.VMEM((d, f), BF16), pltpu.VMEM((f, d), BF16)],
    )
    return pl.pallas_call(
        _experts_kernel,
        grid_spec=grid_spec,
        out_shape=jax.ShapeDtypeStruct((n_rows, d), F32),
        compiler_params=_cparams(("arbitrary",)),
        name="moe_experts",
    )(block_exp, n_used, xs, w1, w3, w2)


def _combine_kernel(slots_ref, ys_ref, x1_ref, route_ref, g2_ref, lng_ref, lnb_ref, out_ref, buf_ref, sem_ref,
                    *, alpha):
    t = pl.program_id(0)
    tm = x1_ref.shape[0]
    base = t * (2 * tm)

    def issue(r, carry):
        s0 = slots_ref[base + 2 * r]
        s1 = slots_ref[base + 2 * r + 1]
        pltpu.make_async_copy(ys_ref.at[pl.ds(s0, 1)], buf_ref.at[0, pl.ds(r, 1)], sem_ref.at[0]).start()
        pltpu.make_async_copy(ys_ref.at[pl.ds(s1, 1)], buf_ref.at[1, pl.ds(r, 1)], sem_ref.at[1]).start()
        return carry

    lax.fori_loop(0, tm, issue, 0)
    pltpu.make_async_copy(ys_ref.at[pl.ds(0, tm)], buf_ref.at[0], sem_ref.at[0]).wait()
    pltpu.make_async_copy(ys_ref.at[pl.ds(0, tm)], buf_ref.at[1], sem_ref.at[1]).wait()
    route = route_ref[...]
    y = route[:, 2:3] * buf_ref[0] + route[:, 3:4] * buf_ref[1]
    out_ref[...] = _layer_norm(alpha * x1_ref[...] + g2_ref[...] * y, lng_ref[...], lnb_ref[...])


def _moe_combine(st, ys, slots, x1, route, mods, ln_g, ln_b, alpha):
    n, d = x1.shape
    tm = ROW_TILE
    grid_spec = pltpu.PrefetchScalarGridSpec(
        num_scalar_prefetch=1,
        grid=(n // tm,),
        in_specs=[
            pl.BlockSpec(memory_space=pl.ANY),
            pl.BlockSpec((tm, d), lambda t, s: (t, 0)),
            pl.BlockSpec((tm, LANES), lambda t, s: (t, 0)),
            pl.BlockSpec((None, 1, d), lambda t, s: (st.mod_row(t), 0, 5)),
            pl.BlockSpec((1, d), lambda t, s: (0, 0)),
            pl.BlockSpec((1, d), lambda t, s: (0, 0)),
        ],
        out_specs=pl.BlockSpec((tm, d), lambda t, s: (t, 0)),
        scratch_shapes=[pltpu.VMEM((2, tm, d), F32), pltpu.SemaphoreType.DMA((2,))],
    )
    return pl.pallas_call(
        functools.partial(_combine_kernel, alpha=alpha),
        grid_spec=grid_spec,
        out_shape=jax.ShapeDtypeStruct((n, d), F32),
        compiler_params=_cparams(("arbitrary",)),
        name="moe_combine_ln",
    )(slots, ys, x1, route, mods, ln_g, ln_b)


def _hier_moe(st, h2, route, x1, mods, ln_g, ln_b, w1, w3, w2, layer, alpha):
    n = h2.shape[0]
    ranks, counts = _moe_ranks(route)
    counts = counts[0, :MOE_EXPERTS]
    padded = (counts + MOE_BLOCK - 1) // MOE_BLOCK * MOE_BLOCK
    pad_end = jnp.cumsum(padded).astype(jnp.int32)
    pad_start = pad_end - padded
    expert = route[:, :2].astype(jnp.int32)
    eids = jnp.arange(MOE_EXPERTS, dtype=jnp.int32)
    start_of = jnp.sum(jnp.where(expert[:, :, None] == eids, pad_start, 0), axis=-1)
    slots = (start_of + ranks[:, :2]).reshape(2 * n)
    n_blocks = -(-(2 * n) // MOE_BLOCK) + MOE_EXPERTS
    block_pos = jnp.arange(n_blocks, dtype=jnp.int32) * MOE_BLOCK
    block_exp = jnp.minimum(jnp.sum((pad_end[None, :] <= block_pos[:, None]).astype(jnp.int32), axis=1),
                            MOE_EXPERTS - 1)
    n_used = (pad_end[-1:] // MOE_BLOCK).astype(jnp.int32)
    xs = _moe_dispatch(h2, slots, pad_start, pad_end, n_blocks * MOE_BLOCK)
    ys = _moe_experts(xs, block_exp, n_used, w1, w3, w2, layer)
    return _moe_combine(st, ys, slots, x1, route, mods, ln_g, ln_b, alpha)


def _swap_halves(w):
    half = w.shape[-1] // 2
    return jnp.concatenate([w[..., half:], w[..., :half]], axis=-1)


def _mla_weights(w_in, w_uq, w_ukv, w_o):
    qr, kvr = MLA_Q_RANK, MLA_KV_RANK
    w_pe = w_in[:, qr + kvr:]
    w_in_ext = jnp.concatenate([w_in, _swap_halves(w_pe)], axis=1).astype(BF16)
    uq = w_uq.reshape(qr, MLA_HEADS, MLA_NOPE + MLA_ROPE)
    q_nope = uq[:, :, :MLA_NOPE].reshape(qr, MLA_HEADS * MLA_NOPE)
    q_pe = uq[:, :, MLA_NOPE:]
    q_pe2 = jnp.concatenate([q_pe, _swap_halves(q_pe)], axis=-1).reshape(qr, MLA_HEADS * 2 * MLA_ROPE)
    w_q = jnp.concatenate([q_nope, q_pe2], axis=1).astype(BF16)
    ukv = w_ukv.reshape(kvr, MLA_HEADS, MLA_NOPE + MLA_V)
    w_kv = jnp.concatenate([ukv[:, :, :MLA_NOPE].reshape(kvr, -1), ukv[:, :, MLA_NOPE:].reshape(kvr, -1)],
                           axis=1).astype(BF16)
    return w_in_ext, w_q, w_kv, w_o.astype(BF16)


def _gla_weights(w_in, gate_a, gate_b, gate_bias, w_o):
    d = w_in.shape[0]
    key_dim = gate_b.shape[-1]
    rank = gate_a.shape[-1]
    ga = jnp.concatenate([gate_a[0], gate_a[1], jnp.zeros((d, LANES - 2 * rank), F32)], axis=1)
    w_qkg = jnp.concatenate([w_in[:, :2 * key_dim], ga], axis=1).astype(BF16)
    w_vr = w_in[:, 2 * key_dim:].astype(BF16)
    gb = jnp.zeros((LANES, 2 * key_dim), F32)
    gb = gb.at[:rank, :key_dim].set(gate_b[0]).at[rank:2 * rank, key_dim:].set(gate_b[1])
    bias = jnp.concatenate([gate_bias[0], gate_bias[1]])[None, :]
    return w_qkg, gb.astype(BF16), bias, w_vr, w_o.astype(BF16)


def _rope_table(seq):
    n_rows = seq // GRID_W
    row = jnp.repeat(jnp.arange(n_rows, dtype=F32), GRID_W)
    col = jnp.tile(jnp.arange(GRID_W, dtype=F32), n_rows)
    n_freq = MLA_ROPE // 4
    inv_freq = jnp.power(ROPE_THETA, -jnp.arange(n_freq, dtype=F32) / n_freq)
    ang = jnp.concatenate([row[:, None] * inv_freq, col[:, None] * inv_freq], axis=-1)
    cos, sin = jnp.cos(ang), jnp.sin(ang)
    lat = jnp.concatenate([cos, cos, -sin, sin], axis=-1)
    ident = jnp.concatenate([jnp.ones((ROW_TILE, MLA_ROPE), F32), jnp.zeros((ROW_TILE, MLA_ROPE), F32)], axis=-1)
    return jnp.concatenate([lat, ident], axis=0)


def kernel(x, c, ctx, c_ctx, w_mod, b_mod, ln1_g, ln1_b, ln2_g, ln2_b, mla_w_in, mla_q_norm, mla_w_uq, mla_kv_norm, mla_w_ukv, mla_w_o, gla_w_in, gla_gate_a, gla_gate_b, gla_gate_bias, gla_norm, gla_w_o, moe_w_grp, moe_b_grp, moe_w_exp, moe_b_exp, moe_w1, moe_w3, moe_w2):
    batch, seq, d = x.shape
    ctx_len = ctx.shape[1]
    depth = w_mod.shape[0]
    assert batch + 1 <= 8
    alpha = (2.0 * depth) ** 0.25
    full = _Stream(batch, seq, ctx_len, d)

    cond = jnp.concatenate([c, c_ctx[None, :], jnp.zeros((8 - batch - 1, d), F32)], axis=0)
    mods_all = _adaln_tables(cond, w_mod, b_mod)
    cs_tab = _rope_table(seq)
    xs = jnp.concatenate([x.reshape(batch * seq, d), ctx.reshape(batch * ctx_len, d)], axis=0)

    for i in range(depth):
        last = i == depth - 1
        mods = mods_all[i].reshape(8, 1, N_MOD * d)
        j = i // 2
        n_tiles = full.lat_tiles if last else full.tiles
        wr = jnp.concatenate([moe_w_grp[i], moe_w_exp[i],
                              jnp.zeros((d, LANES - MOE_GROUPS - MOE_EXPERTS), F32)], axis=1)
        wr_hi = wr.astype(BF16)
        wr_lo = (wr - wr_hi.astype(F32)).astype(BF16)
        rbias = jnp.concatenate([moe_b_grp[i], moe_b_exp[i],
                                 jnp.zeros((LANES - MOE_GROUPS - MOE_EXPERTS,), F32)])[None, :]
        ln1 = (ln1_g[i][None, :], ln1_b[i][None, :])
        if i % 2 == 0:
            w_in_ext, w_q, w_kv, w_o = _mla_weights(mla_w_in[j], mla_w_uq[j], mla_w_ukv[j], mla_w_o[j])
            q, k, v = _mla_proj(full, xs, mods, cs_tab, w_in_ext, mla_q_norm[j][None, :],
                                mla_kv_norm[j][None, :], w_q, w_kv)
            o = _attention(full, q, k, v, latent=True)
            if not last:
                o = jnp.concatenate([o, _attention(full, q, k, v, latent=False)], axis=0)
            x1, h2, route = _mix_out(full, n_tiles, [o], [_row_spec(o.shape[1])],
                                     functools.partial(_mla_out_kernel, alpha=alpha),
                                     xs, mods, ln1[0], ln1[1], w_o, wr_hi, wr_lo, rbias)
        else:
            w_qkg, gb_ext, gbias, w_vr, w_o = _gla_weights(gla_w_in[j], gla_gate_a[j], gla_gate_b[j],
                                                          gla_gate_bias[j], gla_w_o[j])
            q, k, v, r, gf, gb = _gla_proj(full, xs, mods, w_qkg, gb_ext, gbias, w_vr)
            o_f = _gla_scan(full, q, k, v, gf, reverse=False)
            o_b = _gla_scan(full, q, k, v, gb, reverse=True)
            vd = v.shape[1]
            x1, h2, route = _mix_out(full, n_tiles, [o_f, o_b, r, gla_norm[j][None, :]],
                                     [_row_spec(vd), _row_spec(vd), _row_spec(vd),
                                      _const_spec((1, vd // GLA_HEADS))],
                                     functools.partial(_gla_out_kernel, alpha=alpha, dv=vd // GLA_HEADS),
                                     xs, mods, ln1[0], ln1[1], w_o, wr_hi, wr_lo, rbias)
        xs = _hier_moe(full, h2, route, x1, mods, ln2_g[i][None, :], ln2_b[i][None, :],
                       moe_w1, moe_w3, moe_w2, i, alpha)
    return xs[:batch * seq].reshape(batch, seq, d)
```

```python
import functools
import math

import jax
import jax.numpy as jnp
import numpy as np
from jax import lax
from jax.experimental import pallas as pl
from jax.experimental.pallas import tpu as pltpu

F32 = jnp.float32
BF16 = jnp.bfloat16

GRID_W = 64
N_MOD = 6
NORM_EPS = 1e-6
MLA_HEADS = 16
MLA_Q_RANK = 768
MLA_KV_RANK = 256
MLA_NOPE = 128
MLA_ROPE = 64
MLA_V = 128
ROPE_THETA = 10000.0
GLA_HEADS = 4
GLA_GATE_RANK = 16
GLA_TAU = 16.0
GLA_CHUNK = 64
GLA_SUB = 16
MOE_GROUPS = 4
MOE_PER_GROUP = 8
MOE_EXPERTS = MOE_GROUPS * MOE_PER_GROUP
MOE_BLOCK = 256

LANES = 128
VMEM_LIMIT = 56 * 1024 * 1024

ROW_TILE = 256
ATTN_TQ = 512
ATTN_TK = 1024


def _cparams(sem):
    return pltpu.CompilerParams(dimension_semantics=sem, vmem_limit_bytes=VMEM_LIMIT)


def _sigmoid(x):
    return 1.0 / (1.0 + jnp.exp(-x))


def _layer_norm(z, g, b):
    mu = jnp.mean(z, axis=-1, keepdims=True)
    zc = z - mu
    var = jnp.mean(zc * zc, axis=-1, keepdims=True)
    return zc * lax.rsqrt(var + NORM_EPS) * g + b


def _rms_norm(z, g):
    return z * lax.rsqrt(jnp.mean(z * z, axis=-1, keepdims=True) + NORM_EPS) * g


def _mods_kernel(cond_ref, w_ref, b_ref, o_ref):
    c = cond_ref[...]
    s = c * _sigmoid(c)
    o_ref[...] = jnp.dot(s, w_ref[...], preferred_element_type=F32,
                         precision=lax.Precision.HIGHEST) + b_ref[...]


def _adaln_tables(cond, w_mod, b_mod):
    depth, d, n6 = w_mod.shape
    tn = 1024
    return pl.pallas_call(
        _mods_kernel,
        grid=(depth, n6 // tn),
        in_specs=[
            pl.BlockSpec((8, d), lambda l, j: (0, 0)),
            pl.BlockSpec((None, d, tn), lambda l, j: (l, 0, j)),
            pl.BlockSpec((None, 1, tn), lambda l, j: (l, 0, j)),
        ],
        out_specs=pl.BlockSpec((None, 8, tn), lambda l, j: (l, 0, j)),
        out_shape=jax.ShapeDtypeStruct((depth, 8, n6), F32),
        compiler_params=_cparams(("parallel", "parallel")),
        name="adaln_tables",
    )(cond, w_mod, b_mod.reshape(depth, 1, n6))


class _Stream:
    def __init__(self, batch, seq, ctx_len, d):
        self.batch, self.seq, self.ctx_len, self.d = batch, seq, ctx_len, d
        self.n_lat = batch * seq
        self.n_ctx = batch * ctx_len
        self.n = self.n_lat + self.n_ctx
        assert seq % ROW_TILE == 0 and ctx_len % ROW_TILE == 0
        self.lat_tiles = self.n_lat // ROW_TILE
        self.tiles = self.n // ROW_TILE
        self.tiles_per_batch = seq // ROW_TILE

    def mod_row(self, t):
        return jnp.where(t < self.lat_tiles, t // self.tiles_per_batch, self.batch)


def _mod_spec(st, chunk):
    d = st.d
    return pl.BlockSpec((None, 1, d), lambda t: (st.mod_row(t), 0, chunk))


def _row_spec(width, tm=ROW_TILE):
    return pl.BlockSpec((tm, width), lambda t: (t, 0))


def _const_spec(shape):
    nd = len(shape)
    return pl.BlockSpec(shape, lambda t: (0,) * nd, pipeline_mode=pl.Buffered(1))


def _half_sum(x):
    return x + pltpu.roll(x, LANES // 2, axis=1)


def _mla_proj_kernel(x_ref, sh_ref, sc_ref, cs_ref, w_in_ref, qg_ref, kvg_ref, w_q_ref, w_kv_ref,
                     q_ref, k_ref, v_ref, *, scale):
    h = x_ref[...] * (1.0 + sc_ref[...]) + sh_ref[...]
    lat = jnp.dot(h.astype(BF16), w_in_ref[...], preferred_element_type=F32)
    cq = _rms_norm(lat[:, :MLA_Q_RANK], qg_ref[...]).astype(BF16)
    ckv = _rms_norm(lat[:, MLA_Q_RANK:MLA_Q_RANK + MLA_KV_RANK], kvg_ref[...]).astype(BF16)
    cs = cs_ref[...]
    lane = lax.broadcasted_iota(jnp.int32, cs.shape, 1)
    kpe = _half_sum(lat[:, MLA_Q_RANK + MLA_KV_RANK:] * cs)
    kpe = jnp.where(lane < MLA_ROPE, kpe, 0.0).astype(BF16)
    qall = jnp.dot(cq, w_q_ref[...], preferred_element_type=F32)
    kvall = jnp.dot(ckv, w_kv_ref[...], preferred_element_type=F32)
    hn = MLA_HEADS * MLA_NOPE
    for hd in range(MLA_HEADS):
        lo = hd * LANES
        q_ref[hd, :, :MLA_NOPE] = (qall[:, lo:lo + LANES] * scale).astype(BF16)
        q_ref[hd, :, MLA_NOPE:] = (_half_sum(qall[:, hn + lo:hn + lo + LANES] * cs) * scale).astype(BF16)
        k_ref[hd, :, :MLA_NOPE] = kvall[:, lo:lo + LANES].astype(BF16)
        k_ref[hd, :, MLA_NOPE:] = kpe
        v_ref[hd] = jnp.transpose(kvall[:, hn + lo:hn + lo + LANES]).astype(BF16)


def _mla_proj(st, x, mods, cs_tab, w_in_ext, q_norm, kv_norm, w_q, w_kv):
    n, d = st.n, st.d
    tm = ROW_TILE
    lat_w = w_in_ext.shape[1]
    qk_w = MLA_NOPE + LANES
    scale = (MLA_NOPE + MLA_ROPE) ** -0.5 * math.log2(math.e)
    seq_tiles = st.tiles_per_batch

    def cs_map(t):
        return (jnp.where(t < st.lat_tiles, t % seq_tiles, seq_tiles), 0)

    head_spec = lambda w: pl.BlockSpec((MLA_HEADS, tm, w), lambda t: (0, t, 0))
    return pl.pallas_call(
        functools.partial(_mla_proj_kernel, scale=scale),
        grid=(st.tiles,),
        in_specs=[
            _row_spec(d), _mod_spec(st, 0), _mod_spec(st, 1),
            pl.BlockSpec((tm, LANES), cs_map),
            _const_spec((d, lat_w)), _const_spec((1, MLA_Q_RANK)), _const_spec((1, MLA_KV_RANK)),
            _const_spec(w_q.shape), _const_spec(w_kv.shape),
        ],
        out_specs=[head_spec(qk_w), head_spec(qk_w),
                   pl.BlockSpec((MLA_HEADS, MLA_V, tm), lambda t: (0, 0, t))],
        out_shape=[
            jax.ShapeDtypeStruct((MLA_HEADS, n, qk_w), BF16),
            jax.ShapeDtypeStruct((MLA_HEADS, n, qk_w), BF16),
            jax.ShapeDtypeStruct((MLA_HEADS, MLA_V, n), BF16),
        ],
        compiler_params=_cparams(("parallel",)),
        name="mla_proj",
    )(x, mods, mods, cs_tab, w_in_ext, q_norm, kv_norm, w_q, w_kv)


def _attn_kernel(q_ref, kc_ref, vct_ref, *rest, n_chunks):
    if n_chunks:
        kl_ref, vlt_ref, o_ref = rest
    else:
        (o_ref,) = rest
    q = q_ref[...]
    nt = (((1,), (1,)), ((), ()))

    def scores(k):
        s = lax.dot_general(k, q, nt, preferred_element_type=F32)
        return s, jnp.max(s, axis=0, keepdims=True)

    def chunk_rows(j):
        return pl.ds(j * ATTN_TK, ATTN_TK)

    s, m = scores(kc_ref[...])
    nxt = scores(kl_ref[chunk_rows(0), :]) if n_chunks else None
    p = jnp.exp2(s - m)
    l = jnp.sum(p, axis=0, keepdims=True)
    acc = jnp.dot(vct_ref[...], p.astype(BF16), preferred_element_type=F32)
    for j in range(n_chunks):
        s, m_blk = nxt
        if j + 1 < n_chunks:
            nxt = scores(kl_ref[chunk_rows(j + 1), :])
        m_new = jnp.maximum(m, m_blk)
        alpha = jnp.exp2(m - m_new)
        p = jnp.exp2(s - m_new)
        l = alpha * l + jnp.sum(p, axis=0, keepdims=True)
        acc = alpha * acc + jnp.dot(vlt_ref[:, chunk_rows(j)], p.astype(BF16), preferred_element_type=F32)
        m = m_new
    o_ref[...] = jnp.transpose(acc / l).astype(o_ref.dtype)


def _attention(st, q, k, vt, latent):
    b, seq, lc = st.batch, st.seq, st.ctx_len
    qk_w = q.shape[-1]
    tq = ATTN_TQ if latent else min(ATTN_TQ, lc)
    ctx_blk0 = st.n_lat // lc
    if latent:
        nq = seq // tq
        q_map = lambda bi, h, qi: (h, bi * nq + qi, 0)
        o_map = lambda bi, h, qi: (bi * nq + qi, h)
        rows = st.n_lat
        n_chunks = seq // ATTN_TK
    else:
        nq = lc // tq
        q_map = lambda bi, h, qi: (h, st.n_lat // tq + bi * nq + qi, 0)
        o_map = lambda bi, h, qi: (bi * nq + qi, h)
        rows = st.n_ctx
        n_chunks = 0
    kc_map = lambda bi, h, qi: (h, ctx_blk0 + bi, 0)
    kl_map = lambda bi, h, qi: (h, bi, 0)
    vc_map = lambda bi, h, qi: (h, 0, ctx_blk0 + bi)
    vl_map = lambda bi, h, qi: (h, 0, bi)
    in_specs = [
        pl.BlockSpec((None, tq, qk_w), q_map),
        pl.BlockSpec((None, lc, qk_w), kc_map),
        pl.BlockSpec((None, MLA_V, lc), vc_map),
    ]
    args = [q, k, vt]
    if latent:
        in_specs += [pl.BlockSpec((None, seq, qk_w), kl_map), pl.BlockSpec((None, MLA_V, seq), vl_map)]
        args += [k, vt]
    return pl.pallas_call(
        functools.partial(_attn_kernel, n_chunks=n_chunks),
        grid=(b, MLA_HEADS, nq),
        in_specs=in_specs,
        out_specs=pl.BlockSpec((tq, MLA_V), o_map),
        out_shape=jax.ShapeDtypeStruct((rows, MLA_HEADS * MLA_V), BF16),
        compiler_params=_cparams(("parallel", "parallel", "arbitrary")),
        name="mla_attention_latent" if latent else "mla_attention_context",
    )(*args)


def _gla_qkg_kernel(x_ref, sh_ref, sc_ref, w_ref, gb_ref, bias_ref, q_ref, k_ref, gf_ref, gb_out_ref,
                    *, key_dim, q_scale):
    h = (x_ref[...] * (1.0 + sc_ref[...]) + sh_ref[...]).astype(BF16)
    y = jnp.dot(h, w_ref[...], preferred_element_type=F32)
    q_ref[...] = (y[:, :key_dim] * q_scale).astype(BF16)
    k_ref[...] = y[:, key_dim:2 * key_dim].astype(BF16)
    z = jnp.dot(y[:, 2 * key_dim:].astype(BF16), gb_ref[...], preferred_element_type=F32) + bias_ref[...]
    g = (jnp.minimum(z, 0.0) - jnp.log(1.0 + jnp.exp(-jnp.abs(z)))) * (math.log2(math.e) / GLA_TAU)
    gf_ref[...] = g[:, :key_dim]
    gb_out_ref[...] = g[:, key_dim:]


def _gla_vr_kernel(x_ref, sh_ref, sc_ref, w_ref, v_ref, r_ref, *, value_dim):
    h = (x_ref[...] * (1.0 + sc_ref[...]) + sh_ref[...]).astype(BF16)
    y = jnp.dot(h, w_ref[...], preferred_element_type=F32)
    v_ref[...] = y[:, :value_dim].astype(BF16)
    r_ref[...] = y[:, value_dim:]


def _gla_proj(st, x, mods, w_qkg, gate_b_ext, gate_bias, w_vr):
    n, d = st.n, st.d
    key_dim = (w_qkg.shape[1] - LANES) // 2
    value_dim = w_vr.shape[1] // 2
    dk = key_dim // GLA_HEADS
    q, k, gf, gb = pl.pallas_call(
        functools.partial(_gla_qkg_kernel, key_dim=key_dim, q_scale=dk ** -0.5),
        grid=(st.tiles,),
        in_specs=[_row_spec(d), _mod_spec(st, 0), _mod_spec(st, 1), _const_spec(w_qkg.shape),
                  _const_spec(gate_b_ext.shape), _const_spec(gate_bias.shape)],
        out_specs=[_row_spec(key_dim)] * 4,
        out_shape=[jax.ShapeDtypeStruct((n, key_dim), BF16), jax.ShapeDtypeStruct((n, key_dim), BF16),
                   jax.ShapeDtypeStruct((n, key_dim), F32), jax.ShapeDtypeStruct((n, key_dim), F32)],
        compiler_params=_cparams(("parallel",)),
        name="gla_proj_qkg",
    )(x, mods, mods, w_qkg, gate_b_ext, gate_bias)
    v, r = pl.pallas_call(
        functools.partial(_gla_vr_kernel, value_dim=value_dim),
        grid=(st.tiles,),
        in_specs=[_row_spec(d), _mod_spec(st, 0), _mod_spec(st, 1), _const_spec(w_vr.shape)],
        out_specs=[_row_spec(value_dim)] * 2,
        out_shape=[jax.ShapeDtypeStruct((n, value_dim), BF16), jax.ShapeDtypeStruct((n, value_dim), F32)],
        compiler_params=_cparams(("parallel",)),
        name="gla_proj_vr",
    )(x, mods, mods, w_vr)
    return q, k, v, r, gf, gb


def _cumsum_rows(x, reverse):
    n = x.shape[0]
    row = lax.broadcasted_iota(jnp.int32, x.shape, 0)
    s = 1
    while s < n:
        if reverse:
            x = x + jnp.where(row < n - s, pltpu.roll(x, n - s, axis=0), 0.0)
        else:
            x = x + jnp.where(row >= s, pltpu.roll(x, s, axis=0), 0.0)
        s *= 2
    return x


def _gla_chunk(q, k, v, g, state_t, reverse):
    c, dk = q.shape
    nsub = c // GLA_SUB
    half = GLA_SUB // 2
    b = _cumsum_rows(g, reverse)
    b_last = jnp.sum(g, axis=0, keepdims=True)
    nt = (((1,), (1,)), ((), ()))
    qe = (q * jnp.exp2(b)).astype(BF16)
    o = lax.dot_general(qe, state_t.astype(BF16), nt, preferred_element_type=F32)

    q_parts, k_parts = [], []
    for src in (range(1, nsub) if reverse else range(nsub - 1)):
        lo = src * GLA_SUB
        if reverse:
            ref = b[lo:lo + 1, :]
            qd = (q[:lo] * jnp.exp2(b[:lo] - ref)).astype(BF16)
            q_parts.append(jnp.concatenate([qd, jnp.zeros((c - lo, dk), BF16)], axis=0))
        else:
            ref = b[lo + GLA_SUB - 1:lo + GLA_SUB, :]
            qd = (q[lo + GLA_SUB:] * jnp.exp2(b[lo + GLA_SUB:] - ref)).astype(BF16)
            q_parts.append(jnp.concatenate([jnp.zeros((lo + GLA_SUB, dk), BF16), qd], axis=0))
        kd = (k[lo:lo + GLA_SUB] * jnp.exp2(ref - b[lo:lo + GLA_SUB])).astype(BF16)
        pieces = [kd]
        if lo:
            pieces.insert(0, jnp.zeros((lo, dk), BF16))
        if c - lo - GLA_SUB:
            pieces.append(jnp.zeros((c - lo - GLA_SUB, dk), BF16))
        k_parts.append(jnp.concatenate(pieces, axis=0))
    off = lax.dot_general(jnp.concatenate(q_parts, axis=1), jnp.concatenate(k_parts, axis=1), nt,
                          preferred_element_type=F32)
    a_mat = jnp.concatenate([off, jnp.zeros((c, LANES - c), F32)], axis=1)

    ones = jnp.ones((dk, LANES), BF16)
    sub_r = lax.broadcasted_iota(jnp.int32, (half, LANES), 0)
    sub_c = lax.broadcasted_iota(jnp.int32, (half, LANES), 1)
    diag_rows = []
    for a in range(nsub):
        lo = a * GLA_SUB
        terms, spans = [], []
        for jj in range(GLA_SUB):
            if reverse:
                r0, r1 = 0, (half if jj < half else GLA_SUB)
            else:
                r0, r1 = (0 if jj < half else half), GLA_SUB
            kj = k[lo + jj:lo + jj + 1, :]
            bj = b[lo + jj:lo + jj + 1, :]
            terms.append(q[lo + r0:lo + r1] * kj * jnp.exp2(b[lo + r0:lo + r1] - bj))
            spans.append((r0, r1))
        sums = jnp.dot(jnp.concatenate(terms, axis=0).astype(BF16), ones, preferred_element_type=F32)
        blk = [jnp.zeros((half, LANES), F32), jnp.zeros((half, LANES), F32)]
        pos = 0
        for jj, (r0, r1) in enumerate(spans):
            for r in range(r0, r1, half):
                piece = sums[pos:pos + half, :]
                pos += half
                rows = sub_r + r
                keep = (sub_c == lo + jj) & ((rows <= jj) if reverse else (rows >= jj))
                blk[r // half] = blk[r // half] + jnp.where(keep, piece, 0.0)
        diag_rows += blk
    a_mat = a_mat + jnp.concatenate(diag_rows, axis=0)
    o = o + jnp.dot(a_mat[:, :c].astype(BF16), v, preferred_element_type=F32)
    kd = (k * jnp.exp2(b_last - b)).astype(BF16)
    tn = (((0,), (0,)), ((), ()))
    new_state = state_t * jnp.exp2(b_last) + lax.dot_general(v, kd, tn, preferred_element_type=F32)
    return o, new_state


def _gla_scan_kernel(q_ref, k_ref, v_ref, g_ref, o_ref, state_ref, *, reverse, n_chunks):
    @pl.when(pl.program_id(2) == 0)
    def _():
        state_ref[...] = jnp.zeros_like(state_ref)

    order = range(n_chunks - 1, -1, -1) if reverse else range(n_chunks)
    for cidx in order:
        rows = pl.ds(cidx * GLA_CHUNK, GLA_CHUNK)
        o, new_state = _gla_chunk(q_ref[rows, :].astype(F32), k_ref[rows, :].astype(F32), v_ref[rows, :],
                                  g_ref[rows, :], state_ref[...], reverse)
        o_ref[rows, :] = o
        state_ref[...] = new_state


def _gla_scan(st, q, k, v, g, reverse):
    b = st.batch
    key_dim, value_dim = q.shape[1], v.shape[1]
    dk, dv = key_dim // GLA_HEADS, value_dim // GLA_HEADS
    t_rows = ROW_TILE
    nc, nl = st.ctx_len // t_rows, st.seq // t_rows

    def row_block(bi, s):
        if reverse:
            ctx = st.lat_tiles + bi * nc + (nc - 1 - s)
            lat = bi * nl + (nl - 1 - (s - nc))
        else:
            ctx = st.lat_tiles + bi * nc + s
            lat = bi * nl + (s - nc)
        return jnp.where(s < nc, ctx, lat)

    spec = lambda w: pl.BlockSpec((t_rows, w), lambda bi, h, s: (row_block(bi, s), h))
    return pl.pallas_call(
        functools.partial(_gla_scan_kernel, reverse=reverse, n_chunks=t_rows // GLA_CHUNK),
        grid=(b, GLA_HEADS, nc + nl),
        in_specs=[spec(dk), spec(dk), spec(dv), spec(dk)],
        out_specs=spec(dv),
        out_shape=jax.ShapeDtypeStruct((st.n, value_dim), F32),
        scratch_shapes=[pltpu.VMEM((dv, dk), F32)],
        compiler_params=_cparams(("parallel", "parallel", "arbitrary")),
        name="gla_scan_bwd" if reverse else "gla_scan_fwd",
    )(q, k, v, g)


def _route(logits, n_real):
    lane = lax.broadcasted_iota(jnp.int32, logits.shape, 1)
    neg = jnp.float32(-jnp.inf)
    big = jnp.int32(2 ** 30)
    is_grp = lane < MOE_GROUPS
    gl = jnp.where(is_grp, logits, neg)
    gmax = jnp.max(gl, axis=-1, keepdims=True)
    g_top = jnp.min(jnp.where(gl == gmax, lane, big), axis=-1, keepdims=True)
    p_grp = 1.0 / jnp.sum(jnp.exp(gl - gmax), axis=-1, keepdims=True)
    first = MOE_GROUPS + g_top * MOE_PER_GROUP
    in_grp = (lane >= first) & (lane < first + MOE_PER_GROUP)
    el = jnp.where(in_grp, logits, neg)
    emax = jnp.max(el, axis=-1, keepdims=True)
    pe = jnp.exp(el - emax)
    pe = pe / jnp.sum(pe, axis=-1, keepdims=True)
    v1 = jnp.max(pe, axis=-1, keepdims=True)
    i1 = jnp.min(jnp.where(in_grp & (pe == v1), lane, big), axis=-1, keepdims=True)
    rest = jnp.where(in_grp & (lane != i1), pe, -1.0)
    v2 = jnp.max(rest, axis=-1, keepdims=True)
    i2 = jnp.min(jnp.where(rest == v2, lane, big), axis=-1, keepdims=True)
    denom = v1 + v2
    w1 = p_grp * v1 / denom
    w2 = p_grp * v2 / denom
    e1 = (i1 - MOE_GROUPS).astype(F32)
    e2 = (i2 - MOE_GROUPS).astype(F32)
    del n_real
    return jnp.where(lane == 0, e1, jnp.where(lane == 1, e2, jnp.where(lane == 2, w1, jnp.where(lane == 3, w2, 0.0))))


def _post_mix(y, x_ref, g1_ref, lng_ref, lnb_ref, sh2_ref, sc2_ref, wr_hi_ref, wr_lo_ref, rb_ref,
              x1_ref, h2_ref, route_ref, alpha):
    x1 = _layer_norm(alpha * x_ref[...] + g1_ref[...] * y, lng_ref[...], lnb_ref[...])
    x1_ref[...] = x1
    h2 = x1 * (1.0 + sc2_ref[...]) + sh2_ref[...]
    h2_ref[...] = h2
    hi = h2.astype(BF16)
    lo = (h2 - hi.astype(F32)).astype(BF16)
    w_hi = wr_hi_ref[...]
    logits = (jnp.dot(hi, w_hi, preferred_element_type=F32) + jnp.dot(lo, w_hi, preferred_element_type=F32)
              + jnp.dot(hi, wr_lo_ref[...], preferred_element_type=F32)) + rb_ref[...]
    route_ref[...] = _route(logits, None)


def _mla_out_kernel(o_ref, w_o_ref, *rest, alpha):
    y = jnp.dot(o_ref[...], w_o_ref[...], preferred_element_type=F32)
    _post_mix(y, *rest, alpha=alpha)


def _gla_out_kernel(of_ref, ob_ref, r_ref, ng_ref, w_o_ref, *rest, alpha, dv):
    o = of_ref[...] + ob_ref[...]
    r = r_ref[...]
    gate = r * _sigmoid(r)
    ng = ng_ref[...]
    parts = []
    for hd in range(GLA_HEADS):
        parts.append(_rms_norm(o[:, hd * dv:(hd + 1) * dv], ng) * gate[:, hd * dv:(hd + 1) * dv])
    u = jnp.concatenate(parts, axis=1).astype(BF16)
    y = jnp.dot(u, w_o_ref[...], preferred_element_type=F32)
    _post_mix(y, *rest, alpha=alpha)


def _mix_out(st, n_tiles, mixer_inputs, mixer_specs, kernel, x, mods, ln_g, ln_b, w_o, wr_hi, wr_lo, rbias):
    d = st.d
    rows = n_tiles * ROW_TILE
    common_specs = [
        _row_spec(d), _mod_spec(st, 2), _const_spec((1, d)), _const_spec((1, d)),
        _mod_spec(st, 3), _mod_spec(st, 4), _const_spec(wr_hi.shape), _const_spec(wr_lo.shape),
        _const_spec((1, LANES)),
    ]
    return pl.pallas_call(
        kernel,
        grid=(n_tiles,),
        in_specs=mixer_specs + [_const_spec(w_o.shape)] + common_specs,
        out_specs=[_row_spec(d), _row_spec(d), _row_spec(LANES)],
        out_shape=[jax.ShapeDtypeStruct((rows, d), F32), jax.ShapeDtypeStruct((rows, d), F32),
                   jax.ShapeDtypeStruct((rows, LANES), F32)],
        compiler_params=_cparams(("parallel",)),
        name="mixer_out_ln_route",
    )(*mixer_inputs, w_o, x, mods, ln_g, ln_b, mods, mods, wr_hi, wr_lo, rbias)


def _rank_kernel(route_ref, rank_ref, counts_ref, run_ref):
    t = pl.program_id(0)

    @pl.when(t == 0)
    def _():
        run_ref[...] = jnp.zeros_like(run_ref)

    route = route_ref[...]
    tm = route.shape[0]
    lane = lax.broadcasted_iota(jnp.int32, route.shape, 1)
    e0 = route[:, 0:1].astype(jnp.int32)
    e1 = route[:, 1:2].astype(jnp.int32)
    oh0 = (lane == e0).astype(F32)
    oh1 = (lane == e1).astype(F32)
    both = oh0 + oh1
    ri = lax.broadcasted_iota(jnp.int32, (tm, tm), 0)
    ci = lax.broadcasted_iota(jnp.int32, (tm, tm), 1)
    tri = (ci < ri).astype(BF16)
    before = jnp.dot(tri, both.astype(BF16), preferred_element_type=F32) + run_ref[0:1, :]
    r0 = jnp.sum(oh0 * before, axis=-1, keepdims=True)
    r1 = jnp.sum(oh1 * before, axis=-1, keepdims=True)
    rank_ref[...] = jnp.where(lane == 0, r0, jnp.where(lane == 1, r1, 0.0)).astype(jnp.int32)
    run_ref[...] = run_ref[...] + jnp.sum(both, axis=0, keepdims=True)
    counts_ref[...] = run_ref[...].astype(jnp.int32)


def _moe_ranks(route):
    n = route.shape[0]
    tm = ROW_TILE
    return pl.pallas_call(
        _rank_kernel,
        grid=(n // tm,),
        in_specs=[_row_spec(LANES)],
        out_specs=[_row_spec(LANES), _const_spec((8, LANES))],
        out_shape=[jax.ShapeDtypeStruct((n, LANES), jnp.int32), jax.ShapeDtypeStruct((8, LANES), jnp.int32)],
        scratch_shapes=[pltpu.VMEM((8, LANES), F32)],
        compiler_params=_cparams(("arbitrary",)),
        name="moe_ranks",
    )(route)


def _dispatch_kernel(slots_ref, pad_start_ref, pad_end_ref, h_ref, xs_ref, zero_ref, sem_ref, zsem_ref):
    t = pl.program_id(0)
    tm = h_ref.shape[0]

    @pl.when(t == 0)
    def _():
        zero_ref[...] = jnp.zeros_like(zero_ref)
        for e in range(MOE_EXPERTS):
            @pl.when(pad_end_ref[e] > pad_start_ref[e])
            def _():
                start = pl.multiple_of(pad_end_ref[e] - MOE_BLOCK, MOE_BLOCK)
                pltpu.make_async_copy(zero_ref, xs_ref.at[pl.ds(start, MOE_BLOCK)], zsem_ref).start()
        n_used = pad_end_ref[MOE_EXPERTS - 1] // MOE_BLOCK
        n_blocks = xs_ref.shape[0] // MOE_BLOCK

        def fill_tail(i, carry):
            start = pl.multiple_of(i * MOE_BLOCK, MOE_BLOCK)
            pltpu.make_async_copy(zero_ref, xs_ref.at[pl.ds(start, MOE_BLOCK)], zsem_ref).start()
            return carry

        def wait_tail(i, carry):
            pltpu.make_async_copy(zero_ref, xs_ref.at[pl.ds(0, MOE_BLOCK)], zsem_ref).wait()
            return carry

        lax.fori_loop(n_used, n_blocks, fill_tail, 0)
        for e in range(MOE_EXPERTS):
            @pl.when(pad_end_ref[e] > pad_start_ref[e])
            def _():
                pltpu.make_async_copy(zero_ref, xs_ref.at[pl.ds(0, MOE_BLOCK)], zsem_ref).wait()
        lax.fori_loop(n_used, n_blocks, wait_tail, 0)

    base = t * (2 * tm)

    def issue(r, carry):
        s0 = slots_ref[base + 2 * r]
        s1 = slots_ref[base + 2 * r + 1]
        pltpu.make_async_copy(h_ref.at[pl.ds(r, 1)], xs_ref.at[pl.ds(s0, 1)], sem_ref.at[0]).start()
        pltpu.make_async_copy(h_ref.at[pl.ds(r, 1)], xs_ref.at[pl.ds(s1, 1)], sem_ref.at[1]).start()
        return carry

    lax.fori_loop(0, tm, issue, 0, unroll=8)
    pltpu.make_async_copy(h_ref, xs_ref.at[pl.ds(0, tm)], sem_ref.at[0]).wait()
    pltpu.make_async_copy(h_ref, xs_ref.at[pl.ds(0, tm)], sem_ref.at[1]).wait()


def _moe_dispatch(h2, slots, pad_start, pad_end, n_rows):
    n, d = h2.shape
    tm = ROW_TILE
    grid_spec = pltpu.PrefetchScalarGridSpec(
        num_scalar_prefetch=3,
        grid=(n // tm,),
        in_specs=[pl.BlockSpec((tm, d), lambda t, *_: (t, 0))],
        out_specs=pl.BlockSpec(memory_space=pl.ANY),
        scratch_shapes=[pltpu.VMEM((MOE_BLOCK, d), F32), pltpu.SemaphoreType.DMA((2,)),
                        pltpu.SemaphoreType.DMA(())],
    )
    return pl.pallas_call(
        _dispatch_kernel,
        grid_spec=grid_spec,
        out_shape=jax.ShapeDtypeStruct((n_rows, d), F32),
        compiler_params=_cparams(("arbitrary",)),
        name="moe_dispatch",
    )(slots, pad_start, pad_end, h2)


def _experts_kernel(block_exp_ref, n_used_ref, xs_ref, w1_ref, w3_ref, w2_ref, o_ref, w1b_ref, w3b_ref, w2b_ref):
    i = pl.program_id(0)

    @pl.when(i < n_used_ref[0])
    def _():
        prev = block_exp_ref[jnp.maximum(i - 1, 0)]

        @pl.when((i == 0) | (block_exp_ref[i] != prev))
        def _():
            w1b_ref[...] = w1_ref[...].astype(BF16)
            w3b_ref[...] = w3_ref[...].astype(BF16)
            w2b_ref[...] = w2_ref[...].astype(BF16)

        xb = xs_ref[...].astype(BF16)
        a = jnp.dot(xb, w1b_ref[...], preferred_element_type=F32)
        g = jnp.dot(xb, w3b_ref[...], preferred_element_type=F32)
        hid = (a * _sigmoid(a) * g).astype(BF16)
        o_ref[...] = jnp.dot(hid, w2b_ref[...], preferred_element_type=F32)

    @pl.when(i >= n_used_ref[0])
    def _():
        o_ref[...] = jnp.zeros_like(o_ref)


def _moe_experts(xs, block_exp, n_used, w1, w3, w2, layer):
    n_rows, d = xs.shape
    n_blocks = n_rows // MOE_BLOCK
    f = w1.shape[-1]

    def blk(i, be, nu):
        return jnp.minimum(i, nu[0] - 1)

    grid_spec = pltpu.PrefetchScalarGridSpec(
        num_scalar_prefetch=2,
        grid=(n_blocks,),
        in_specs=[
            pl.BlockSpec((MOE_BLOCK, d), lambda i, be, nu: (blk(i, be, nu), 0)),
            pl.BlockSpec((None, None, d, f), lambda i, be, nu: (layer, be[blk(i, be, nu)], 0, 0)),
            pl.BlockSpec((None, None, d, f), lambda i, be, nu: (layer, be[blk(i, be, nu)], 0, 0)),
            pl.BlockSpec((None, None, f, d), lambda i, be, nu: (layer, be[blk(i, be, nu)], 0, 0)),
        ],
        out_specs=pl.BlockSpec((MOE_BLOCK, d), lambda i, be, nu: (i, 0)),
        scratch_shapes=[pltpu.VMEM((d, f), BF16), pltpu.VMEM((d, f), BF16), pltpu.VMEM((f, d), BF16)],
    )
    return pl.pallas_call(
        _experts_kernel,
        grid_spec=grid_spec,
        out_shape=jax.ShapeDtypeStruct((n_rows, d), F32),
        compiler_params=_cparams(("arbitrary",)),
        name="moe_experts",
    )(block_exp, n_used, xs, w1, w3, w2)


def _combine_kernel(slots_ref, ys_ref, x1_ref, route_ref, g2_ref, lng_ref, lnb_ref, out_ref, buf_ref, sem_ref,
                    *, alpha):
    t = pl.program_id(0)
    tm = x1_ref.shape[0]
    base = t * (2 * tm)

    def issue(r, carry):
        s0 = slots_ref[base + 2 * r]
        s1 = slots_ref[base + 2 * r + 1]
        pltpu.make_async_copy(ys_ref.at[pl.ds(s0, 1)], buf_ref.at[0, pl.ds(r, 1)], sem_ref.at[0]).start()
        pltpu.make_async_copy(ys_ref.at[pl.ds(s1, 1)], buf_ref.at[1, pl.ds(r, 1)], sem_ref.at[1]).start()
        return carry

    lax.fori_loop(0, tm, issue, 0, unroll=8)
    pltpu.make_async_copy(ys_ref.at[pl.ds(0, tm)], buf_ref.at[0], sem_ref.at[0]).wait()
    pltpu.make_async_copy(ys_ref.at[pl.ds(0, tm)], buf_ref.at[1], sem_ref.at[1]).wait()
    route = route_ref[...]
    y = route[:, 2:3] * buf_ref[0] + route[:, 3:4] * buf_ref[1]
    out_ref[...] = _layer_norm(alpha * x1_ref[...] + g2_ref[...] * y, lng_ref[...], lnb_ref[...])


def _moe_combine(st, ys, slots, x1, route, mods, ln_g, ln_b, alpha):
    n, d = x1.shape
    tm = ROW_TILE
    grid_spec = pltpu.PrefetchScalarGridSpec(
        num_scalar_prefetch=1,
        grid=(n // tm,),
        in_specs=[
            pl.BlockSpec(memory_space=pl.ANY),
            pl.BlockSpec((tm, d), lambda t, s: (t, 0)),
            pl.BlockSpec((tm, LANES), lambda t, s: (t, 0)),
            pl.BlockSpec((None, 1, d), lambda t, s: (st.mod_row(t), 0, 5)),
            pl.BlockSpec((1, d), lambda t, s: (0, 0)),
            pl.BlockSpec((1, d), lambda t, s: (0, 0)),
        ],
        out_specs=pl.BlockSpec((tm, d), lambda t, s: (t, 0)),
        scratch_shapes=[pltpu.VMEM((2, tm, d), F32), pltpu.SemaphoreType.DMA((2,))],
    )
    return pl.pallas_call(
        functools.partial(_combine_kernel, alpha=alpha),
        grid_spec=grid_spec,
        out_shape=jax.ShapeDtypeStruct((n, d), F32),
        compiler_params=_cparams(("arbitrary",)),
        name="moe_combine_ln",
    )(slots, ys, x1, route, mods, ln_g, ln_b)


def _hier_moe(st, h2, route, x1, mods, ln_g, ln_b, w1, w3, w2, layer, alpha):
    n = h2.shape[0]
    ranks, counts = _moe_ranks(route)
    counts = counts[0, :MOE_EXPERTS]
    padded = (counts + MOE_BLOCK - 1) // MOE_BLOCK * MOE_BLOCK
    pad_end = jnp.cumsum(padded).astype(jnp.int32)
    pad_start = pad_end - padded
    expert = route[:, :2].astype(jnp.int32)
    eids = jnp.arange(MOE_EXPERTS, dtype=jnp.int32)
    start_of = jnp.sum(jnp.where(expert[:, :, None] == eids, pad_start, 0), axis=-1)
    slots = (start_of + ranks[:, :2]).reshape(2 * n)
    n_blocks = -(-(2 * n) // MOE_BLOCK) + MOE_EXPERTS
    block_pos = jnp.arange(n_blocks, dtype=jnp.int32) * MOE_BLOCK
    block_exp = jnp.minimum(jnp.sum((pad_end[None, :] <= block_pos[:, None]).astype(jnp.int32), axis=1),
                            MOE_EXPERTS - 1)
    n_used = (pad_end[-1:] // MOE_BLOCK).astype(jnp.int32)
    xs = _moe_dispatch(h2, slots, pad_start, pad_end, n_blocks * MOE_BLOCK)
    ys = _moe_experts(xs, block_exp, n_used, w1, w3, w2, layer)
    return _moe_combine(st, ys, slots, x1, route, mods, ln_g, ln_b, alpha)


def _swap_halves(w):
    half = w.shape[-1] // 2
    return jnp.concatenate([w[..., half:], w[..., :half]], axis=-1)


def _mla_weights(w_in, w_uq, w_ukv, w_o):
    qr, kvr = MLA_Q_RANK, MLA_KV_RANK
    w_pe = w_in[:, qr + kvr:]
    w_in_ext = jnp.concatenate([w_in, _swap_halves(w_pe)], axis=1).astype(BF16)
    uq = w_uq.reshape(qr, MLA_HEADS, MLA_NOPE + MLA_ROPE)
    q_nope = uq[:, :, :MLA_NOPE].reshape(qr, MLA_HEADS * MLA_NOPE)
    q_pe = uq[:, :, MLA_NOPE:]
    q_pe2 = jnp.concatenate([q_pe, _swap_halves(q_pe)], axis=-1).reshape(qr, MLA_HEADS * 2 * MLA_ROPE)
    w_q = jnp.concatenate([q_nope, q_pe2], axis=1).astype(BF16)
    ukv = w_ukv.reshape(kvr, MLA_HEADS, MLA_NOPE + MLA_V)
    w_kv = jnp.concatenate([ukv[:, :, :MLA_NOPE].reshape(kvr, -1), ukv[:, :, MLA_NOPE:].reshape(kvr, -1)],
                           axis=1).astype(BF16)
    return w_in_ext, w_q, w_kv, w_o.astype(BF16)


def _gla_weights(w_in, gate_a, gate_b, gate_bias, w_o):
    d = w_in.shape[0]
    key_dim = gate_b.shape[-1]
    rank = gate_a.shape[-1]
    ga = jnp.concatenate([gate_a[0], gate_a[1], jnp.zeros((d, LANES - 2 * rank), F32)], axis=1)
    w_qkg = jnp.concatenate([w_in[:, :2 * key_dim], ga], axis=1).astype(BF16)
    w_vr = w_in[:, 2 * key_dim:].astype(BF16)
    gb = jnp.zeros((LANES, 2 * key_dim), F32)
    gb = gb.at[:rank, :key_dim].set(gate_b[0]).at[rank:2 * rank, key_dim:].set(gate_b[1])
    bias = jnp.concatenate([gate_bias[0], gate_bias[1]])[None, :]
    return w_qkg, gb.astype(BF16), bias, w_vr, w_o.astype(BF16)


def _rope_table(seq):
    n_rows = seq // GRID_W
    row = jnp.repeat(jnp.arange(n_rows, dtype=F32), GRID_W)
    col = jnp.tile(jnp.arange(GRID_W, dtype=F32), n_rows)
    n_freq = MLA_ROPE // 4
    inv_freq = jnp.power(ROPE_THETA, -jnp.arange(n_freq, dtype=F32) / n_freq)
    ang = jnp.concatenate([row[:, None] * inv_freq, col[:, None] * inv_freq], axis=-1)
    cos, sin = jnp.cos(ang), jnp.sin(ang)
    lat = jnp.concatenate([cos, cos, -sin, sin], axis=-1)
    ident = jnp.concatenate([jnp.ones((ROW_TILE, MLA_ROPE), F32), jnp.zeros((ROW_TILE, MLA_ROPE), F32)], axis=-1)
    return jnp.concatenate([lat, ident], axis=0)


def kernel(x, c, ctx, c_ctx, w_mod, b_mod, ln1_g, ln1_b, ln2_g, ln2_b, mla_w_in, mla_q_norm, mla_w_uq, mla_kv_norm, mla_w_ukv, mla_w_o, gla_w_in, gla_gate_a, gla_gate_b, gla_gate_bias, gla_norm, gla_w_o, moe_w_grp, moe_b_grp, moe_w_exp, moe_b_exp, moe_w1, moe_w3, moe_w2):
    batch, seq, d = x.shape
    ctx_len = ctx.shape[1]
    depth = w_mod.shape[0]
    assert batch + 1 <= 8
    alpha = (2.0 * depth) ** 0.25
    full = _Stream(batch, seq, ctx_len, d)

    cond = jnp.concatenate([c, c_ctx[None, :], jnp.zeros((8 - batch - 1, d), F32)], axis=0)
    mods_all = _adaln_tables(cond, w_mod, b_mod)
    cs_tab = _rope_table(seq)
    xs = jnp.concatenate([x.reshape(batch * seq, d), ctx.reshape(batch * ctx_len, d)], axis=0)

    for i in range(depth):
        last = i == depth - 1
        mods = mods_all[i].reshape(8, 1, N_MOD * d)
        j = i // 2
        n_tiles = full.lat_tiles if last else full.tiles
        wr = jnp.concatenate([moe_w_grp[i], moe_w_exp[i],
                              jnp.zeros((d, LANES - MOE_GROUPS - MOE_EXPERTS), F32)], axis=1)
        wr_hi = wr.astype(BF16)
        wr_lo = (wr - wr_hi.astype(F32)).astype(BF16)
        rbias = jnp.concatenate([moe_b_grp[i], moe_b_exp[i],
                                 jnp.zeros((LANES - MOE_GROUPS - MOE_EXPERTS,), F32)])[None, :]
        ln1 = (ln1_g[i][None, :], ln1_b[i][None, :])
        if i % 2 == 0:
            w_in_ext, w_q, w_kv, w_o = _mla_weights(mla_w_in[j], mla_w_uq[j], mla_w_ukv[j], mla_w_o[j])
            q, k, v = _mla_proj(full, xs, mods, cs_tab, w_in_ext, mla_q_norm[j][None, :],
                                mla_kv_norm[j][None, :], w_q, w_kv)
            o = _attention(full, q, k, v, latent=True)
            if not last:
                o = jnp.concatenate([o, _attention(full, q, k, v, latent=False)], axis=0)
            x1, h2, route = _mix_out(full, n_tiles, [o], [_row_spec(o.shape[1])],
                                     functools.partial(_mla_out_kernel, alpha=alpha),
                                     xs, mods, ln1[0], ln1[1], w_o, wr_hi, wr_lo, rbias)
        else:
            w_qkg, gb_ext, gbias, w_vr, w_o = _gla_weights(gla_w_in[j], gla_gate_a[j], gla_gate_b[j],
                                                          gla_gate_bias[j], gla_w_o[j])
            q, k, v, r, gf, gb = _gla_proj(full, xs, mods, w_qkg, gb_ext, gbias, w_vr)
            o_f = _gla_scan(full, q, k, v, gf, reverse=False)
            o_b = _gla_scan(full, q, k, v, gb, reverse=True)
            vd = v.shape[1]
            x1, h2, route = _mix_out(full, n_tiles, [o_f, o_b, r, gla_norm[j][None, :]],
                                     [_row_spec(vd), _row_spec(vd), _row_spec(vd),
                                      _const_spec((1, vd // GLA_HEADS))],
                                     functools.partial(_gla_out_kernel, alpha=alpha, dv=vd // GLA_HEADS),
                                     xs, mods, ln1[0], ln1[1], w_o, wr_hi, wr_lo, rbias)
        xs = _hier_moe(full, h2, route, x1, mods, ln2_g[i][None, :], ln2_b[i][None, :],
                       moe_w1, moe_w3, moe_w2, i, alpha)
    return xs[:batch * seq].reshape(batch, seq, d)
```

```python
import functools
import math

import jax
import jax.numpy as jnp
import numpy as np
from jax import lax
from jax.experimental import pallas as pl
from jax.experimental.pallas import tpu as pltpu

F32 = jnp.float32
BF16 = jnp.bfloat16

GRID_W = 64
N_MOD = 6
NORM_EPS = 1e-6
MLA_HEADS = 16
MLA_Q_RANK = 768
MLA_KV_RANK = 256
MLA_NOPE = 128
MLA_ROPE = 64
MLA_V = 128
ROPE_THETA = 10000.0
GLA_HEADS = 4
GLA_GATE_RANK = 16
GLA_TAU = 16.0
GLA_CHUNK = 64
GLA_SUB = 16
MOE_GROUPS = 4
MOE_PER_GROUP = 8
MOE_EXPERTS = MOE_GROUPS * MOE_PER_GROUP
MOE_BLOCK = 256

LANES = 128
VMEM_LIMIT = 56 * 1024 * 1024

ROW_TILE = 256
ATTN_TQ = 512
ATTN_TK = 1024


def _cparams(sem):
    return pltpu.CompilerParams(dimension_semantics=sem, vmem_limit_bytes=VMEM_LIMIT)


def _sigmoid(x):
    return 1.0 / (1.0 + jnp.exp(-x))


def _layer_norm(z, g, b):
    mu = jnp.mean(z, axis=-1, keepdims=True)
    zc = z - mu
    var = jnp.mean(zc * zc, axis=-1, keepdims=True)
    return zc * lax.rsqrt(var + NORM_EPS) * g + b


def _rms_norm(z, g):
    return z * lax.rsqrt(jnp.mean(z * z, axis=-1, keepdims=True) + NORM_EPS) * g


def _mods_kernel(cond_ref, w_ref, b_ref, o_ref):
    c = cond_ref[...]
    s = c * _sigmoid(c)
    o_ref[...] = jnp.dot(s, w_ref[...], preferred_element_type=F32,
                         precision=lax.Precision.HIGHEST) + b_ref[...]


def _adaln_tables(cond, w_mod, b_mod):
    depth, d, n6 = w_mod.shape
    tn = 1024
    return pl.pallas_call(
        _mods_kernel,
        grid=(depth, n6 // tn),
        in_specs=[
            pl.BlockSpec((8, d), lambda l, j: (0, 0)),
            pl.BlockSpec((None, d, tn), lambda l, j: (l, 0, j)),
            pl.BlockSpec((None, 1, tn), lambda l, j: (l, 0, j)),
        ],
        out_specs=pl.BlockSpec((None, 8, tn), lambda l, j: (l, 0, j)),
        out_shape=jax.ShapeDtypeStruct((depth, 8, n6), F32),
        compiler_params=_cparams(("parallel", "parallel")),
        name="adaln_tables",
    )(cond, w_mod, b_mod.reshape(depth, 1, n6))


class _Stream:
    def __init__(self, batch, seq, ctx_len, d):
        self.batch, self.seq, self.ctx_len, self.d = batch, seq, ctx_len, d
        self.n_lat = batch * seq
        self.n_ctx = batch * ctx_len
        self.n = self.n_lat + self.n_ctx
        assert seq % ROW_TILE == 0 and ctx_len % ROW_TILE == 0
        self.lat_tiles = self.n_lat // ROW_TILE
        self.tiles = self.n // ROW_TILE
        self.tiles_per_batch = seq // ROW_TILE

    def mod_row(self, t):
        return jnp.where(t < self.lat_tiles, t // self.tiles_per_batch, self.batch)


def _mod_spec(st, chunk):
    d = st.d
    return pl.BlockSpec((None, 1, d), lambda t: (st.mod_row(t), 0, chunk))


def _row_spec(width, tm=ROW_TILE):
    return pl.BlockSpec((tm, width), lambda t: (t, 0))


def _const_spec(shape):
    nd = len(shape)
    return pl.BlockSpec(shape, lambda t: (0,) * nd, pipeline_mode=pl.Buffered(1))


def _half_sum(x):
    return x + pltpu.roll(x, LANES // 2, axis=1)


def _mla_proj_kernel(x_ref, sh_ref, sc_ref, cs_ref, w_in_ref, qg_ref, kvg_ref, w_q_ref, w_kv_ref,
                     q_ref, k_ref, v_ref, *, scale):
    h = x_ref[...] * (1.0 + sc_ref[...]) + sh_ref[...]
    lat = jnp.dot(h.astype(BF16), w_in_ref[...], preferred_element_type=F32)
    cq = _rms_norm(lat[:, :MLA_Q_RANK], qg_ref[...]).astype(BF16)
    ckv = _rms_norm(lat[:, MLA_Q_RANK:MLA_Q_RANK + MLA_KV_RANK], kvg_ref[...]).astype(BF16)
    cs = cs_ref[...]
    lane = lax.broadcasted_iota(jnp.int32, cs.shape, 1)
    kpe = _half_sum(lat[:, MLA_Q_RANK + MLA_KV_RANK:] * cs)
    kpe = jnp.where(lane < MLA_ROPE, kpe, 0.0).astype(BF16)
    qall = jnp.dot(cq, w_q_ref[...], preferred_element_type=F32)
    kvall = jnp.dot(ckv, w_kv_ref[...], preferred_element_type=F32)
    hn = MLA_HEADS * MLA_NOPE
    for hd in range(MLA_HEADS):
        lo = hd * LANES
        q_ref[hd, :, :MLA_NOPE] = (qall[:, lo:lo + LANES] * scale).astype(BF16)
        q_ref[hd, :, MLA_NOPE:] = (_half_sum(qall[:, hn + lo:hn + lo + LANES] * cs) * scale).astype(BF16)
        k_ref[hd, :, :MLA_NOPE] = kvall[:, lo:lo + LANES].astype(BF16)
        k_ref[hd, :, MLA_NOPE:] = kpe
        v_ref[hd] = jnp.transpose(kvall[:, hn + lo:hn + lo + LANES]).astype(BF16)


def _mla_proj(st, x, mods, cs_tab, w_in_ext, q_norm, kv_norm, w_q, w_kv):
    n, d = st.n, st.d
    tm = ROW_TILE
    lat_w = w_in_ext.shape[1]
    qk_w = MLA_NOPE + LANES
    scale = (MLA_NOPE + MLA_ROPE) ** -0.5 * math.log2(math.e)
    seq_tiles = st.tiles_per_batch

    def cs_map(t):
        return (jnp.where(t < st.lat_tiles, t % seq_tiles, seq_tiles), 0)

    head_spec = lambda w: pl.BlockSpec((MLA_HEADS, tm, w), lambda t: (0, t, 0))
    return pl.pallas_call(
        functools.partial(_mla_proj_kernel, scale=scale),
        grid=(st.tiles,),
        in_specs=[
            _row_spec(d), _mod_spec(st, 0), _mod_spec(st, 1),
            pl.BlockSpec((tm, LANES), cs_map),
            _const_spec((d, lat_w)), _const_spec((1, MLA_Q_RANK)), _const_spec((1, MLA_KV_RANK)),
            _const_spec(w_q.shape), _const_spec(w_kv.shape),
        ],
        out_specs=[head_spec(qk_w), head_spec(qk_w),
                   pl.BlockSpec((MLA_HEADS, MLA_V, tm), lambda t: (0, 0, t))],
        out_shape=[
            jax.ShapeDtypeStruct((MLA_HEADS, n, qk_w), BF16),
            jax.ShapeDtypeStruct((MLA_HEADS, n, qk_w), BF16),
            jax.ShapeDtypeStruct((MLA_HEADS, MLA_V, n), BF16),
        ],
        compiler_params=_cparams(("parallel",)),
        name="mla_proj",
    )(x, mods, mods, cs_tab, w_in_ext, q_norm, kv_norm, w_q, w_kv)


def _attn_kernel(q_ref, kc_ref, vct_ref, *rest, n_chunks):
    if n_chunks:
        kl_ref, vlt_ref, o_ref = rest
    else:
        (o_ref,) = rest
    q = q_ref[...]
    nt = (((1,), (1,)), ((), ()))

    def scores(k):
        s = lax.dot_general(k, q, nt, preferred_element_type=F32)
        return s, jnp.max(s, axis=0, keepdims=True)

    def chunk_rows(j):
        return pl.ds(j * ATTN_TK, ATTN_TK)

    s, m = scores(kc_ref[...])
    nxt = scores(kl_ref[chunk_rows(0), :]) if n_chunks else None
    p = jnp.exp2(s - m)
    l = jnp.sum(p, axis=0, keepdims=True)
    acc = jnp.dot(vct_ref[...], p.astype(BF16), preferred_element_type=F32)
    for j in range(n_chunks):
        s, m_blk = nxt
        if j + 1 < n_chunks:
            nxt = scores(kl_ref[chunk_rows(j + 1), :])
        m_new = jnp.maximum(m, m_blk)
        alpha = jnp.exp2(m - m_new)
        p = jnp.exp2(s - m_new)
        l = alpha * l + jnp.sum(p, axis=0, keepdims=True)
        acc = alpha * acc + jnp.dot(vlt_ref[:, chunk_rows(j)], p.astype(BF16), preferred_element_type=F32)
        m = m_new
    o_ref[...] = jnp.transpose(acc / l).astype(o_ref.dtype)


def _attention(st, q, k, vt, latent):
    b, seq, lc = st.batch, st.seq, st.ctx_len
    qk_w = q.shape[-1]
    tq = ATTN_TQ if latent else min(ATTN_TQ, lc)
    ctx_blk0 = st.n_lat // lc
    if latent:
        nq = seq // tq
        q_map = lambda bi, h, qi: (h, bi * nq + qi, 0)
        o_map = lambda bi, h, qi: (bi * nq + qi, h)
        rows = st.n_lat
        n_chunks = seq // ATTN_TK
    else:
        nq = lc // tq
        q_map = lambda bi, h, qi: (h, st.n_lat // tq + bi * nq + qi, 0)
        o_map = lambda bi, h, qi: (bi * nq + qi, h)
        rows = st.n_ctx
        n_chunks = 0
    kc_map = lambda bi, h, qi: (h, ctx_blk0 + bi, 0)
    kl_map = lambda bi, h, qi: (h, bi, 0)
    vc_map = lambda bi, h, qi: (h, 0, ctx_blk0 + bi)
    vl_map = lambda bi, h, qi: (h, 0, bi)
    in_specs = [
        pl.BlockSpec((None, tq, qk_w), q_map),
        pl.BlockSpec((None, lc, qk_w), kc_map),
        pl.BlockSpec((None, MLA_V, lc), vc_map),
    ]
    args = [q, k, vt]
    if latent:
        in_specs += [pl.BlockSpec((None, seq, qk_w), kl_map), pl.BlockSpec((None, MLA_V, seq), vl_map)]
        args += [k, vt]
    return pl.pallas_call(
        functools.partial(_attn_kernel, n_chunks=n_chunks),
        grid=(b, MLA_HEADS, nq),
        in_specs=in_specs,
        out_specs=pl.BlockSpec((tq, MLA_V), o_map),
        out_shape=jax.ShapeDtypeStruct((rows, MLA_HEADS * MLA_V), BF16),
        compiler_params=_cparams(("parallel", "parallel", "arbitrary")),
        name="mla_attention_latent" if latent else "mla_attention_context",
    )(*args)


def _gla_qkg_kernel(x_ref, sh_ref, sc_ref, w_ref, gb_ref, bias_ref, q_ref, k_ref, gf_ref, gb_out_ref,
                    *, key_dim, q_scale):
    h = (x_ref[...] * (1.0 + sc_ref[...]) + sh_ref[...]).astype(BF16)
    y = jnp.dot(h, w_ref[...], preferred_element_type=F32)
    q_ref[...] = (y[:, :key_dim] * q_scale).astype(BF16)
    k_ref[...] = y[:, key_dim:2 * key_dim].astype(BF16)
    z = jnp.dot(y[:, 2 * key_dim:].astype(BF16), gb_ref[...], preferred_element_type=F32) + bias_ref[...]
    g = (jnp.minimum(z, 0.0) - jnp.log(1.0 + jnp.exp(-jnp.abs(z)))) * (math.log2(math.e) / GLA_TAU)
    gf_ref[...] = g[:, :key_dim]
    gb_out_ref[...] = g[:, key_dim:]


def _gla_vr_kernel(x_ref, sh_ref, sc_ref, w_ref, v_ref, r_ref, *, value_dim):
    h = (x_ref[...] * (1.0 + sc_ref[...]) + sh_ref[...]).astype(BF16)
    y = jnp.dot(h, w_ref[...], preferred_element_type=F32)
    v_ref[...] = y[:, :value_dim].astype(BF16)
    r_ref[...] = y[:, value_dim:]


def _gla_proj(st, x, mods, w_qkg, gate_b_ext, gate_bias, w_vr):
    n, d = st.n, st.d
    key_dim = (w_qkg.shape[1] - LANES) // 2
    value_dim = w_vr.shape[1] // 2
    dk = key_dim // GLA_HEADS
    q, k, gf, gb = pl.pallas_call(
        functools.partial(_gla_qkg_kernel, key_dim=key_dim, q_scale=dk ** -0.5),
        grid=(st.tiles,),
        in_specs=[_row_spec(d), _mod_spec(st, 0), _mod_spec(st, 1), _const_spec(w_qkg.shape),
                  _const_spec(gate_b_ext.shape), _const_spec(gate_bias.shape)],
        out_specs=[_row_spec(key_dim)] * 4,
        out_shape=[jax.ShapeDtypeStruct((n, key_dim), BF16), jax.ShapeDtypeStruct((n, key_dim), BF16),
                   jax.ShapeDtypeStruct((n, key_dim), F32), jax.ShapeDtypeStruct((n, key_dim), F32)],
        compiler_params=_cparams(("parallel",)),
        name="gla_proj_qkg",
    )(x, mods, mods, w_qkg, gate_b_ext, gate_bias)
    v, r = pl.pallas_call(
        functools.partial(_gla_vr_kernel, value_dim=value_dim),
        grid=(st.tiles,),
        in_specs=[_row_spec(d), _mod_spec(st, 0), _mod_spec(st, 1), _const_spec(w_vr.shape)],
        out_specs=[_row_spec(value_dim)] * 2,
        out_shape=[jax.ShapeDtypeStruct((n, value_dim), BF16), jax.ShapeDtypeStruct((n, value_dim), F32)],
        compiler_params=_cparams(("parallel",)),
        name="gla_proj_vr",
    )(x, mods, mods, w_vr)
    return q, k, v, r, gf, gb


def _cumsum_rows(x, reverse):
    n = x.shape[0]
    row = lax.broadcasted_iota(jnp.int32, x.shape, 0)
    s = 1
    while s < n:
        if reverse:
            x = x + jnp.where(row < n - s, pltpu.roll(x, n - s, axis=0), 0.0)
        else:
            x = x + jnp.where(row >= s, pltpu.roll(x, s, axis=0), 0.0)
        s *= 2
    return x


def _gla_chunk(q, k, v, g, state_t, reverse):
    c, dk = q.shape
    nsub = c // GLA_SUB
    half = GLA_SUB // 2
    b = _cumsum_rows(g, reverse)
    b_last = jnp.sum(g, axis=0, keepdims=True)
    nt = (((1,), (1,)), ((), ()))
    qe = (q * jnp.exp2(b)).astype(BF16)
    o = lax.dot_general(qe, state_t.astype(BF16), nt, preferred_element_type=F32)

    q_parts, k_parts = [], []
    for src in (range(1, nsub) if reverse else range(nsub - 1)):
        lo = src * GLA_SUB
        if reverse:
            ref = b[lo:lo + 1, :]
            qd = (q[:lo] * jnp.exp2(b[:lo] - ref)).astype(BF16)
            q_parts.append(jnp.concatenate([qd, jnp.zeros((c - lo, dk), BF16)], axis=0))
        else:
            ref = b[lo + GLA_SUB - 1:lo + GLA_SUB, :]
            qd = (q[lo + GLA_SUB:] * jnp.exp2(b[lo + GLA_SUB:] - ref)).astype(BF16)
            q_parts.append(jnp.concatenate([jnp.zeros((lo + GLA_SUB, dk), BF16), qd], axis=0))
        kd = (k[lo:lo + GLA_SUB] * jnp.exp2(ref - b[lo:lo + GLA_SUB])).astype(BF16)
        pieces = [kd]
        if lo:
            pieces.insert(0, jnp.zeros((lo, dk), BF16))
        if c - lo - GLA_SUB:
            pieces.append(jnp.zeros((c - lo - GLA_SUB, dk), BF16))
        k_parts.append(jnp.concatenate(pieces, axis=0))
    off = lax.dot_general(jnp.concatenate(q_parts, axis=1), jnp.concatenate(k_parts, axis=1), nt,
                          preferred_element_type=F32)
    a_mat = jnp.concatenate([off, jnp.zeros((c, LANES - c), F32)], axis=1)

    ones = jnp.ones((dk, LANES), BF16)
    sub_r = lax.broadcasted_iota(jnp.int32, (half, LANES), 0)
    sub_c = lax.broadcasted_iota(jnp.int32, (half, LANES), 1)
    diag_rows = []
    for a in range(nsub):
        lo = a * GLA_SUB
        terms, spans = [], []
        for jj in range(GLA_SUB):
            if reverse:
                r0, r1 = 0, (half if jj < half else GLA_SUB)
            else:
                r0, r1 = (0 if jj < half else half), GLA_SUB
            kj = k[lo + jj:lo + jj + 1, :]
            bj = b[lo + jj:lo + jj + 1, :]
            terms.append(q[lo + r0:lo + r1] * kj * jnp.exp2(b[lo + r0:lo + r1] - bj))
            spans.append((r0, r1))
        sums = jnp.dot(jnp.concatenate(terms, axis=0).astype(BF16), ones, preferred_element_type=F32)
        blk = [jnp.zeros((half, LANES), F32), jnp.zeros((half, LANES), F32)]
        pos = 0
        for jj, (r0, r1) in enumerate(spans):
            for r in range(r0, r1, half):
                piece = sums[pos:pos + half, :]
                pos += half
                rows = sub_r + r
                keep = (sub_c == lo + jj) & ((rows <= jj) if reverse else (rows >= jj))
                blk[r // half] = blk[r // half] + jnp.where(keep, piece, 0.0)
        diag_rows += blk
    a_mat = a_mat + jnp.concatenate(diag_rows, axis=0)
    o = o + jnp.dot(a_mat[:, :c].astype(BF16), v, preferred_element_type=F32)
    kd = (k * jnp.exp2(b_last - b)).astype(BF16)
    tn = (((0,), (0,)), ((), ()))
    new_state = state_t * jnp.exp2(b_last) + lax.dot_general(v, kd, tn, preferred_element_type=F32)
    return o, new_state


def _gla_scan_kernel(qf_ref, kf_ref, vf_ref, gf_ref, qb_ref, kb_ref, vb_ref, gb_ref, of_ref, ob_ref,
                     sf_ref, sb_ref, *, n_chunks):
    @pl.when(pl.program_id(2) == 0)
    def _():
        sf_ref[...] = jnp.zeros_like(sf_ref)
        sb_ref[...] = jnp.zeros_like(sb_ref)

    def step(q_ref, k_ref, v_ref, g_ref, o_ref, state_ref, cidx, reverse):
        rows = pl.ds(cidx * GLA_CHUNK, GLA_CHUNK)
        o, new_state = _gla_chunk(q_ref[rows, :].astype(F32), k_ref[rows, :].astype(F32), v_ref[rows, :],
                                  g_ref[rows, :], state_ref[...], reverse)
        o_ref[rows, :] = o
        state_ref[...] = new_state

    for cidx in range(n_chunks):
        step(qf_ref, kf_ref, vf_ref, gf_ref, of_ref, sf_ref, cidx, False)
        step(qb_ref, kb_ref, vb_ref, gb_ref, ob_ref, sb_ref, n_chunks - 1 - cidx, True)


def _gla_scan(st, q, k, v, gf, gb):
    b = st.batch
    key_dim, value_dim = q.shape[1], v.shape[1]
    dk, dv = key_dim // GLA_HEADS, value_dim // GLA_HEADS
    t_rows = ROW_TILE
    nc, nl = st.ctx_len // t_rows, st.seq // t_rows

    def row_block(bi, s, reverse):
        if reverse:
            ctx = st.lat_tiles + bi * nc + (nc - 1 - s)
            lat = bi * nl + (nl - 1 - (s - nc))
        else:
            ctx = st.lat_tiles + bi * nc + s
            lat = bi * nl + (s - nc)
        return jnp.where(s < nc, ctx, lat)

    def spec(w, reverse):
        return pl.BlockSpec((t_rows, w), lambda bi, h, s: (row_block(bi, s, reverse), h))

    dir_specs = lambda reverse: [spec(dk, reverse), spec(dk, reverse), spec(dv, reverse), spec(dk, reverse)]
    out = jax.ShapeDtypeStruct((st.n, value_dim), F32)
    return pl.pallas_call(
        functools.partial(_gla_scan_kernel, n_chunks=t_rows // GLA_CHUNK),
        grid=(b, GLA_HEADS, nc + nl),
        in_specs=dir_specs(False) + dir_specs(True),
        out_specs=[spec(dv, False), spec(dv, True)],
        out_shape=[out, out],
        scratch_shapes=[pltpu.VMEM((dv, dk), F32), pltpu.VMEM((dv, dk), F32)],
        compiler_params=_cparams(("parallel", "parallel", "arbitrary")),
        name="gla_scan_bidir",
    )(q, k, v, gf, q, k, v, gb)


def _route(logits, n_real):
    lane = lax.broadcasted_iota(jnp.int32, logits.shape, 1)
    neg = jnp.float32(-jnp.inf)
    big = jnp.int32(2 ** 30)
    is_grp = lane < MOE_GROUPS
    gl = jnp.where(is_grp, logits, neg)
    gmax = jnp.max(gl, axis=-1, keepdims=True)
    g_top = jnp.min(jnp.where(gl == gmax, lane, big), axis=-1, keepdims=True)
    p_grp = 1.0 / jnp.sum(jnp.exp(gl - gmax), axis=-1, keepdims=True)
    first = MOE_GROUPS + g_top * MOE_PER_GROUP
    in_grp = (lane >= first) & (lane < first + MOE_PER_GROUP)
    el = jnp.where(in_grp, logits, neg)
    emax = jnp.max(el, axis=-1, keepdims=True)
    pe = jnp.exp(el - emax)
    pe = pe / jnp.sum(pe, axis=-1, keepdims=True)
    v1 = jnp.max(pe, axis=-1, keepdims=True)
    i1 = jnp.min(jnp.where(in_grp & (pe == v1), lane, big), axis=-1, keepdims=True)
    rest = jnp.where(in_grp & (lane != i1), pe, -1.0)
    v2 = jnp.max(rest, axis=-1, keepdims=True)
    i2 = jnp.min(jnp.where(rest == v2, lane, big), axis=-1, keepdims=True)
    denom = v1 + v2
    w1 = p_grp * v1 / denom
    w2 = p_grp * v2 / denom
    e1 = (i1 - MOE_GROUPS).astype(F32)
    e2 = (i2 - MOE_GROUPS).astype(F32)
    del n_real
    return jnp.where(lane == 0, e1, jnp.where(lane == 1, e2, jnp.where(lane == 2, w1, jnp.where(lane == 3, w2, 0.0))))


def _pack_bf16_pairs(x):
    half = x.shape[1] // 2
    bits = lax.bitcast_convert_type(x.astype(BF16).astype(F32), jnp.uint32)
    return bits[:, half:] | (bits[:, :half] >> 16)


def _unpack_bf16_pairs(w):
    lo = lax.bitcast_convert_type(w << 16, F32)
    hi = lax.bitcast_convert_type(w & jnp.uint32(0xFFFF0000), F32)
    return jnp.concatenate([lo, hi], axis=1)


MIX_GROUPS = 2


def _row_groups(tm):
    rows = tm // MIX_GROUPS
    return [pl.ds(i * rows, rows) for i in range(MIX_GROUPS)]


def _post_mix(y, rows, x_ref, g1_ref, lng_ref, lnb_ref, sh2_ref, sc2_ref, wr_hi_ref, wr_lo_ref, rb_ref,
              x1_ref, h2_ref, route_ref, alpha):
    x1 = _layer_norm(alpha * x_ref[rows, :] + g1_ref[...] * y, lng_ref[...], lnb_ref[...])
    x1_ref[rows, :] = x1
    h2 = x1 * (1.0 + sc2_ref[...]) + sh2_ref[...]
    h2_ref[rows, :] = _pack_bf16_pairs(h2)
    hi = h2.astype(BF16)
    lo = (h2 - hi.astype(F32)).astype(BF16)
    w_hi = wr_hi_ref[...]
    logits = (jnp.dot(hi, w_hi, preferred_element_type=F32) + jnp.dot(lo, w_hi, preferred_element_type=F32)
              + jnp.dot(hi, wr_lo_ref[...], preferred_element_type=F32)) + rb_ref[...]
    route_ref[rows, :] = _route(logits, None)


def _mla_out_kernel(o_ref, w_o_ref, *rest, alpha):
    groups = _row_groups(o_ref.shape[0])
    ys = [jnp.dot(o_ref[rows, :], w_o_ref[...], preferred_element_type=F32) for rows in groups]
    for rows, y in zip(groups, ys):
        _post_mix(y, rows, *rest, alpha=alpha)


def _gla_out_kernel(of_ref, ob_ref, r_ref, ng_ref, w_o_ref, *rest, alpha, dv):
    groups = _row_groups(of_ref.shape[0])
    ng = ng_ref[...]
    ys = []
    for rows in groups:
        o = of_ref[rows, :] + ob_ref[rows, :]
        r = r_ref[rows, :]
        gate = r * _sigmoid(r)
        parts = []
        for hd in range(GLA_HEADS):
            parts.append(_rms_norm(o[:, hd * dv:(hd + 1) * dv], ng) * gate[:, hd * dv:(hd + 1) * dv])
        u = jnp.concatenate(parts, axis=1).astype(BF16)
        ys.append(jnp.dot(u, w_o_ref[...], preferred_element_type=F32))
    for rows, y in zip(groups, ys):
        _post_mix(y, rows, *rest, alpha=alpha)


def _mix_out(st, n_tiles, mixer_inputs, mixer_specs, kernel, x, mods, ln_g, ln_b, w_o, wr_hi, wr_lo, rbias):
    d = st.d
    rows = n_tiles * ROW_TILE
    common_specs = [
        _row_spec(d), _mod_spec(st, 2), _const_spec((1, d)), _const_spec((1, d)),
        _mod_spec(st, 3), _mod_spec(st, 4), _const_spec(wr_hi.shape), _const_spec(wr_lo.shape),
        _const_spec((1, LANES)),
    ]
    return pl.pallas_call(
        kernel,
        grid=(n_tiles,),
        in_specs=mixer_specs + [_const_spec(w_o.shape)] + common_specs,
        out_specs=[_row_spec(d), _row_spec(d // 2), _row_spec(LANES)],
        out_shape=[jax.ShapeDtypeStruct((rows, d), F32), jax.ShapeDtypeStruct((rows, d // 2), jnp.uint32),
                   jax.ShapeDtypeStruct((rows, LANES), F32)],
        compiler_params=_cparams(("parallel",)),
        name="mixer_out_ln_route",
    )(*mixer_inputs, w_o, x, mods, ln_g, ln_b, mods, mods, wr_hi, wr_lo, rbias)


def _rank_kernel(route_ref, rank_ref, counts_ref, run_ref):
    t = pl.program_id(0)

    @pl.when(t == 0)
    def _():
        run_ref[...] = jnp.zeros_like(run_ref)

    route = route_ref[...]
    tm = route.shape[0]
    lane = lax.broadcasted_iota(jnp.int32, route.shape, 1)
    e0 = route[:, 0:1].astype(jnp.int32)
    e1 = route[:, 1:2].astype(jnp.int32)
    oh0 = (lane == e0).astype(F32)
    oh1 = (lane == e1).astype(F32)
    both = oh0 + oh1
    ri = lax.broadcasted_iota(jnp.int32, (tm, tm), 0)
    ci = lax.broadcasted_iota(jnp.int32, (tm, tm), 1)
    tri = (ci < ri).astype(BF16)
    before = jnp.dot(tri, both.astype(BF16), preferred_element_type=F32) + run_ref[0:1, :]
    r0 = jnp.sum(oh0 * before, axis=-1, keepdims=True)
    r1 = jnp.sum(oh1 * before, axis=-1, keepdims=True)
    rank_ref[...] = jnp.where(lane == 0, r0, jnp.where(lane == 1, r1, 0.0)).astype(jnp.int32)
    run_ref[...] = run_ref[...] + jnp.sum(both, axis=0, keepdims=True)
    counts_ref[...] = run_ref[...].astype(jnp.int32)


def _moe_ranks(route):
    n = route.shape[0]
    tm = ROW_TILE
    return pl.pallas_call(
        _rank_kernel,
        grid=(n // tm,),
        in_specs=[_row_spec(LANES)],
        out_specs=[_row_spec(LANES), _const_spec((8, LANES))],
        out_shape=[jax.ShapeDtypeStruct((n, LANES), jnp.int32), jax.ShapeDtypeStruct((8, LANES), jnp.int32)],
        scratch_shapes=[pltpu.VMEM((8, LANES), F32)],
        compiler_params=_cparams(("arbitrary",)),
        name="moe_ranks",
    )(route)


def _dispatch_kernel(slots_ref, pad_start_ref, pad_end_ref, h_ref, xs_ref, zero_ref, sem_ref, zsem_ref):
    t = pl.program_id(0)
    tm = h_ref.shape[0]

    @pl.when(t == 0)
    def _():
        zero_ref[...] = jnp.zeros_like(zero_ref)
        for e in range(MOE_EXPERTS):
            @pl.when(pad_end_ref[e] > pad_start_ref[e])
            def _():
                start = pl.multiple_of(pad_end_ref[e] - MOE_BLOCK, MOE_BLOCK)
                pltpu.make_async_copy(zero_ref, xs_ref.at[pl.ds(start, MOE_BLOCK)], zsem_ref).start()
        n_used = pad_end_ref[MOE_EXPERTS - 1] // MOE_BLOCK
        n_blocks = xs_ref.shape[0] // MOE_BLOCK

        def fill_tail(i, carry):
            start = pl.multiple_of(i * MOE_BLOCK, MOE_BLOCK)
            pltpu.make_async_copy(zero_ref, xs_ref.at[pl.ds(start, MOE_BLOCK)], zsem_ref).start()
            return carry

        def wait_tail(i, carry):
            pltpu.make_async_copy(zero_ref, xs_ref.at[pl.ds(0, MOE_BLOCK)], zsem_ref).wait()
            return carry

        lax.fori_loop(n_used, n_blocks, fill_tail, 0)
        for e in range(MOE_EXPERTS):
            @pl.when(pad_end_ref[e] > pad_start_ref[e])
            def _():
                pltpu.make_async_copy(zero_ref, xs_ref.at[pl.ds(0, MOE_BLOCK)], zsem_ref).wait()
        lax.fori_loop(n_used, n_blocks, wait_tail, 0)

    base = t * (2 * tm)

    def issue(r, carry):
        s0 = slots_ref[base + 2 * r]
        s1 = slots_ref[base + 2 * r + 1]
        pltpu.make_async_copy(h_ref.at[pl.ds(r, 1)], xs_ref.at[pl.ds(s0, 1)], sem_ref.at[0]).start()
        pltpu.make_async_copy(h_ref.at[pl.ds(r, 1)], xs_ref.at[pl.ds(s1, 1)], sem_ref.at[1]).start()
        return carry

    lax.fori_loop(0, tm, issue, 0, unroll=8)
    pltpu.make_async_copy(h_ref, xs_ref.at[pl.ds(0, tm)], sem_ref.at[0]).wait()
    pltpu.make_async_copy(h_ref, xs_ref.at[pl.ds(0, tm)], sem_ref.at[1]).wait()


def _moe_dispatch(h2, slots, pad_start, pad_end, n_rows):
    n, d = h2.shape
    tm = ROW_TILE
    grid_spec = pltpu.PrefetchScalarGridSpec(
        num_scalar_prefetch=3,
        grid=(n // tm,),
        in_specs=[pl.BlockSpec((tm, d), lambda t, *_: (t, 0))],
        out_specs=pl.BlockSpec(memory_space=pl.ANY),
        scratch_shapes=[pltpu.VMEM((MOE_BLOCK, d), h2.dtype), pltpu.SemaphoreType.DMA((2,)),
                        pltpu.SemaphoreType.DMA(())],
    )
    return pl.pallas_call(
        _dispatch_kernel,
        grid_spec=grid_spec,
        out_shape=jax.ShapeDtypeStruct((n_rows, d), h2.dtype),
        compiler_params=_cparams(("arbitrary",)),
        name="moe_dispatch",
    )(slots, pad_start, pad_end, h2)


def _experts_kernel(block_exp_ref, n_used_ref, xs_ref, w1_ref, w3_ref, w2_ref, o_ref, w1b_ref, w3b_ref, w2b_ref):
    i = pl.program_id(0)

    @pl.when(i < n_used_ref[0])
    def _():
        prev = block_exp_ref[jnp.maximum(i - 1, 0)]

        @pl.when((i == 0) | (block_exp_ref[i] != prev))
        def _():
            w1b_ref[...] = w1_ref[...].astype(BF16)
            w3b_ref[...] = w3_ref[...].astype(BF16)
            w2b_ref[...] = w2_ref[...].astype(BF16)

        xb = _unpack_bf16_pairs(xs_ref[...]).astype(BF16)
        a = jnp.dot(xb, w1b_ref[...], preferred_element_type=F32)
        g = jnp.dot(xb, w3b_ref[...], preferred_element_type=F32)
        hid = (a * _sigmoid(a) * g).astype(BF16)
        o_ref[...] = _pack_bf16_pairs(jnp.dot(hid, w2b_ref[...], preferred_element_type=F32))

    @pl.when(i >= n_used_ref[0])
    def _():
        o_ref[...] = jnp.zeros_like(o_ref)


def _moe_experts(xs, block_exp, n_used, w1, w3, w2, layer):
    n_rows, dp = xs.shape
    n_blocks = n_rows // MOE_BLOCK
    d, f = w1.shape[-2], w1.shape[-1]
    assert dp * 2 == d

    def blk(i, be, nu):
        return jnp.minimum(i, nu[0] - 1)

    grid_spec = pltpu.PrefetchScalarGridSpec(
        num_scalar_prefetch=2,
        grid=(n_blocks,),
        in_specs=[
            pl.BlockSpec((MOE_BLOCK, dp), lambda i, be, nu: (blk(i, be, nu), 0)),
            pl.BlockSpec((None, None, d, f), lambda i, be, nu: (layer, be[blk(i, be, nu)], 0, 0)),
            pl.BlockSpec((None, None, d, f), lambda i, be, nu: (layer, be[blk(i, be, nu)], 0, 0)),
            pl.BlockSpec((None, None, f, d), lambda i, be, nu: (layer, be[blk(i, be, nu)], 0, 0)),
        ],
        out_specs=pl.BlockSpec((MOE_BLOCK, dp), lambda i, be, nu: (i, 0)),
        scratch_shapes=[pltpu.VMEM((d, f), BF16), pltpu.VMEM((d, f), BF16), pltpu.VMEM((f, d), BF16)],
    )
    return pl.pallas_call(
        _experts_kernel,
        grid_spec=grid_spec,
        out_shape=jax.ShapeDtypeStruct((n_rows, dp), xs.dtype),
        compiler_params=_cparams(("arbitrary",)),
        name="moe_experts",
    )(block_exp, n_used, xs, w1, w3, w2)


def _combine_kernel(slots_ref, ys_ref, x1_ref, route_ref, g2_ref, lng_ref, lnb_ref, out_ref, buf_ref, sem_ref,
                    *, alpha):
    t = pl.program_id(0)
    n_tiles = pl.num_programs(0)
    tm = x1_ref.shape[0]

    def gather_tile(tile, slot):
        base = tile * (2 * tm)

        def issue(r, carry):
            for k in range(2):
                src = ys_ref.at[pl.ds(slots_ref[base + 2 * r + k], 1)]
                pltpu.make_async_copy(src, buf_ref.at[slot, k, pl.ds(r, 1)], sem_ref.at[slot, k]).start()
            return carry

        lax.fori_loop(0, tm, issue, 0, unroll=8)

    @pl.when(t == 0)
    def _():
        gather_tile(0, 0)

    @pl.when(t + 1 < n_tiles)
    def _():
        gather_tile(t + 1, (t + 1) % 2)

    slot = t % 2
    for k in range(2):
        pltpu.make_async_copy(ys_ref.at[pl.ds(0, tm)], buf_ref.at[slot, k], sem_ref.at[slot, k]).wait()
    route = route_ref[...]
    y = (route[:, 2:3] * _unpack_bf16_pairs(buf_ref[slot, 0])
         + route[:, 3:4] * _unpack_bf16_pairs(buf_ref[slot, 1]))
    out_ref[...] = _layer_norm(alpha * x1_ref[...] + g2_ref[...] * y, lng_ref[...], lnb_ref[...])


def _moe_combine(st, ys, slots, x1, route, mods, ln_g, ln_b, alpha):
    n, d = x1.shape
    tm = ROW_TILE
    grid_spec = pltpu.PrefetchScalarGridSpec(
        num_scalar_prefetch=1,
        grid=(n // tm,),
        in_specs=[
            pl.BlockSpec(memory_space=pl.ANY),
            pl.BlockSpec((tm, d), lambda t, s: (t, 0)),
            pl.BlockSpec((tm, LANES), lambda t, s: (t, 0)),
            pl.BlockSpec((None, 1, d), lambda t, s: (st.mod_row(t), 0, 5)),
            pl.BlockSpec((1, d), lambda t, s: (0, 0)),
            pl.BlockSpec((1, d), lambda t, s: (0, 0)),
        ],
        out_specs=pl.BlockSpec((tm, d), lambda t, s: (t, 0)),
        scratch_shapes=[pltpu.VMEM((2, 2, tm, ys.shape[1]), ys.dtype), pltpu.SemaphoreType.DMA((2, 2))],
    )
    return pl.pallas_call(
        functools.partial(_combine_kernel, alpha=alpha),
        grid_spec=grid_spec,
        out_shape=jax.ShapeDtypeStruct((n, d), F32),
        compiler_params=_cparams(("arbitrary",)),
        name="moe_combine_ln",
    )(slots, ys, x1, route, mods, ln_g, ln_b)


def _hier_moe(st, h2, route, x1, mods, ln_g, ln_b, w1, w3, w2, layer, alpha):
    n = h2.shape[0]
    ranks, counts = _moe_ranks(route)
    counts = counts[0, :MOE_EXPERTS]
    padded = (counts + MOE_BLOCK - 1) // MOE_BLOCK * MOE_BLOCK
    pad_end = jnp.cumsum(padded).astype(jnp.int32)
    pad_start = pad_end - padded
    expert = route[:, :2].astype(jnp.int32)
    eids = jnp.arange(MOE_EXPERTS, dtype=jnp.int32)
    start_of = jnp.sum(jnp.where(expert[:, :, None] == eids, pad_start, 0), axis=-1)
    slots = (start_of + ranks[:, :2]).reshape(2 * n)
    n_blocks = -(-(2 * n) // MOE_BLOCK) + MOE_EXPERTS
    block_pos = jnp.arange(n_blocks, dtype=jnp.int32) * MOE_BLOCK
    block_exp = jnp.minimum(jnp.sum((pad_end[None, :] <= block_pos[:, None]).astype(jnp.int32), axis=1),
                            MOE_EXPERTS - 1)
    n_used = (pad_end[-1:] // MOE_BLOCK).astype(jnp.int32)
    xs = _moe_dispatch(h2, slots, pad_start, pad_end, n_blocks * MOE_BLOCK)
    ys = _moe_experts(xs, block_exp, n_used, w1, w3, w2, layer)
    return _moe_combine(st, ys, slots, x1, route, mods, ln_g, ln_b, alpha)


def _swap_halves(w):
    half = w.shape[-1] // 2
    return jnp.concatenate([w[..., half:], w[..., :half]], axis=-1)


def _mla_weights(w_in, w_uq, w_ukv, w_o):
    qr, kvr = MLA_Q_RANK, MLA_KV_RANK
    w_pe = w_in[:, qr + kvr:]
    w_in_ext = jnp.concatenate([w_in, _swap_halves(w_pe)], axis=1).astype(BF16)
    uq = w_uq.reshape(qr, MLA_HEADS, MLA_NOPE + MLA_ROPE)
    q_nope = uq[:, :, :MLA_NOPE].reshape(qr, MLA_HEADS * MLA_NOPE)
    q_pe = uq[:, :, MLA_NOPE:]
    q_pe2 = jnp.concatenate([q_pe, _swap_halves(q_pe)], axis=-1).reshape(qr, MLA_HEADS * 2 * MLA_ROPE)
    w_q = jnp.concatenate([q_nope, q_pe2], axis=1).astype(BF16)
    ukv = w_ukv.reshape(kvr, MLA_HEADS, MLA_NOPE + MLA_V)
    w_kv = jnp.concatenate([ukv[:, :, :MLA_NOPE].reshape(kvr, -1), ukv[:, :, MLA_NOPE:].reshape(kvr, -1)],
                           axis=1).astype(BF16)
    return w_in_ext, w_q, w_kv, w_o.astype(BF16)


def _gla_weights(w_in, gate_a, gate_b, gate_bias, w_o):
    d = w_in.shape[0]
    key_dim = gate_b.shape[-1]
    rank = gate_a.shape[-1]
    ga = jnp.concatenate([gate_a[0], gate_a[1], jnp.zeros((d, LANES - 2 * rank), F32)], axis=1)
    w_qkg = jnp.concatenate([w_in[:, :2 * key_dim], ga], axis=1).astype(BF16)
    w_vr = w_in[:, 2 * key_dim:].astype(BF16)
    gb = jnp.zeros((LANES, 2 * key_dim), F32)
    gb = gb.at[:rank, :key_dim].set(gate_b[0]).at[rank:2 * rank, key_dim:].set(gate_b[1])
    bias = jnp.concatenate([gate_bias[0], gate_bias[1]])[None, :]
    return w_qkg, gb.astype(BF16), bias, w_vr, w_o.astype(BF16)


def _rope_table(seq):
    n_rows = seq // GRID_W
    row = jnp.repeat(jnp.arange(n_rows, dtype=F32), GRID_W)
    col = jnp.tile(jnp.arange(GRID_W, dtype=F32), n_rows)
    n_freq = MLA_ROPE // 4
    inv_freq = jnp.power(ROPE_THETA, -jnp.arange(n_freq, dtype=F32) / n_freq)
    ang = jnp.concatenate([row[:, None] * inv_freq, col[:, None] * inv_freq], axis=-1)
    cos, sin = jnp.cos(ang), jnp.sin(ang)
    lat = jnp.concatenate([cos, cos, -sin, sin], axis=-1)
    ident = jnp.concatenate([jnp.ones((ROW_TILE, MLA_ROPE), F32), jnp.zeros((ROW_TILE, MLA_ROPE), F32)], axis=-1)
    return jnp.concatenate([lat, ident], axis=0)


def kernel(x, c, ctx, c_ctx, w_mod, b_mod, ln1_g, ln1_b, ln2_g, ln2_b, mla_w_in, mla_q_norm, mla_w_uq, mla_kv_norm, mla_w_ukv, mla_w_o, gla_w_in, gla_gate_a, gla_gate_b, gla_gate_bias, gla_norm, gla_w_o, moe_w_grp, moe_b_grp, moe_w_exp, moe_b_exp, moe_w1, moe_w3, moe_w2):
    batch, seq, d = x.shape
    ctx_len = ctx.shape[1]
    depth = w_mod.shape[0]
    assert batch + 1 <= 8
    alpha = (2.0 * depth) ** 0.25
    full = _Stream(batch, seq, ctx_len, d)

    cond = jnp.concatenate([c, c_ctx[None, :], jnp.zeros((8 - batch - 1, d), F32)], axis=0)
    mods_all = _adaln_tables(cond, w_mod, b_mod)
    cs_tab = _rope_table(seq)
    xs = jnp.concatenate([x.reshape(batch * seq, d), ctx.reshape(batch * ctx_len, d)], axis=0)

    for i in range(depth):
        last = i == depth - 1
        mods = mods_all[i].reshape(8, 1, N_MOD * d)
        j = i // 2
        n_tiles = full.lat_tiles if last else full.tiles
        wr = jnp.concatenate([moe_w_grp[i], moe_w_exp[i],
                              jnp.zeros((d, LANES - MOE_GROUPS - MOE_EXPERTS), F32)], axis=1)
        wr_hi = wr.astype(BF16)
        wr_lo = (wr - wr_hi.astype(F32)).astype(BF16)
        rbias = jnp.concatenate([moe_b_grp[i], moe_b_exp[i],
                                 jnp.zeros((LANES - MOE_GROUPS - MOE_EXPERTS,), F32)])[None, :]
        ln1 = (ln1_g[i][None, :], ln1_b[i][None, :])
        if i % 2 == 0:
            w_in_ext, w_q, w_kv, w_o = _mla_weights(mla_w_in[j], mla_w_uq[j], mla_w_ukv[j], mla_w_o[j])
            q, k, v = _mla_proj(full, xs, mods, cs_tab, w_in_ext, mla_q_norm[j][None, :],
                                mla_kv_norm[j][None, :], w_q, w_kv)
            o = _attention(full, q, k, v, latent=True)
            if not last:
                o = jnp.concatenate([o, _attention(full, q, k, v, latent=False)], axis=0)
            x1, h2, route = _mix_out(full, n_tiles, [o], [_row_spec(o.shape[1])],
                                     functools.partial(_mla_out_kernel, alpha=alpha),
                                     xs, mods, ln1[0], ln1[1], w_o, wr_hi, wr_lo, rbias)
        else:
            w_qkg, gb_ext, gbias, w_vr, w_o = _gla_weights(gla_w_in[j], gla_gate_a[j], gla_gate_b[j],
                                                          gla_gate_bias[j], gla_w_o[j])
            q, k, v, r, gf, gb = _gla_proj(full, xs, mods, w_qkg, gb_ext, gbias, w_vr)
            o_f, o_b = _gla_scan(full, q, k, v, gf, gb)
            vd = v.shape[1]
            x1, h2, route = _mix_out(full, n_tiles, [o_f, o_b, r, gla_norm[j][None, :]],
                                     [_row_spec(vd), _row_spec(vd), _row_spec(vd),
                                      _const_spec((1, vd // GLA_HEADS))],
                                     functools.partial(_gla_out_kernel, alpha=alpha, dv=vd // GLA_HEADS),
                                     xs, mods, ln1[0], ln1[1], w_o, wr_hi, wr_lo, rbias)
        xs = _hier_moe(full, h2, route, x1, mods, ln2_g[i][None, :], ln2_b[i][None, :],
                       moe_w1, moe_w3, moe_w2, i, alpha)
    return xs[:batch * seq].reshape(batch, seq, d)
```

```python
import functools
import math

import jax
import jax.numpy as jnp
import numpy as np
from jax import lax
from jax.experimental import pallas as pl
from jax.experimental.pallas import tpu as pltpu

F32 = jnp.float32
BF16 = jnp.bfloat16

GRID_W = 64
N_MOD = 6
NORM_EPS = 1e-6
MLA_HEADS = 16
MLA_Q_RANK = 768
MLA_KV_RANK = 256
MLA_NOPE = 128
MLA_ROPE = 64
MLA_V = 128
ROPE_THETA = 10000.0
GLA_HEADS = 4
GLA_GATE_RANK = 16
GLA_TAU = 16.0
GLA_CHUNK = 64
GLA_SUB = 16
MOE_GROUPS = 4
MOE_PER_GROUP = 8
MOE_EXPERTS = MOE_GROUPS * MOE_PER_GROUP
MOE_BLOCK = 256

LANES = 128
VMEM_LIMIT = 56 * 1024 * 1024

ROW_TILE = 256
RANK_TILE = 1024
ATTN_TQ = 1024
ATTN_TK = 1024
ATTN_STREAMS = 2
ATTN_SKEW = 2


def _cparams(sem):
    return pltpu.CompilerParams(dimension_semantics=sem, vmem_limit_bytes=VMEM_LIMIT)


def _sigmoid(x):
    return 1.0 / (1.0 + jnp.exp(-x))


def _layer_norm(z, g, b):
    mu = jnp.mean(z, axis=-1, keepdims=True)
    zc = z - mu
    var = jnp.mean(zc * zc, axis=-1, keepdims=True)
    return zc * lax.rsqrt(var + NORM_EPS) * g + b


def _rms_norm(z, g):
    return z * lax.rsqrt(jnp.mean(z * z, axis=-1, keepdims=True) + NORM_EPS) * g


def _mods_kernel(cond_ref, w_ref, b_ref, o_ref):
    c = cond_ref[...]
    s = c * _sigmoid(c)
    o_ref[...] = jnp.dot(s, w_ref[...], preferred_element_type=F32,
                         precision=lax.Precision.HIGHEST) + b_ref[...]


def _adaln_tables(cond, w_mod, b_mod):
    depth, d, n6 = w_mod.shape
    tn = 1024
    return pl.pallas_call(
        _mods_kernel,
        grid=(depth, n6 // tn),
        in_specs=[
            pl.BlockSpec((8, d), lambda l, j: (0, 0)),
            pl.BlockSpec((None, d, tn), lambda l, j: (l, 0, j)),
            pl.BlockSpec((None, 1, tn), lambda l, j: (l, 0, j)),
        ],
        out_specs=pl.BlockSpec((None, 8, tn), lambda l, j: (l, 0, j)),
        out_shape=jax.ShapeDtypeStruct((depth, 8, n6), F32),
        compiler_params=_cparams(("parallel", "parallel")),
        name="adaln_tables",
    )(cond, w_mod, b_mod.reshape(depth, 1, n6))


class _Stream:
    def __init__(self, batch, seq, ctx_len, d):
        self.batch, self.seq, self.ctx_len, self.d = batch, seq, ctx_len, d
        self.n_lat = batch * seq
        self.n_ctx = batch * ctx_len
        self.n = self.n_lat + self.n_ctx
        assert seq % ROW_TILE == 0 and ctx_len % ROW_TILE == 0
        self.lat_tiles = self.n_lat // ROW_TILE
        self.tiles = self.n // ROW_TILE
        self.tiles_per_batch = seq // ROW_TILE

    def mod_row(self, t):
        return jnp.where(t < self.lat_tiles, t // self.tiles_per_batch, self.batch)


def _mod_spec(st, chunk):
    d = st.d
    return pl.BlockSpec((None, 1, d), lambda t: (st.mod_row(t), 0, chunk))


def _row_spec(width, tm=ROW_TILE):
    return pl.BlockSpec((tm, width), lambda t: (t, 0))


def _const_spec(shape):
    nd = len(shape)
    return pl.BlockSpec(shape, lambda t: (0,) * nd, pipeline_mode=pl.Buffered(1))


def _half_sum(x):
    return x + pltpu.roll(x, LANES // 2, axis=1)


def _mla_proj_kernel(x_ref, sh_ref, sc_ref, cs_ref, w_in_ref, qg_ref, kvg_ref, w_q_ref, w_kv_ref,
                     q_ref, k_ref, v_ref, *, scale):
    h = x_ref[...] * (1.0 + sc_ref[...]) + sh_ref[...]
    lat = jnp.dot(h.astype(BF16), w_in_ref[...], preferred_element_type=F32)
    cq = _rms_norm(lat[:, :MLA_Q_RANK], qg_ref[...]).astype(BF16)
    ckv = _rms_norm(lat[:, MLA_Q_RANK:MLA_Q_RANK + MLA_KV_RANK], kvg_ref[...]).astype(BF16)
    cs = cs_ref[...]
    lane = lax.broadcasted_iota(jnp.int32, cs.shape, 1)
    kpe = _half_sum(lat[:, MLA_Q_RANK + MLA_KV_RANK:] * cs)
    kpe = jnp.where(lane < MLA_ROPE, kpe, 0.0).astype(BF16)
    qall = jnp.dot(cq, w_q_ref[...], preferred_element_type=F32)
    kvall = jnp.dot(ckv, w_kv_ref[...], preferred_element_type=F32)
    hn = MLA_HEADS * MLA_NOPE
    for hd in range(MLA_HEADS):
        lo = hd * LANES
        q_ref[hd, :, :MLA_NOPE] = (qall[:, lo:lo + LANES] * scale).astype(BF16)
        q_ref[hd, :, MLA_NOPE:] = (_half_sum(qall[:, hn + lo:hn + lo + LANES] * cs) * scale).astype(BF16)
        k_ref[hd, :, :MLA_NOPE] = kvall[:, lo:lo + LANES].astype(BF16)
        k_ref[hd, :, MLA_NOPE:] = kpe
        v_ref[hd] = jnp.transpose(kvall[:, hn + lo:hn + lo + LANES]).astype(BF16)


def _mla_proj(st, x, mods, cs_tab, w_in_ext, q_norm, kv_norm, w_q, w_kv):
    n, d = st.n, st.d
    tm = ROW_TILE
    lat_w = w_in_ext.shape[1]
    qk_w = MLA_NOPE + LANES
    scale = (MLA_NOPE + MLA_ROPE) ** -0.5 * math.log2(math.e)
    seq_tiles = st.tiles_per_batch

    def cs_map(t):
        return (jnp.where(t < st.lat_tiles, t % seq_tiles, seq_tiles), 0)

    head_spec = lambda w: pl.BlockSpec((MLA_HEADS, tm, w), lambda t: (0, t, 0))
    return pl.pallas_call(
        functools.partial(_mla_proj_kernel, scale=scale),
        grid=(st.tiles,),
        in_specs=[
            _row_spec(d), _mod_spec(st, 0), _mod_spec(st, 1),
            pl.BlockSpec((tm, LANES), cs_map),
            _const_spec((d, lat_w)), _const_spec((1, MLA_Q_RANK)), _const_spec((1, MLA_KV_RANK)),
            _const_spec(w_q.shape), _const_spec(w_kv.shape),
        ],
        out_specs=[head_spec(qk_w), head_spec(qk_w),
                   pl.BlockSpec((MLA_HEADS, MLA_V, tm), lambda t: (0, 0, t))],
        out_shape=[
            jax.ShapeDtypeStruct((MLA_HEADS, n, qk_w), BF16),
            jax.ShapeDtypeStruct((MLA_HEADS, n, qk_w), BF16),
            jax.ShapeDtypeStruct((MLA_HEADS, MLA_V, n), BF16),
        ],
        compiler_params=_cparams(("parallel",)),
        name="mla_proj",
    )(x, mods, mods, cs_tab, w_in_ext, q_norm, kv_norm, w_q, w_kv)


def _attn_kernel(q_ref, kc_ref, vct_ref, *rest, n_chunks):
    if n_chunks:
        kl_ref, vlt_ref, o_ref = rest
    else:
        (o_ref,) = rest
    nt = (((1,), (1,)), ((), ()))
    tq = q_ref.shape[0]
    n_streams = ATTN_STREAMS if n_chunks else 1
    rows_q = tq // n_streams

    def chunk_rows(j):
        return pl.ds(j * ATTN_TK, ATTN_TK)

    class Stream:
        def __init__(self, idx):
            self.qrows = pl.ds(idx * rows_q, rows_q)
            self.q = q_ref[self.qrows, :]
            self.stage = 0

        def scores(self, k):
            s = lax.dot_general(k, self.q, nt, preferred_element_type=F32)
            return s, jnp.max(s, axis=0, keepdims=True)

        def advance(self):
            j = self.stage - 1
            if self.stage == 0:
                s, m_new = self.scores(kc_ref[...])
                self.nxt = self.scores(kl_ref[chunk_rows(0), :]) if n_chunks else None
                p = jnp.exp2(s - m_new)
                self.l = jnp.sum(p, axis=0, keepdims=True)
                self.acc = jnp.dot(vct_ref[...], p.astype(BF16), preferred_element_type=F32)
            else:
                s, m_blk = self.nxt
                if j + 1 < n_chunks:
                    self.nxt = self.scores(kl_ref[chunk_rows(j + 1), :])
                m_new = jnp.maximum(self.m, m_blk)
                alpha = jnp.exp2(self.m - m_new)
                p = jnp.exp2(s - m_new)
                self.l = alpha * self.l + jnp.sum(p, axis=0, keepdims=True)
                self.acc = alpha * self.acc + jnp.dot(vlt_ref[:, chunk_rows(j)], p.astype(BF16),
                                                      preferred_element_type=F32)
            self.m = m_new
            self.stage += 1
            if self.stage == n_chunks + 1:
                o_ref[self.qrows, :] = jnp.transpose(self.acc / self.l).astype(o_ref.dtype)

    streams = [Stream(i) for i in range(n_streams)]
    n_stages = n_chunks + 1
    for t in range(n_stages + ATTN_SKEW * (n_streams - 1)):
        for i, st in enumerate(streams):
            if 0 <= t - i * ATTN_SKEW < n_stages:
                st.advance()


def _attention(st, q, k, vt, latent):
    b, seq, lc = st.batch, st.seq, st.ctx_len
    qk_w = q.shape[-1]
    tq = ATTN_TQ if latent else min(ATTN_TQ, lc)
    ctx_blk0 = st.n_lat // lc
    if latent:
        nq = seq // tq
        q_map = lambda bi, h, qi: (h, bi * nq + qi, 0)
        o_map = lambda bi, h, qi: (bi * nq + qi, h)
        rows = st.n_lat
        n_chunks = seq // ATTN_TK
    else:
        nq = lc // tq
        q_map = lambda bi, h, qi: (h, st.n_lat // tq + bi * nq + qi, 0)
        o_map = lambda bi, h, qi: (bi * nq + qi, h)
        rows = st.n_ctx
        n_chunks = 0
    kc_map = lambda bi, h, qi: (h, ctx_blk0 + bi, 0)
    kl_map = lambda bi, h, qi: (h, bi, 0)
    vc_map = lambda bi, h, qi: (h, 0, ctx_blk0 + bi)
    vl_map = lambda bi, h, qi: (h, 0, bi)
    in_specs = [
        pl.BlockSpec((None, tq, qk_w), q_map),
        pl.BlockSpec((None, lc, qk_w), kc_map),
        pl.BlockSpec((None, MLA_V, lc), vc_map),
    ]
    args = [q, k, vt]
    if latent:
        in_specs += [pl.BlockSpec((None, seq, qk_w), kl_map), pl.BlockSpec((None, MLA_V, seq), vl_map)]
        args += [k, vt]
    return pl.pallas_call(
        functools.partial(_attn_kernel, n_chunks=n_chunks),
        grid=(b, MLA_HEADS, nq),
        in_specs=in_specs,
        out_specs=pl.BlockSpec((tq, MLA_V), o_map),
        out_shape=jax.ShapeDtypeStruct((rows, MLA_HEADS * MLA_V), BF16),
        compiler_params=_cparams(("parallel", "parallel", "arbitrary")),
        name="mla_attention_latent" if latent else "mla_attention_context",
    )(*args)


def _gla_qkg_kernel(x_ref, sh_ref, sc_ref, w_ref, gb_ref, bias_ref, q_ref, k_ref, gf_ref, gb_out_ref,
                    *, key_dim, q_scale):
    h = (x_ref[...] * (1.0 + sc_ref[...]) + sh_ref[...]).astype(BF16)
    y = jnp.dot(h, w_ref[...], preferred_element_type=F32)
    q_ref[...] = (y[:, :key_dim] * q_scale).astype(BF16)
    k_ref[...] = y[:, key_dim:2 * key_dim].astype(BF16)
    z = jnp.dot(y[:, 2 * key_dim:].astype(BF16), gb_ref[...], preferred_element_type=F32) + bias_ref[...]
    g = (jnp.minimum(z, 0.0) - jnp.log(1.0 + jnp.exp(-jnp.abs(z)))) * (math.log2(math.e) / GLA_TAU)
    gf_ref[...] = g[:, :key_dim]
    gb_out_ref[...] = g[:, key_dim:]


def _gla_vr_kernel(x_ref, sh_ref, sc_ref, w_ref, v_ref, r_ref, *, value_dim):
    h = (x_ref[...] * (1.0 + sc_ref[...]) + sh_ref[...]).astype(BF16)
    y = jnp.dot(h, w_ref[...], preferred_element_type=F32)
    v_ref[...] = y[:, :value_dim].astype(BF16)
    r_ref[...] = y[:, value_dim:]


def _gla_proj(st, x, mods, w_qkg, gate_b_ext, gate_bias, w_vr):
    n, d = st.n, st.d
    key_dim = (w_qkg.shape[1] - LANES) // 2
    value_dim = w_vr.shape[1] // 2
    dk = key_dim // GLA_HEADS
    q, k, gf, gb = pl.pallas_call(
        functools.partial(_gla_qkg_kernel, key_dim=key_dim, q_scale=dk ** -0.5),
        grid=(st.tiles,),
        in_specs=[_row_spec(d), _mod_spec(st, 0), _mod_spec(st, 1), _const_spec(w_qkg.shape),
                  _const_spec(gate_b_ext.shape), _const_spec(gate_bias.shape)],
        out_specs=[_row_spec(key_dim)] * 4,
        out_shape=[jax.ShapeDtypeStruct((n, key_dim), BF16), jax.ShapeDtypeStruct((n, key_dim), BF16),
                   jax.ShapeDtypeStruct((n, key_dim), F32), jax.ShapeDtypeStruct((n, key_dim), F32)],
        compiler_params=_cparams(("parallel",)),
        name="gla_proj_qkg",
    )(x, mods, mods, w_qkg, gate_b_ext, gate_bias)
    v, r = pl.pallas_call(
        functools.partial(_gla_vr_kernel, value_dim=value_dim),
        grid=(st.tiles,),
        in_specs=[_row_spec(d), _mod_spec(st, 0), _mod_spec(st, 1), _const_spec(w_vr.shape)],
        out_specs=[_row_spec(value_dim)] * 2,
        out_shape=[jax.ShapeDtypeStruct((n, value_dim), BF16), jax.ShapeDtypeStruct((n, value_dim), F32)],
        compiler_params=_cparams(("parallel",)),
        name="gla_proj_vr",
    )(x, mods, mods, w_vr)
    return q, k, v, r, gf, gb


def _cumsum_rows(x, reverse):
    n = x.shape[0]
    row = lax.broadcasted_iota(jnp.int32, x.shape, 0)
    s = 1
    while s < n:
        if reverse:
            x = x + jnp.where(row < n - s, pltpu.roll(x, n - s, axis=0), 0.0)
        else:
            x = x + jnp.where(row >= s, pltpu.roll(x, s, axis=0), 0.0)
        s *= 2
    return x


def _gla_chunk(q, k, v, g, state_t, reverse):
    c, dk = q.shape
    nsub = c // GLA_SUB
    half = GLA_SUB // 2
    b = _cumsum_rows(g, reverse)
    b_last = jnp.sum(g, axis=0, keepdims=True)
    nt = (((1,), (1,)), ((), ()))
    qe = (q * jnp.exp2(b)).astype(BF16)
    o = lax.dot_general(qe, state_t.astype(BF16), nt, preferred_element_type=F32)

    q_parts, k_parts = [], []
    for src in (range(1, nsub) if reverse else range(nsub - 1)):
        lo = src * GLA_SUB
        if reverse:
            ref = b[lo:lo + 1, :]
            qd = (q[:lo] * jnp.exp2(b[:lo] - ref)).astype(BF16)
            q_parts.append(jnp.concatenate([qd, jnp.zeros((c - lo, dk), BF16)], axis=0))
        else:
            ref = b[lo + GLA_SUB - 1:lo + GLA_SUB, :]
            qd = (q[lo + GLA_SUB:] * jnp.exp2(b[lo + GLA_SUB:] - ref)).astype(BF16)
            q_parts.append(jnp.concatenate([jnp.zeros((lo + GLA_SUB, dk), BF16), qd], axis=0))
        kd = (k[lo:lo + GLA_SUB] * jnp.exp2(ref - b[lo:lo + GLA_SUB])).astype(BF16)
        pieces = [kd]
        if lo:
            pieces.insert(0, jnp.zeros((lo, dk), BF16))
        if c - lo - GLA_SUB:
            pieces.append(jnp.zeros((c - lo - GLA_SUB, dk), BF16))
        k_parts.append(jnp.concatenate(pieces, axis=0))
    off = lax.dot_general(jnp.concatenate(q_parts, axis=1), jnp.concatenate(k_parts, axis=1), nt,
                          preferred_element_type=F32)
    a_mat = jnp.concatenate([off, jnp.zeros((c, LANES - c), F32)], axis=1)

    ones = jnp.ones((dk, LANES), BF16)
    sub_r = lax.broadcasted_iota(jnp.int32, (half, LANES), 0)
    sub_c = lax.broadcasted_iota(jnp.int32, (half, LANES), 1)
    diag_rows = []
    for a in range(nsub):
        lo = a * GLA_SUB
        terms, spans = [], []
        for jj in range(GLA_SUB):
            if reverse:
                r0, r1 = 0, (half if jj < half else GLA_SUB)
            else:
                r0, r1 = (0 if jj < half else half), GLA_SUB
            kj = k[lo + jj:lo + jj + 1, :]
            bj = b[lo + jj:lo + jj + 1, :]
            terms.append(q[lo + r0:lo + r1] * kj * jnp.exp2(b[lo + r0:lo + r1] - bj))
            spans.append((r0, r1))
        sums = jnp.dot(jnp.concatenate(terms, axis=0).astype(BF16), ones, preferred_element_type=F32)
        blk = [jnp.zeros((half, LANES), F32), jnp.zeros((half, LANES), F32)]
        pos = 0
        for jj, (r0, r1) in enumerate(spans):
            for r in range(r0, r1, half):
                piece = sums[pos:pos + half, :]
                pos += half
                rows = sub_r + r
                keep = (sub_c == lo + jj) & ((rows <= jj) if reverse else (rows >= jj))
                blk[r // half] = blk[r // half] + jnp.where(keep, piece, 0.0)
        diag_rows += blk
    a_mat = a_mat + jnp.concatenate(diag_rows, axis=0)
    o = o + jnp.dot(a_mat[:, :c].astype(BF16), v, preferred_element_type=F32)
    kd = (k * jnp.exp2(b_last - b)).astype(BF16)
    tn = (((0,), (0,)), ((), ()))
    new_state = state_t * jnp.exp2(b_last) + lax.dot_general(v, kd, tn, preferred_element_type=F32)
    return o, new_state


def _gla_scan_kernel(qf_ref, kf_ref, vf_ref, gf_ref, qb_ref, kb_ref, vb_ref, gb_ref, of_ref, ob_ref,
                     sf_ref, sb_ref, *, n_chunks):
    @pl.when(pl.program_id(2) == 0)
    def _():
        sf_ref[...] = jnp.zeros_like(sf_ref)
        sb_ref[...] = jnp.zeros_like(sb_ref)

    def step(q_ref, k_ref, v_ref, g_ref, o_ref, state_ref, cidx, reverse):
        rows = pl.ds(cidx * GLA_CHUNK, GLA_CHUNK)
        o, new_state = _gla_chunk(q_ref[rows, :].astype(F32), k_ref[rows, :].astype(F32), v_ref[rows, :],
                                  g_ref[rows, :], state_ref[...], reverse)
        o_ref[rows, :] = o
        state_ref[...] = new_state

    for cidx in range(n_chunks):
        step(qf_ref, kf_ref, vf_ref, gf_ref, of_ref, sf_ref, cidx, False)
        step(qb_ref, kb_ref, vb_ref, gb_ref, ob_ref, sb_ref, n_chunks - 1 - cidx, True)


def _gla_scan(st, q, k, v, gf, gb):
    b = st.batch
    key_dim, value_dim = q.shape[1], v.shape[1]
    dk, dv = key_dim // GLA_HEADS, value_dim // GLA_HEADS
    t_rows = ROW_TILE
    nc, nl = st.ctx_len // t_rows, st.seq // t_rows

    def row_block(bi, s, reverse):
        if reverse:
            ctx = st.lat_tiles + bi * nc + (nc - 1 - s)
            lat = bi * nl + (nl - 1 - (s - nc))
        else:
            ctx = st.lat_tiles + bi * nc + s
            lat = bi * nl + (s - nc)
        return jnp.where(s < nc, ctx, lat)

    def spec(w, reverse):
        return pl.BlockSpec((t_rows, w), lambda bi, h, s: (row_block(bi, s, reverse), h))

    dir_specs = lambda reverse: [spec(dk, reverse), spec(dk, reverse), spec(dv, reverse), spec(dk, reverse)]
    out = jax.ShapeDtypeStruct((st.n, value_dim), F32)
    return pl.pallas_call(
        functools.partial(_gla_scan_kernel, n_chunks=t_rows // GLA_CHUNK),
        grid=(b, GLA_HEADS, nc + nl),
        in_specs=dir_specs(False) + dir_specs(True),
        out_specs=[spec(dv, False), spec(dv, True)],
        out_shape=[out, out],
        scratch_shapes=[pltpu.VMEM((dv, dk), F32), pltpu.VMEM((dv, dk), F32)],
        compiler_params=_cparams(("parallel", "parallel", "arbitrary")),
        name="gla_scan_bidir",
    )(q, k, v, gf, q, k, v, gb)


def _route(logits, n_real):
    lane = lax.broadcasted_iota(jnp.int32, logits.shape, 1)
    neg = jnp.float32(-jnp.inf)
    big = jnp.int32(2 ** 30)
    is_grp = lane < MOE_GROUPS
    gl = jnp.where(is_grp, logits, neg)
    gmax = jnp.max(gl, axis=-1, keepdims=True)
    g_top = jnp.min(jnp.where(gl == gmax, lane, big), axis=-1, keepdims=True)
    p_grp = 1.0 / jnp.sum(jnp.exp(gl - gmax), axis=-1, keepdims=True)
    first = MOE_GROUPS + g_top * MOE_PER_GROUP
    in_grp = (lane >= first) & (lane < first + MOE_PER_GROUP)
    el = jnp.where(in_grp, logits, neg)
    emax = jnp.max(el, axis=-1, keepdims=True)
    pe = jnp.exp(el - emax)
    pe = pe / jnp.sum(pe, axis=-1, keepdims=True)
    v1 = jnp.max(pe, axis=-1, keepdims=True)
    i1 = jnp.min(jnp.where(in_grp & (pe == v1), lane, big), axis=-1, keepdims=True)
    rest = jnp.where(in_grp & (lane != i1), pe, -1.0)
    v2 = jnp.max(rest, axis=-1, keepdims=True)
    i2 = jnp.min(jnp.where(rest == v2, lane, big), axis=-1, keepdims=True)
    denom = v1 + v2
    w1 = p_grp * v1 / denom
    w2 = p_grp * v2 / denom
    e1 = (i1 - MOE_GROUPS).astype(F32)
    e2 = (i2 - MOE_GROUPS).astype(F32)
    del n_real
    return jnp.where(lane == 0, e1, jnp.where(lane == 1, e2, jnp.where(lane == 2, w1, jnp.where(lane == 3, w2, 0.0))))


def _pack_bf16_pairs(x):
    half = x.shape[1] // 2
    bits = lax.bitcast_convert_type(x.astype(BF16).astype(F32), jnp.uint32)
    return bits[:, half:] | (bits[:, :half] >> 16)


def _unpack_bf16_pairs(w):
    lo = lax.bitcast_convert_type(w << 16, F32)
    hi = lax.bitcast_convert_type(w & jnp.uint32(0xFFFF0000), F32)
    return jnp.concatenate([lo, hi], axis=1)


MIX_GROUPS = 2


def _row_groups(tm):
    rows = tm // MIX_GROUPS
    return [pl.ds(i * rows, rows) for i in range(MIX_GROUPS)]


def _post_mix(y, rows, x_ref, g1_ref, lng_ref, lnb_ref, sh2_ref, sc2_ref, wr_hi_ref, wr_lo_ref, rb_ref,
              x1_ref, h2_ref, route_ref, alpha):
    x1 = _layer_norm(alpha * x_ref[rows, :] + g1_ref[...] * y, lng_ref[...], lnb_ref[...])
    x1_ref[rows, :] = x1
    h2 = x1 * (1.0 + sc2_ref[...]) + sh2_ref[...]
    h2_ref[rows, :] = _pack_bf16_pairs(h2)
    hi = h2.astype(BF16)
    lo = (h2 - hi.astype(F32)).astype(BF16)
    w_hi = wr_hi_ref[...]
    logits = (jnp.dot(hi, w_hi, preferred_element_type=F32) + jnp.dot(lo, w_hi, preferred_element_type=F32)
              + jnp.dot(hi, wr_lo_ref[...], preferred_element_type=F32)) + rb_ref[...]
    route_ref[rows, :] = _route(logits, None)


def _mla_out_kernel(o_ref, w_o_ref, *rest, alpha):
    groups = _row_groups(o_ref.shape[0])
    ys = [jnp.dot(o_ref[rows, :], w_o_ref[...], preferred_element_type=F32) for rows in groups]
    for rows, y in zip(groups, ys):
        _post_mix(y, rows, *rest, alpha=alpha)


def _gla_out_kernel(of_ref, ob_ref, r_ref, ng_ref, w_o_ref, *rest, alpha, dv):
    groups = _row_groups(of_ref.shape[0])
    ng = ng_ref[...]
    ys = []
    for rows in groups:
        o = of_ref[rows, :] + ob_ref[rows, :]
        r = r_ref[rows, :]
        gate = r * _sigmoid(r)
        parts = []
        for hd in range(GLA_HEADS):
            parts.append(_rms_norm(o[:, hd * dv:(hd + 1) * dv], ng) * gate[:, hd * dv:(hd + 1) * dv])
        u = jnp.concatenate(parts, axis=1).astype(BF16)
        ys.append(jnp.dot(u, w_o_ref[...], preferred_element_type=F32))
    for rows, y in zip(groups, ys):
        _post_mix(y, rows, *rest, alpha=alpha)


def _mix_out(st, n_tiles, mixer_inputs, mixer_specs, kernel, x, mods, ln_g, ln_b, w_o, wr_hi, wr_lo, rbias):
    d = st.d
    rows = n_tiles * ROW_TILE
    common_specs = [
        _row_spec(d), _mod_spec(st, 2), _const_spec((1, d)), _const_spec((1, d)),
        _mod_spec(st, 3), _mod_spec(st, 4), _const_spec(wr_hi.shape), _const_spec(wr_lo.shape),
        _const_spec((1, LANES)),
    ]
    return pl.pallas_call(
        kernel,
        grid=(n_tiles,),
        in_specs=mixer_specs + [_const_spec(w_o.shape)] + common_specs,
        out_specs=[_row_spec(d), _row_spec(d // 2), _row_spec(LANES)],
        out_shape=[jax.ShapeDtypeStruct((rows, d), F32), jax.ShapeDtypeStruct((rows, d // 2), jnp.uint32),
                   jax.ShapeDtypeStruct((rows, LANES), F32)],
        compiler_params=_cparams(("parallel",)),
        name="mixer_out_ln_route",
    )(*mixer_inputs, w_o, x, mods, ln_g, ln_b, mods, mods, wr_hi, wr_lo, rbias)


def _rank_kernel(route_ref, rank_ref, counts_ref, run_ref):
    t = pl.program_id(0)

    @pl.when(t == 0)
    def _():
        run_ref[...] = jnp.zeros_like(run_ref)

    route = route_ref[...]
    tm = route.shape[0]
    lane = lax.broadcasted_iota(jnp.int32, route.shape, 1)
    e0 = route[:, 0:1].astype(jnp.int32)
    e1 = route[:, 1:2].astype(jnp.int32)
    oh0 = (lane == e0).astype(F32)
    oh1 = (lane == e1).astype(F32)
    both = oh0 + oh1
    ri = lax.broadcasted_iota(jnp.int32, (tm, tm), 0)
    ci = lax.broadcasted_iota(jnp.int32, (tm, tm), 1)
    tri = (ci < ri).astype(BF16)
    before = jnp.dot(tri, both.astype(BF16), preferred_element_type=F32) + run_ref[0:1, :]
    r0 = jnp.sum(oh0 * before, axis=-1, keepdims=True)
    r1 = jnp.sum(oh1 * before, axis=-1, keepdims=True)
    rank_ref[...] = jnp.where(lane == 0, r0, jnp.where(lane == 1, r1, 0.0)).astype(jnp.int32)
    run_ref[...] = run_ref[...] + jnp.sum(both, axis=0, keepdims=True)
    counts_ref[...] = run_ref[...].astype(jnp.int32)


def _moe_ranks(route):
    n = route.shape[0]
    tm = RANK_TILE if n % RANK_TILE == 0 else ROW_TILE
    return pl.pallas_call(
        _rank_kernel,
        grid=(n // tm,),
        in_specs=[_row_spec(LANES, tm)],
        out_specs=[_row_spec(LANES, tm), _const_spec((8, LANES))],
        out_shape=[jax.ShapeDtypeStruct((n, LANES), jnp.int32), jax.ShapeDtypeStruct((8, LANES), jnp.int32)],
        scratch_shapes=[pltpu.VMEM((8, LANES), F32)],
        compiler_params=_cparams(("arbitrary",)),
        name="moe_ranks",
    )(route)


def _dispatch_kernel(slots_ref, pad_start_ref, pad_end_ref, h_ref, xs_ref, zero_ref, sem_ref, zsem_ref):
    t = pl.program_id(0)
    tm = h_ref.shape[0]

    @pl.when(t == 0)
    def _():
        zero_ref[...] = jnp.zeros_like(zero_ref)
        for e in range(MOE_EXPERTS):
            @pl.when(pad_end_ref[e] > pad_start_ref[e])
            def _():
                start = pl.multiple_of(pad_end_ref[e] - MOE_BLOCK, MOE_BLOCK)
                pltpu.make_async_copy(zero_ref, xs_ref.at[pl.ds(start, MOE_BLOCK)], zsem_ref).start()
        n_used = pad_end_ref[MOE_EXPERTS - 1] // MOE_BLOCK
        n_blocks = xs_ref.shape[0] // MOE_BLOCK

        def fill_tail(i, carry):
            start = pl.multiple_of(i * MOE_BLOCK, MOE_BLOCK)
            pltpu.make_async_copy(zero_ref, xs_ref.at[pl.ds(start, MOE_BLOCK)], zsem_ref).start()
            return carry

        def wait_tail(i, carry):
            pltpu.make_async_copy(zero_ref, xs_ref.at[pl.ds(0, MOE_BLOCK)], zsem_ref).wait()
            return carry

        lax.fori_loop(n_used, n_blocks, fill_tail, 0)
        for e in range(MOE_EXPERTS):
            @pl.when(pad_end_ref[e] > pad_start_ref[e])
            def _():
                pltpu.make_async_copy(zero_ref, xs_ref.at[pl.ds(0, MOE_BLOCK)], zsem_ref).wait()
        lax.fori_loop(n_used, n_blocks, wait_tail, 0)

    base = t * (2 * tm)

    def issue(r, carry):
        s0 = slots_ref[base + 2 * r]
        s1 = slots_ref[base + 2 * r + 1]
        pltpu.make_async_copy(h_ref.at[pl.ds(r, 1)], xs_ref.at[pl.ds(s0, 1)], sem_ref.at[0]).start()
        pltpu.make_async_copy(h_ref.at[pl.ds(r, 1)], xs_ref.at[pl.ds(s1, 1)], sem_ref.at[1]).start()
        return carry

    lax.fori_loop(0, tm, issue, 0, unroll=8)
    pltpu.make_async_copy(h_ref, xs_ref.at[pl.ds(0, tm)], sem_ref.at[0]).wait()
    pltpu.make_async_copy(h_ref, xs_ref.at[pl.ds(0, tm)], sem_ref.at[1]).wait()


def _moe_dispatch(h2, slots, pad_start, pad_end, n_rows):
    n, d = h2.shape
    tm = ROW_TILE
    grid_spec = pltpu.PrefetchScalarGridSpec(
        num_scalar_prefetch=3,
        grid=(n // tm,),
        in_specs=[pl.BlockSpec((tm, d), lambda t, *_: (t, 0))],
        out_specs=pl.BlockSpec(memory_space=pl.ANY),
        scratch_shapes=[pltpu.VMEM((MOE_BLOCK, d), h2.dtype), pltpu.SemaphoreType.DMA((2,)),
                        pltpu.SemaphoreType.DMA(())],
    )
    return pl.pallas_call(
        _dispatch_kernel,
        grid_spec=grid_spec,
        out_shape=jax.ShapeDtypeStruct((n_rows, d), h2.dtype),
        compiler_params=_cparams(("arbitrary",)),
        name="moe_dispatch",
    )(slots, pad_start, pad_end, h2)


def _experts_kernel(block_exp_ref, n_used_ref, xs_ref, w1_ref, w3_ref, w2_ref, o_ref, w1b_ref, w3b_ref, w2b_ref):
    i = pl.program_id(0)

    @pl.when(i < n_used_ref[0])
    def _():
        prev = block_exp_ref[jnp.maximum(i - 1, 0)]

        @pl.when((i == 0) | (block_exp_ref[i] != prev))
        def _():
            w1b_ref[...] = w1_ref[...].astype(BF16)
            w3b_ref[...] = w3_ref[...].astype(BF16)
            w2b_ref[...] = w2_ref[...].astype(BF16)

        groups = _row_groups(xs_ref.shape[0])
        ups = []
        for rows in groups:
            xb = _unpack_bf16_pairs(xs_ref[rows, :]).astype(BF16)
            ups.append((jnp.dot(xb, w1b_ref[...], preferred_element_type=F32),
                        jnp.dot(xb, w3b_ref[...], preferred_element_type=F32)))
        for rows, (a, g) in zip(groups, ups):
            hid = (a * _sigmoid(a) * g).astype(BF16)
            o_ref[rows, :] = _pack_bf16_pairs(jnp.dot(hid, w2b_ref[...], preferred_element_type=F32))

    @pl.when(i >= n_used_ref[0])
    def _():
        o_ref[...] = jnp.zeros_like(o_ref)


def _moe_experts(xs, block_exp, n_used, w1, w3, w2, layer):
    n_rows, dp = xs.shape
    n_blocks = n_rows // MOE_BLOCK
    d, f = w1.shape[-2], w1.shape[-1]
    assert dp * 2 == d

    def blk(i, be, nu):
        return jnp.minimum(i, nu[0] - 1)

    grid_spec = pltpu.PrefetchScalarGridSpec(
        num_scalar_prefetch=2,
        grid=(n_blocks,),
        in_specs=[
            pl.BlockSpec((MOE_BLOCK, dp), lambda i, be, nu: (blk(i, be, nu), 0)),
            pl.BlockSpec((None, None, d, f), lambda i, be, nu: (layer, be[blk(i, be, nu)], 0, 0)),
            pl.BlockSpec((None, None, d, f), lambda i, be, nu: (layer, be[blk(i, be, nu)], 0, 0)),
            pl.BlockSpec((None, None, f, d), lambda i, be, nu: (layer, be[blk(i, be, nu)], 0, 0)),
        ],
        out_specs=pl.BlockSpec((MOE_BLOCK, dp), lambda i, be, nu: (i, 0)),
        scratch_shapes=[pltpu.VMEM((d, f), BF16), pltpu.VMEM((d, f), BF16), pltpu.VMEM((f, d), BF16)],
    )
    return pl.pallas_call(
        _experts_kernel,
        grid_spec=grid_spec,
        out_shape=jax.ShapeDtypeStruct((n_rows, dp), xs.dtype),
        compiler_params=_cparams(("arbitrary",)),
        name="moe_experts",
    )(block_exp, n_used, xs, w1, w3, w2)


def _combine_kernel(slots_ref, ys_ref, x1_ref, route_ref, g2_ref, lng_ref, lnb_ref, out_ref, buf_ref, sem_ref,
                    *, alpha):
    t = pl.program_id(0)
    n_tiles = pl.num_programs(0)
    tm = x1_ref.shape[0]

    def gather_tile(tile, slot):
        base = tile * (2 * tm)

        def issue(r, carry):
            for k in range(2):
                src = ys_ref.at[pl.ds(slots_ref[base + 2 * r + k], 1)]
                pltpu.make_async_copy(src, buf_ref.at[slot, k, pl.ds(r, 1)], sem_ref.at[slot, k]).start()
            return carry

        lax.fori_loop(0, tm, issue, 0, unroll=8)

    @pl.when(t == 0)
    def _():
        gather_tile(0, 0)

    @pl.when(t + 1 < n_tiles)
    def _():
        gather_tile(t + 1, (t + 1) % 2)

    slot = t % 2
    for k in range(2):
        pltpu.make_async_copy(ys_ref.at[pl.ds(0, tm)], buf_ref.at[slot, k], sem_ref.at[slot, k]).wait()
    route = route_ref[...]
    y = (route[:, 2:3] * _unpack_bf16_pairs(buf_ref[slot, 0])
         + route[:, 3:4] * _unpack_bf16_pairs(buf_ref[slot, 1]))
    out_ref[...] = _layer_norm(alpha * x1_ref[...] + g2_ref[...] * y, lng_ref[...], lnb_ref[...])


def _moe_combine(st, ys, slots, x1, route, mods, ln_g, ln_b, alpha):
    n, d = x1.shape
    tm = ROW_TILE
    grid_spec = pltpu.PrefetchScalarGridSpec(
        num_scalar_prefetch=1,
        grid=(n // tm,),
        in_specs=[
            pl.BlockSpec(memory_space=pl.ANY),
            pl.BlockSpec((tm, d), lambda t, s: (t, 0)),
            pl.BlockSpec((tm, LANES), lambda t, s: (t, 0)),
            pl.BlockSpec((None, 1, d), lambda t, s: (st.mod_row(t), 0, 5)),
            pl.BlockSpec((1, d), lambda t, s: (0, 0)),
            pl.BlockSpec((1, d), lambda t, s: (0, 0)),
        ],
        out_specs=pl.BlockSpec((tm, d), lambda t, s: (t, 0)),
        scratch_shapes=[pltpu.VMEM((2, 2, tm, ys.shape[1]), ys.dtype), pltpu.SemaphoreType.DMA((2, 2))],
    )
    return pl.pallas_call(
        functools.partial(_combine_kernel, alpha=alpha),
        grid_spec=grid_spec,
        out_shape=jax.ShapeDtypeStruct((n, d), F32),
        compiler_params=_cparams(("arbitrary",)),
        name="moe_combine_ln",
    )(slots, ys, x1, route, mods, ln_g, ln_b)


def _hier_moe(st, h2, route, x1, mods, ln_g, ln_b, w1, w3, w2, layer, alpha):
    n = h2.shape[0]
    ranks, counts = _moe_ranks(route)
    counts = counts[0, :MOE_EXPERTS]
    padded = (counts + MOE_BLOCK - 1) // MOE_BLOCK * MOE_BLOCK
    pad_end = jnp.cumsum(padded).astype(jnp.int32)
    pad_start = pad_end - padded
    expert = route[:, :2].astype(jnp.int32)
    eids = jnp.arange(MOE_EXPERTS, dtype=jnp.int32)
    start_of = jnp.sum(jnp.where(expert[:, :, None] == eids, pad_start, 0), axis=-1)
    slots = (start_of + ranks[:, :2]).reshape(2 * n)
    n_blocks = -(-(2 * n) // MOE_BLOCK) + MOE_EXPERTS
    block_pos = jnp.arange(n_blocks, dtype=jnp.int32) * MOE_BLOCK
    block_exp = jnp.minimum(jnp.sum((pad_end[None, :] <= block_pos[:, None]).astype(jnp.int32), axis=1),
                            MOE_EXPERTS - 1)
    n_used = (pad_end[-1:] // MOE_BLOCK).astype(jnp.int32)
    xs = _moe_dispatch(h2, slots, pad_start, pad_end, n_blocks * MOE_BLOCK)
    ys = _moe_experts(xs, block_exp, n_used, w1, w3, w2, layer)
    return _moe_combine(st, ys, slots, x1, route, mods, ln_g, ln_b, alpha)


def _swap_halves(w):
    half = w.shape[-1] // 2
    return jnp.concatenate([w[..., half:], w[..., :half]], axis=-1)


def _mla_weights(w_in, w_uq, w_ukv, w_o):
    qr, kvr = MLA_Q_RANK, MLA_KV_RANK
    w_pe = w_in[:, qr + kvr:]
    w_in_ext = jnp.concatenate([w_in, _swap_halves(w_pe)], axis=1).astype(BF16)
    uq = w_uq.reshape(qr, MLA_HEADS, MLA_NOPE + MLA_ROPE)
    q_nope = uq[:, :, :MLA_NOPE].reshape(qr, MLA_HEADS * MLA_NOPE)
    q_pe = uq[:, :, MLA_NOPE:]
    q_pe2 = jnp.concatenate([q_pe, _swap_halves(q_pe)], axis=-1).reshape(qr, MLA_HEADS * 2 * MLA_ROPE)
    w_q = jnp.concatenate([q_nope, q_pe2], axis=1).astype(BF16)
    ukv = w_ukv.reshape(kvr, MLA_HEADS, MLA_NOPE + MLA_V)
    w_kv = jnp.concatenate([ukv[:, :, :MLA_NOPE].reshape(kvr, -1), ukv[:, :, MLA_NOPE:].reshape(kvr, -1)],
                           axis=1).astype(BF16)
    return w_in_ext, w_q, w_kv, w_o.astype(BF16)


def _gla_weights(w_in, gate_a, gate_b, gate_bias, w_o):
    d = w_in.shape[0]
    key_dim = gate_b.shape[-1]
    rank = gate_a.shape[-1]
    ga = jnp.concatenate([gate_a[0], gate_a[1], jnp.zeros((d, LANES - 2 * rank), F32)], axis=1)
    w_qkg = jnp.concatenate([w_in[:, :2 * key_dim], ga], axis=1).astype(BF16)
    w_vr = w_in[:, 2 * key_dim:].astype(BF16)
    gb = jnp.zeros((LANES, 2 * key_dim), F32)
    gb = gb.at[:rank, :key_dim].set(gate_b[0]).at[rank:2 * rank, key_dim:].set(gate_b[1])
    bias = jnp.concatenate([gate_bias[0], gate_bias[1]])[None, :]
    return w_qkg, gb.astype(BF16), bias, w_vr, w_o.astype(BF16)


def _rope_table(seq):
    n_rows = seq // GRID_W
    row = jnp.repeat(jnp.arange(n_rows, dtype=F32), GRID_W)
    col = jnp.tile(jnp.arange(GRID_W, dtype=F32), n_rows)
    n_freq = MLA_ROPE // 4
    inv_freq = jnp.power(ROPE_THETA, -jnp.arange(n_freq, dtype=F32) / n_freq)
    ang = jnp.concatenate([row[:, None] * inv_freq, col[:, None] * inv_freq], axis=-1)
    cos, sin = jnp.cos(ang), jnp.sin(ang)
    lat = jnp.concatenate([cos, cos, -sin, sin], axis=-1)
    ident = jnp.concatenate([jnp.ones((ROW_TILE, MLA_ROPE), F32), jnp.zeros((ROW_TILE, MLA_ROPE), F32)], axis=-1)
    return jnp.concatenate([lat, ident], axis=0)


def kernel(x, c, ctx, c_ctx, w_mod, b_mod, ln1_g, ln1_b, ln2_g, ln2_b, mla_w_in, mla_q_norm, mla_w_uq, mla_kv_norm, mla_w_ukv, mla_w_o, gla_w_in, gla_gate_a, gla_gate_b, gla_gate_bias, gla_norm, gla_w_o, moe_w_grp, moe_b_grp, moe_w_exp, moe_b_exp, moe_w1, moe_w3, moe_w2):
    batch, seq, d = x.shape
    ctx_len = ctx.shape[1]
    depth = w_mod.shape[0]
    assert batch + 1 <= 8
    alpha = (2.0 * depth) ** 0.25
    full = _Stream(batch, seq, ctx_len, d)

    cond = jnp.concatenate([c, c_ctx[None, :], jnp.zeros((8 - batch - 1, d), F32)], axis=0)
    mods_all = _adaln_tables(cond, w_mod, b_mod)
    cs_tab = _rope_table(seq)
    xs = jnp.concatenate([x.reshape(batch * seq, d), ctx.reshape(batch * ctx_len, d)], axis=0)

    for i in range(depth):
        last = i == depth - 1
        mods = mods_all[i].reshape(8, 1, N_MOD * d)
        j = i // 2
        n_tiles = full.lat_tiles if last else full.tiles
        wr = jnp.concatenate([moe_w_grp[i], moe_w_exp[i],
                              jnp.zeros((d, LANES - MOE_GROUPS - MOE_EXPERTS), F32)], axis=1)
        wr_hi = wr.astype(BF16)
        wr_lo = (wr - wr_hi.astype(F32)).astype(BF16)
        rbias = jnp.concatenate([moe_b_grp[i], moe_b_exp[i],
                                 jnp.zeros((LANES - MOE_GROUPS - MOE_EXPERTS,), F32)])[None, :]
        ln1 = (ln1_g[i][None, :], ln1_b[i][None, :])
        if i % 2 == 0:
            w_in_ext, w_q, w_kv, w_o = _mla_weights(mla_w_in[j], mla_w_uq[j], mla_w_ukv[j], mla_w_o[j])
            q, k, v = _mla_proj(full, xs, mods, cs_tab, w_in_ext, mla_q_norm[j][None, :],
                                mla_kv_norm[j][None, :], w_q, w_kv)
            o = _attention(full, q, k, v, latent=True)
            if not last:
                o = jnp.concatenate([o, _attention(full, q, k, v, latent=False)], axis=0)
            x1, h2, route = _mix_out(full, n_tiles, [o], [_row_spec(o.shape[1])],
                                     functools.partial(_mla_out_kernel, alpha=alpha),
                                     xs, mods, ln1[0], ln1[1], w_o, wr_hi, wr_lo, rbias)
        else:
            w_qkg, gb_ext, gbias, w_vr, w_o = _gla_weights(gla_w_in[j], gla_gate_a[j], gla_gate_b[j],
                                                          gla_gate_bias[j], gla_w_o[j])
            q, k, v, r, gf, gb = _gla_proj(full, xs, mods, w_qkg, gb_ext, gbias, w_vr)
            o_f, o_b = _gla_scan(full, q, k, v, gf, gb)
            vd = v.shape[1]
            x1, h2, route = _mix_out(full, n_tiles, [o_f, o_b, r, gla_norm[j][None, :]],
                                     [_row_spec(vd), _row_spec(vd), _row_spec(vd),
                                      _const_spec((1, vd // GLA_HEADS))],
                                     functools.partial(_gla_out_kernel, alpha=alpha, dv=vd // GLA_HEADS),
                                     xs, mods, ln1[0], ln1[1], w_o, wr_hi, wr_lo, rbias)
        xs = _hier_moe(full, h2, route, x1, mods, ln2_g[i][None, :], ln2_b[i][None, :],
                       moe_w1, moe_w3, moe_w2, i, alpha)
    return xs[:batch * seq].reshape(batch, seq, d)
```

```python
import functools
import math

import jax
import jax.numpy as jnp
import numpy as np
from jax import lax
from jax.experimental import pallas as pl
from jax.experimental.pallas import tpu as pltpu

F32 = jnp.float32
BF16 = jnp.bfloat16

GRID_W = 64
N_MOD = 6
NORM_EPS = 1e-6
MLA_HEADS = 16
MLA_Q_RANK = 768
MLA_KV_RANK = 256
MLA_NOPE = 128
MLA_ROPE = 64
MLA_V = 128
ROPE_THETA = 10000.0
GLA_HEADS = 4
GLA_GATE_RANK = 16
GLA_TAU = 16.0
GLA_CHUNK = 64
GLA_SUB = 16
MOE_GROUPS = 4
MOE_PER_GROUP = 8
MOE_EXPERTS = MOE_GROUPS * MOE_PER_GROUP
MOE_BLOCK = 256

LANES = 128
VMEM_LIMIT = 56 * 1024 * 1024

ROW_TILE = 256
RANK_TILE = 1024
MIX_TILE = 512
ATTN_TQ = 1024
ATTN_TK = 1024
ATTN_STREAMS = 2
ATTN_SKEW = 2


def _cparams(sem):
    return pltpu.CompilerParams(dimension_semantics=sem, vmem_limit_bytes=VMEM_LIMIT)


def _sigmoid(x):
    return 1.0 / (1.0 + jnp.exp(-x))


def _layer_norm(z, g, b):
    mu = jnp.mean(z, axis=-1, keepdims=True)
    zc = z - mu
    var = jnp.mean(zc * zc, axis=-1, keepdims=True)
    return zc * lax.rsqrt(var + NORM_EPS) * g + b


def _rms_norm(z, g):
    return z * lax.rsqrt(jnp.mean(z * z, axis=-1, keepdims=True) + NORM_EPS) * g


def _mods_kernel(cond_ref, w_ref, b_ref, o_ref):
    c = cond_ref[...]
    s = c * _sigmoid(c)
    o_ref[...] = jnp.dot(s, w_ref[...], preferred_element_type=F32,
                         precision=lax.Precision.HIGHEST) + b_ref[...]


def _adaln_tables(cond, w_mod, b_mod):
    depth, d, n6 = w_mod.shape
    tn = 1024
    return pl.pallas_call(
        _mods_kernel,
        grid=(depth, n6 // tn),
        in_specs=[
            pl.BlockSpec((8, d), lambda l, j: (0, 0)),
            pl.BlockSpec((None, d, tn), lambda l, j: (l, 0, j)),
            pl.BlockSpec((None, 1, tn), lambda l, j: (l, 0, j)),
        ],
        out_specs=pl.BlockSpec((None, 8, tn), lambda l, j: (l, 0, j)),
        out_shape=jax.ShapeDtypeStruct((depth, 8, n6), F32),
        compiler_params=_cparams(("parallel", "parallel")),
        name="adaln_tables",
    )(cond, w_mod, b_mod.reshape(depth, 1, n6))


class _Stream:
    def __init__(self, batch, seq, ctx_len, d):
        self.batch, self.seq, self.ctx_len, self.d = batch, seq, ctx_len, d
        self.n_lat = batch * seq
        self.n_ctx = batch * ctx_len
        self.n = self.n_lat + self.n_ctx
        assert seq % ROW_TILE == 0 and ctx_len % ROW_TILE == 0
        self.lat_tiles = self.n_lat // ROW_TILE
        self.tiles = self.n // ROW_TILE
        self.tiles_per_batch = seq // ROW_TILE

    def mod_row(self, t, tm=ROW_TILE):
        return jnp.where(t < self.n_lat // tm, t // (self.seq // tm), self.batch)


def _mod_spec(st, chunk, tm=ROW_TILE):
    d = st.d
    return pl.BlockSpec((None, 1, d), lambda t: (st.mod_row(t, tm), 0, chunk))


def _row_spec(width, tm=ROW_TILE):
    return pl.BlockSpec((tm, width), lambda t: (t, 0))


def _const_spec(shape):
    nd = len(shape)
    return pl.BlockSpec(shape, lambda t: (0,) * nd, pipeline_mode=pl.Buffered(1))


def _half_sum(x):
    return x + pltpu.roll(x, LANES // 2, axis=1)


def _mla_proj_kernel(x_ref, sh_ref, sc_ref, cs_ref, w_in_ref, qg_ref, kvg_ref, w_q_ref, w_kv_ref,
                     q_ref, k_ref, v_ref, *, scale):
    h = x_ref[...] * (1.0 + sc_ref[...]) + sh_ref[...]
    lat = jnp.dot(h.astype(BF16), w_in_ref[...], preferred_element_type=F32)
    cq = _rms_norm(lat[:, :MLA_Q_RANK], qg_ref[...]).astype(BF16)
    ckv = _rms_norm(lat[:, MLA_Q_RANK:MLA_Q_RANK + MLA_KV_RANK], kvg_ref[...]).astype(BF16)
    cs = cs_ref[...]
    lane = lax.broadcasted_iota(jnp.int32, cs.shape, 1)
    kpe = _half_sum(lat[:, MLA_Q_RANK + MLA_KV_RANK:] * cs)
    kpe = jnp.where(lane < MLA_ROPE, kpe, 0.0).astype(BF16)
    qall = jnp.dot(cq, w_q_ref[...], preferred_element_type=F32)
    kvall = jnp.dot(ckv, w_kv_ref[...], preferred_element_type=F32)
    hn = MLA_HEADS * MLA_NOPE
    for hd in range(MLA_HEADS):
        lo = hd * LANES
        q_ref[hd, :, :MLA_NOPE] = (qall[:, lo:lo + LANES] * scale).astype(BF16)
        q_ref[hd, :, MLA_NOPE:] = (_half_sum(qall[:, hn + lo:hn + lo + LANES] * cs) * scale).astype(BF16)
        k_ref[hd, :, :MLA_NOPE] = kvall[:, lo:lo + LANES].astype(BF16)
        k_ref[hd, :, MLA_NOPE:] = kpe
        v_ref[hd] = jnp.transpose(kvall[:, hn + lo:hn + lo + LANES]).astype(BF16)


def _mla_proj(st, x, mods, cs_tab, w_in_ext, q_norm, kv_norm, w_q, w_kv):
    n, d = st.n, st.d
    tm = ROW_TILE
    lat_w = w_in_ext.shape[1]
    qk_w = MLA_NOPE + LANES
    scale = (MLA_NOPE + MLA_ROPE) ** -0.5 * math.log2(math.e)
    seq_tiles = st.tiles_per_batch

    def cs_map(t):
        return (jnp.where(t < st.lat_tiles, t % seq_tiles, seq_tiles), 0)

    head_spec = lambda w: pl.BlockSpec((MLA_HEADS, tm, w), lambda t: (0, t, 0))
    return pl.pallas_call(
        functools.partial(_mla_proj_kernel, scale=scale),
        grid=(st.tiles,),
        in_specs=[
            _row_spec(d), _mod_spec(st, 0), _mod_spec(st, 1),
            pl.BlockSpec((tm, LANES), cs_map),
            _const_spec((d, lat_w)), _const_spec((1, MLA_Q_RANK)), _const_spec((1, MLA_KV_RANK)),
            _const_spec(w_q.shape), _const_spec(w_kv.shape),
        ],
        out_specs=[head_spec(qk_w), head_spec(qk_w),
                   pl.BlockSpec((MLA_HEADS, MLA_V, tm), lambda t: (0, 0, t))],
        out_shape=[
            jax.ShapeDtypeStruct((MLA_HEADS, n, qk_w), BF16),
            jax.ShapeDtypeStruct((MLA_HEADS, n, qk_w), BF16),
            jax.ShapeDtypeStruct((MLA_HEADS, MLA_V, n), BF16),
        ],
        compiler_params=_cparams(("parallel",)),
        name="mla_proj",
    )(x, mods, mods, cs_tab, w_in_ext, q_norm, kv_norm, w_q, w_kv)


def _attn_kernel(q_ref, kc_ref, vct_ref, *rest, n_chunks):
    if n_chunks:
        kl_ref, vlt_ref, o_ref = rest
    else:
        (o_ref,) = rest
    nt = (((1,), (1,)), ((), ()))
    tq = q_ref.shape[0]
    n_streams = ATTN_STREAMS if n_chunks else 1
    rows_q = tq // n_streams

    def chunk_rows(j):
        return pl.ds(j * ATTN_TK, ATTN_TK)

    class Stream:
        def __init__(self, idx):
            self.qrows = pl.ds(idx * rows_q, rows_q)
            self.q = q_ref[self.qrows, :]
            self.stage = 0

        def scores(self, k):
            s = lax.dot_general(k, self.q, nt, preferred_element_type=F32)
            return s, jnp.max(s, axis=0, keepdims=True)

        def advance(self):
            j = self.stage - 1
            if self.stage == 0:
                s, m_new = self.scores(kc_ref[...])
                self.nxt = self.scores(kl_ref[chunk_rows(0), :]) if n_chunks else None
                p = jnp.exp2(s - m_new)
                self.l = jnp.sum(p, axis=0, keepdims=True)
                self.acc = jnp.dot(vct_ref[...], p.astype(BF16), preferred_element_type=F32)
            else:
                s, m_blk = self.nxt
                if j + 1 < n_chunks:
                    self.nxt = self.scores(kl_ref[chunk_rows(j + 1), :])
                m_new = jnp.maximum(self.m, m_blk)
                alpha = jnp.exp2(self.m - m_new)
                p = jnp.exp2(s - m_new)
                self.l = alpha * self.l + jnp.sum(p, axis=0, keepdims=True)
                self.acc = alpha * self.acc + jnp.dot(vlt_ref[:, chunk_rows(j)], p.astype(BF16),
                                                      preferred_element_type=F32)
            self.m = m_new
            self.stage += 1
            if self.stage == n_chunks + 1:
                o_ref[self.qrows, :] = jnp.transpose(self.acc / self.l).astype(o_ref.dtype)

    streams = [Stream(i) for i in range(n_streams)]
    n_stages = n_chunks + 1
    for t in range(n_stages + ATTN_SKEW * (n_streams - 1)):
        for i, st in enumerate(streams):
            if 0 <= t - i * ATTN_SKEW < n_stages:
                st.advance()


def _attention(st, q, k, vt, latent):
    b, seq, lc = st.batch, st.seq, st.ctx_len
    qk_w = q.shape[-1]
    tq = ATTN_TQ if latent else min(ATTN_TQ, lc)
    ctx_blk0 = st.n_lat // lc
    if latent:
        nq = seq // tq
        q_map = lambda bi, h, qi: (h, bi * nq + qi, 0)
        o_map = lambda bi, h, qi: (bi * nq + qi, h)
        rows = st.n_lat
        n_chunks = seq // ATTN_TK
    else:
        nq = lc // tq
        q_map = lambda bi, h, qi: (h, st.n_lat // tq + bi * nq + qi, 0)
        o_map = lambda bi, h, qi: (bi * nq + qi, h)
        rows = st.n_ctx
        n_chunks = 0
    kc_map = lambda bi, h, qi: (h, ctx_blk0 + bi, 0)
    kl_map = lambda bi, h, qi: (h, bi, 0)
    vc_map = lambda bi, h, qi: (h, 0, ctx_blk0 + bi)
    vl_map = lambda bi, h, qi: (h, 0, bi)
    in_specs = [
        pl.BlockSpec((None, tq, qk_w), q_map),
        pl.BlockSpec((None, lc, qk_w), kc_map),
        pl.BlockSpec((None, MLA_V, lc), vc_map),
    ]
    args = [q, k, vt]
    if latent:
        in_specs += [pl.BlockSpec((None, seq, qk_w), kl_map), pl.BlockSpec((None, MLA_V, seq), vl_map)]
        args += [k, vt]
    return pl.pallas_call(
        functools.partial(_attn_kernel, n_chunks=n_chunks),
        grid=(b, MLA_HEADS, nq),
        in_specs=in_specs,
        out_specs=pl.BlockSpec((tq, MLA_V), o_map),
        out_shape=jax.ShapeDtypeStruct((rows, MLA_HEADS * MLA_V), BF16),
        compiler_params=_cparams(("parallel", "parallel", "arbitrary")),
        name="mla_attention_latent" if latent else "mla_attention_context",
    )(*args)


def _gla_qkg_kernel(x_ref, sh_ref, sc_ref, w_ref, gb_ref, bias_ref, q_ref, k_ref, gf_ref, gb_out_ref,
                    *, key_dim, q_scale):
    h = (x_ref[...] * (1.0 + sc_ref[...]) + sh_ref[...]).astype(BF16)
    y = jnp.dot(h, w_ref[...], preferred_element_type=F32)
    q_ref[...] = (y[:, :key_dim] * q_scale).astype(BF16)
    k_ref[...] = y[:, key_dim:2 * key_dim].astype(BF16)
    z = jnp.dot(y[:, 2 * key_dim:].astype(BF16), gb_ref[...], preferred_element_type=F32) + bias_ref[...]
    g = (jnp.minimum(z, 0.0) - jnp.log(1.0 + jnp.exp(-jnp.abs(z)))) * (math.log2(math.e) / GLA_TAU)
    gf_ref[...] = g[:, :key_dim]
    gb_out_ref[...] = g[:, key_dim:]


def _gla_vr_kernel(x_ref, sh_ref, sc_ref, w_ref, v_ref, r_ref, *, value_dim):
    h = (x_ref[...] * (1.0 + sc_ref[...]) + sh_ref[...]).astype(BF16)
    y = jnp.dot(h, w_ref[...], preferred_element_type=F32)
    v_ref[...] = y[:, :value_dim].astype(BF16)
    r_ref[...] = y[:, value_dim:].astype(BF16)


def _gla_proj(st, x, mods, w_qkg, gate_b_ext, gate_bias, w_vr):
    n, d = st.n, st.d
    key_dim = (w_qkg.shape[1] - LANES) // 2
    value_dim = w_vr.shape[1] // 2
    dk = key_dim // GLA_HEADS
    q, k, gf, gb = pl.pallas_call(
        functools.partial(_gla_qkg_kernel, key_dim=key_dim, q_scale=dk ** -0.5),
        grid=(st.tiles,),
        in_specs=[_row_spec(d), _mod_spec(st, 0), _mod_spec(st, 1), _const_spec(w_qkg.shape),
                  _const_spec(gate_b_ext.shape), _const_spec(gate_bias.shape)],
        out_specs=[_row_spec(key_dim)] * 4,
        out_shape=[jax.ShapeDtypeStruct((n, key_dim), BF16), jax.ShapeDtypeStruct((n, key_dim), BF16),
                   jax.ShapeDtypeStruct((n, key_dim), F32), jax.ShapeDtypeStruct((n, key_dim), F32)],
        compiler_params=_cparams(("parallel",)),
        name="gla_proj_qkg",
    )(x, mods, mods, w_qkg, gate_b_ext, gate_bias)
    v, r = pl.pallas_call(
        functools.partial(_gla_vr_kernel, value_dim=value_dim),
        grid=(st.tiles,),
        in_specs=[_row_spec(d), _mod_spec(st, 0), _mod_spec(st, 1), _const_spec(w_vr.shape)],
        out_specs=[_row_spec(value_dim)] * 2,
        out_shape=[jax.ShapeDtypeStruct((n, value_dim), BF16), jax.ShapeDtypeStruct((n, value_dim), BF16)],
        compiler_params=_cparams(("parallel",)),
        name="gla_proj_vr",
    )(x, mods, mods, w_vr)
    return q, k, v, r, gf, gb


def _cumsum_rows(x, reverse):
    n = x.shape[0]
    row = lax.broadcasted_iota(jnp.int32, x.shape, 0)
    s = 1
    while s < n:
        if reverse:
            x = x + jnp.where(row < n - s, pltpu.roll(x, n - s, axis=0), 0.0)
        else:
            x = x + jnp.where(row >= s, pltpu.roll(x, s, axis=0), 0.0)
        s *= 2
    return x


def _gla_chunk(q, k, v, g, state_t, reverse):
    c, dk = q.shape
    nsub = c // GLA_SUB
    half = GLA_SUB // 2
    b = _cumsum_rows(g, reverse)
    b_last = jnp.sum(g, axis=0, keepdims=True)
    nt = (((1,), (1,)), ((), ()))
    qe = (q * jnp.exp2(b)).astype(BF16)
    o = lax.dot_general(qe, state_t.astype(BF16), nt, preferred_element_type=F32)

    q_parts, k_parts = [], []
    for src in (range(1, nsub) if reverse else range(nsub - 1)):
        lo = src * GLA_SUB
        if reverse:
            ref = b[lo:lo + 1, :]
            qd = (q[:lo] * jnp.exp2(b[:lo] - ref)).astype(BF16)
            q_parts.append(jnp.concatenate([qd, jnp.zeros((c - lo, dk), BF16)], axis=0))
        else:
            ref = b[lo + GLA_SUB - 1:lo + GLA_SUB, :]
            qd = (q[lo + GLA_SUB:] * jnp.exp2(b[lo + GLA_SUB:] - ref)).astype(BF16)
            q_parts.append(jnp.concatenate([jnp.zeros((lo + GLA_SUB, dk), BF16), qd], axis=0))
        kd = (k[lo:lo + GLA_SUB] * jnp.exp2(ref - b[lo:lo + GLA_SUB])).astype(BF16)
        pieces = [kd]
        if lo:
            pieces.insert(0, jnp.zeros((lo, dk), BF16))
        if c - lo - GLA_SUB:
            pieces.append(jnp.zeros((c - lo - GLA_SUB, dk), BF16))
        k_parts.append(jnp.concatenate(pieces, axis=0))
    off = lax.dot_general(jnp.concatenate(q_parts, axis=1), jnp.concatenate(k_parts, axis=1), nt,
                          preferred_element_type=F32)
    a_mat = jnp.concatenate([off, jnp.zeros((c, LANES - c), F32)], axis=1)

    ones = jnp.ones((dk, LANES), BF16)
    sub_r = lax.broadcasted_iota(jnp.int32, (half, LANES), 0)
    sub_c = lax.broadcasted_iota(jnp.int32, (half, LANES), 1)
    diag_rows = []
    for a in range(nsub):
        lo = a * GLA_SUB
        terms, spans = [], []
        for jj in range(GLA_SUB):
            if reverse:
                r0, r1 = 0, (half if jj < half else GLA_SUB)
            else:
                r0, r1 = (0 if jj < half else half), GLA_SUB
            kj = k[lo + jj:lo + jj + 1, :]
            bj = b[lo + jj:lo + jj + 1, :]
            terms.append(q[lo + r0:lo + r1] * kj * jnp.exp2(b[lo + r0:lo + r1] - bj))
            spans.append((r0, r1))
        sums = jnp.dot(jnp.concatenate(terms, axis=0).astype(BF16), ones, preferred_element_type=F32)
        blk = [jnp.zeros((half, LANES), F32), jnp.zeros((half, LANES), F32)]
        pos = 0
        for jj, (r0, r1) in enumerate(spans):
            for r in range(r0, r1, half):
                piece = sums[pos:pos + half, :]
                pos += half
                rows = sub_r + r
                keep = (sub_c == lo + jj) & ((rows <= jj) if reverse else (rows >= jj))
                blk[r // half] = blk[r // half] + jnp.where(keep, piece, 0.0)
        diag_rows += blk
    a_mat = a_mat + jnp.concatenate(diag_rows, axis=0)
    o = o + jnp.dot(a_mat[:, :c].astype(BF16), v, preferred_element_type=F32)
    kd = (k * jnp.exp2(b_last - b)).astype(BF16)
    tn = (((0,), (0,)), ((), ()))
    new_state = state_t * jnp.exp2(b_last) + lax.dot_general(v, kd, tn, preferred_element_type=F32)
    return o, new_state


def _gla_scan_kernel(qf_ref, kf_ref, vf_ref, gf_ref, qb_ref, kb_ref, vb_ref, gb_ref, of_ref, ob_ref,
                     sf_ref, sb_ref, *, n_chunks):
    @pl.when(pl.program_id(2) == 0)
    def _():
        sf_ref[...] = jnp.zeros_like(sf_ref)
        sb_ref[...] = jnp.zeros_like(sb_ref)

    def step(q_ref, k_ref, v_ref, g_ref, o_ref, state_ref, cidx, reverse):
        rows = pl.ds(cidx * GLA_CHUNK, GLA_CHUNK)
        o, new_state = _gla_chunk(q_ref[rows, :].astype(F32), k_ref[rows, :].astype(F32), v_ref[rows, :],
                                  g_ref[rows, :], state_ref[...], reverse)
        o_ref[rows, :] = o.astype(o_ref.dtype)
        state_ref[...] = new_state

    for cidx in range(n_chunks):
        step(qf_ref, kf_ref, vf_ref, gf_ref, of_ref, sf_ref, cidx, False)
        step(qb_ref, kb_ref, vb_ref, gb_ref, ob_ref, sb_ref, n_chunks - 1 - cidx, True)


def _gla_scan(st, q, k, v, gf, gb):
    b = st.batch
    key_dim, value_dim = q.shape[1], v.shape[1]
    dk, dv = key_dim // GLA_HEADS, value_dim // GLA_HEADS
    t_rows = ROW_TILE
    nc, nl = st.ctx_len // t_rows, st.seq // t_rows

    def row_block(bi, s, reverse):
        if reverse:
            ctx = st.lat_tiles + bi * nc + (nc - 1 - s)
            lat = bi * nl + (nl - 1 - (s - nc))
        else:
            ctx = st.lat_tiles + bi * nc + s
            lat = bi * nl + (s - nc)
        return jnp.where(s < nc, ctx, lat)

    def spec(w, reverse):
        return pl.BlockSpec((t_rows, w), lambda bi, h, s: (row_block(bi, s, reverse), h))

    dir_specs = lambda reverse: [spec(dk, reverse), spec(dk, reverse), spec(dv, reverse), spec(dk, reverse)]
    out = jax.ShapeDtypeStruct((st.n, value_dim), BF16)
    return pl.pallas_call(
        functools.partial(_gla_scan_kernel, n_chunks=t_rows // GLA_CHUNK),
        grid=(b, GLA_HEADS, nc + nl),
        in_specs=dir_specs(False) + dir_specs(True),
        out_specs=[spec(dv, False), spec(dv, True)],
        out_shape=[out, out],
        scratch_shapes=[pltpu.VMEM((dv, dk), F32), pltpu.VMEM((dv, dk), F32)],
        compiler_params=_cparams(("parallel", "parallel", "arbitrary")),
        name="gla_scan_bidir",
    )(q, k, v, gf, q, k, v, gb)


def _route(logits, n_real):
    lane = lax.broadcasted_iota(jnp.int32, logits.shape, 1)
    neg = jnp.float32(-jnp.inf)
    big = jnp.int32(2 ** 30)
    is_grp = lane < MOE_GROUPS
    gl = jnp.where(is_grp, logits, neg)
    gmax = jnp.max(gl, axis=-1, keepdims=True)
    g_top = jnp.min(jnp.where(gl == gmax, lane, big), axis=-1, keepdims=True)
    p_grp = 1.0 / jnp.sum(jnp.exp(gl - gmax), axis=-1, keepdims=True)
    first = MOE_GROUPS + g_top * MOE_PER_GROUP
    in_grp = (lane >= first) & (lane < first + MOE_PER_GROUP)
    el = jnp.where(in_grp, logits, neg)
    emax = jnp.max(el, axis=-1, keepdims=True)
    pe = jnp.exp(el - emax)
    pe = pe / jnp.sum(pe, axis=-1, keepdims=True)
    v1 = jnp.max(pe, axis=-1, keepdims=True)
    i1 = jnp.min(jnp.where(in_grp & (pe == v1), lane, big), axis=-1, keepdims=True)
    rest = jnp.where(in_grp & (lane != i1), pe, -1.0)
    v2 = jnp.max(rest, axis=-1, keepdims=True)
    i2 = jnp.min(jnp.where(rest == v2, lane, big), axis=-1, keepdims=True)
    denom = v1 + v2
    w1 = p_grp * v1 / denom
    w2 = p_grp * v2 / denom
    e1 = (i1 - MOE_GROUPS).astype(F32)
    e2 = (i2 - MOE_GROUPS).astype(F32)
    del n_real
    return jnp.where(lane == 0, e1, jnp.where(lane == 1, e2, jnp.where(lane == 2, w1, jnp.where(lane == 3, w2, 0.0))))


def _pack_bf16_pairs(x):
    half = x.shape[1] // 2
    bits = lax.bitcast_convert_type(x.astype(BF16).astype(F32), jnp.uint32)
    return bits[:, half:] | (bits[:, :half] >> 16)


def _unpack_bf16_pairs(w):
    lo = lax.bitcast_convert_type(w << 16, F32)
    hi = lax.bitcast_convert_type(w & jnp.uint32(0xFFFF0000), F32)
    return jnp.concatenate([lo, hi], axis=1)


MIX_GROUPS = 2


def _row_groups(tm):
    rows = tm // MIX_GROUPS
    return [pl.ds(i * rows, rows) for i in range(MIX_GROUPS)]


def _post_mix(y, rows, x_ref, g1_ref, lng_ref, lnb_ref, sh2_ref, sc2_ref, wr_ref, rb_ref,
              x1_ref, h2_ref, route_ref, alpha):
    x1 = _layer_norm(alpha * x_ref[rows, :] + g1_ref[...] * y, lng_ref[...], lnb_ref[...])
    x1_ref[rows, :] = x1
    h2 = x1 * (1.0 + sc2_ref[...]) + sh2_ref[...]
    h2_ref[rows, :] = _pack_bf16_pairs(h2)
    n = y.shape[0]
    hi = h2.astype(BF16)
    lo = (h2 - hi.astype(F32)).astype(BF16)
    pr = jnp.dot(jnp.concatenate([hi, lo], axis=0), wr_ref[...], preferred_element_type=F32)
    logits = pr[:n, :LANES] + pr[:n, LANES:] + pr[n:, :LANES] + rb_ref[...]
    route_ref[rows, :] = _route(logits, None)


def _mla_out_kernel(o_ref, w_o_ref, *rest, alpha):
    groups = _row_groups(o_ref.shape[0])
    ys = [jnp.dot(o_ref[rows, :], w_o_ref[...], preferred_element_type=F32) for rows in groups]
    for rows, y in zip(groups, ys):
        _post_mix(y, rows, *rest, alpha=alpha)


def _gla_out_kernel(of_ref, ob_ref, r_ref, ng_ref, w_o_ref, *rest, alpha, dv):
    groups = _row_groups(of_ref.shape[0])
    ng = ng_ref[...]
    ys = []
    for rows in groups:
        o = of_ref[rows, :].astype(F32) + ob_ref[rows, :].astype(F32)
        r = r_ref[rows, :].astype(F32)
        gate = r * _sigmoid(r)
        parts = []
        for hd in range(GLA_HEADS):
            parts.append(_rms_norm(o[:, hd * dv:(hd + 1) * dv], ng) * gate[:, hd * dv:(hd + 1) * dv])
        u = jnp.concatenate(parts, axis=1).astype(BF16)
        ys.append(jnp.dot(u, w_o_ref[...], preferred_element_type=F32))
    for rows, y in zip(groups, ys):
        _post_mix(y, rows, *rest, alpha=alpha)


def _mix_out(st, rows, mixer_inputs, mixer_widths, kernel, x, mods, ln_g, ln_b, w_o, wr, rbias):
    d = st.d
    tm = MIX_TILE
    assert rows % tm == 0 and st.seq % tm == 0
    mixer_specs = [_row_spec(w, tm) for w in mixer_widths]
    mixer_specs += [_const_spec(a.shape) for a in mixer_inputs[len(mixer_widths):]]
    common_specs = [
        _row_spec(d, tm), _mod_spec(st, 2, tm), _const_spec((1, d)), _const_spec((1, d)),
        _mod_spec(st, 3, tm), _mod_spec(st, 4, tm), _const_spec(wr.shape), _const_spec((1, LANES)),
    ]
    return pl.pallas_call(
        kernel,
        grid=(rows // tm,),
        in_specs=mixer_specs + [_const_spec(w_o.shape)] + common_specs,
        out_specs=[_row_spec(d, tm), _row_spec(d // 2, tm), _row_spec(LANES, tm)],
        out_shape=[jax.ShapeDtypeStruct((rows, d), F32), jax.ShapeDtypeStruct((rows, d // 2), jnp.uint32),
                   jax.ShapeDtypeStruct((rows, LANES), F32)],
        compiler_params=_cparams(("parallel",)),
        name="mixer_out_ln_route",
    )(*mixer_inputs, w_o, x, mods, ln_g, ln_b, mods, mods, wr, rbias)


def _rank_kernel(route_ref, rank_ref, counts_ref, run_ref):
    t = pl.program_id(0)

    @pl.when(t == 0)
    def _():
        run_ref[...] = jnp.zeros_like(run_ref)

    route = route_ref[...]
    tm = route.shape[0]
    lane = lax.broadcasted_iota(jnp.int32, route.shape, 1)
    e0 = route[:, 0:1].astype(jnp.int32)
    e1 = route[:, 1:2].astype(jnp.int32)
    oh0 = (lane == e0).astype(F32)
    oh1 = (lane == e1).astype(F32)
    both = oh0 + oh1
    ri = lax.broadcasted_iota(jnp.int32, (tm, tm), 0)
    ci = lax.broadcasted_iota(jnp.int32, (tm, tm), 1)
    tri = (ci < ri).astype(BF16)
    before = jnp.dot(tri, both.astype(BF16), preferred_element_type=F32) + run_ref[0:1, :]
    r0 = jnp.sum(oh0 * before, axis=-1, keepdims=True)
    r1 = jnp.sum(oh1 * before, axis=-1, keepdims=True)
    rank_ref[...] = jnp.where(lane == 0, r0, jnp.where(lane == 1, r1, 0.0)).astype(jnp.int32)
    run_ref[...] = run_ref[...] + jnp.sum(both, axis=0, keepdims=True)
    counts_ref[...] = run_ref[...].astype(jnp.int32)


def _moe_ranks(route):
    n = route.shape[0]
    tm = RANK_TILE if n % RANK_TILE == 0 else ROW_TILE
    return pl.pallas_call(
        _rank_kernel,
        grid=(n // tm,),
        in_specs=[_row_spec(LANES, tm)],
        out_specs=[_row_spec(LANES, tm), _const_spec((8, LANES))],
        out_shape=[jax.ShapeDtypeStruct((n, LANES), jnp.int32), jax.ShapeDtypeStruct((8, LANES), jnp.int32)],
        scratch_shapes=[pltpu.VMEM((8, LANES), F32)],
        compiler_params=_cparams(("arbitrary",)),
        name="moe_ranks",
    )(route)


def _dispatch_kernel(slots_ref, pad_start_ref, pad_end_ref, h_ref, xs_ref, zero_ref, sem_ref, zsem_ref):
    t = pl.program_id(0)
    tm = h_ref.shape[0]

    @pl.when(t == 0)
    def _():
        zero_ref[...] = jnp.zeros_like(zero_ref)
        for e in range(MOE_EXPERTS):
            @pl.when(pad_end_ref[e] > pad_start_ref[e])
            def _():
                start = pl.multiple_of(pad_end_ref[e] - MOE_BLOCK, MOE_BLOCK)
                pltpu.make_async_copy(zero_ref, xs_ref.at[pl.ds(start, MOE_BLOCK)], zsem_ref).start()
        n_used = pad_end_ref[MOE_EXPERTS - 1] // MOE_BLOCK
        n_blocks = xs_ref.shape[0] // MOE_BLOCK

        def fill_tail(i, carry):
            start = pl.multiple_of(i * MOE_BLOCK, MOE_BLOCK)
            pltpu.make_async_copy(zero_ref, xs_ref.at[pl.ds(start, MOE_BLOCK)], zsem_ref).start()
            return carry

        def wait_tail(i, carry):
            pltpu.make_async_copy(zero_ref, xs_ref.at[pl.ds(0, MOE_BLOCK)], zsem_ref).wait()
            return carry

        lax.fori_loop(n_used, n_blocks, fill_tail, 0)
        for e in range(MOE_EXPERTS):
            @pl.when(pad_end_ref[e] > pad_start_ref[e])
            def _():
                pltpu.make_async_copy(zero_ref, xs_ref.at[pl.ds(0, MOE_BLOCK)], zsem_ref).wait()
        lax.fori_loop(n_used, n_blocks, wait_tail, 0)

    base = t * (2 * tm)

    def issue(r, carry):
        s0 = slots_ref[base + 2 * r]
        s1 = slots_ref[base + 2 * r + 1]
        pltpu.make_async_copy(h_ref.at[pl.ds(r, 1)], xs_ref.at[pl.ds(s0, 1)], sem_ref.at[0]).start()
        pltpu.make_async_copy(h_ref.at[pl.ds(r, 1)], xs_ref.at[pl.ds(s1, 1)], sem_ref.at[1]).start()
        return carry

    lax.fori_loop(0, tm, issue, 0, unroll=8)
    pltpu.make_async_copy(h_ref, xs_ref.at[pl.ds(0, tm)], sem_ref.at[0]).wait()
    pltpu.make_async_copy(h_ref, xs_ref.at[pl.ds(0, tm)], sem_ref.at[1]).wait()


def _moe_dispatch(h2, slots, pad_start, pad_end, n_rows):
    n, d = h2.shape
    tm = ROW_TILE
    grid_spec = pltpu.PrefetchScalarGridSpec(
        num_scalar_prefetch=3,
        grid=(n // tm,),
        in_specs=[pl.BlockSpec((tm, d), lambda t, *_: (t, 0))],
        out_specs=pl.BlockSpec(memory_space=pl.ANY),
        scratch_shapes=[pltpu.VMEM((MOE_BLOCK, d), h2.dtype), pltpu.SemaphoreType.DMA((2,)),
                        pltpu.SemaphoreType.DMA(())],
    )
    return pl.pallas_call(
        _dispatch_kernel,
        grid_spec=grid_spec,
        out_shape=jax.ShapeDtypeStruct((n_rows, d), h2.dtype),
        compiler_params=_cparams(("arbitrary",)),
        name="moe_dispatch",
    )(slots, pad_start, pad_end, h2)


def _experts_kernel(block_exp_ref, n_used_ref, xs_ref, w1_ref, w3_ref, w2_ref, o_ref, w1b_ref, w3b_ref, w2b_ref):
    i = pl.program_id(0)

    @pl.when(i < n_used_ref[0])
    def _():
        prev = block_exp_ref[jnp.maximum(i - 1, 0)]

        @pl.when((i == 0) | (block_exp_ref[i] != prev))
        def _():
            w1b_ref[...] = w1_ref[...].astype(BF16)
            w3b_ref[...] = w3_ref[...].astype(BF16)
            w2b_ref[...] = w2_ref[...].astype(BF16)

        groups = _row_groups(xs_ref.shape[0])
        ups = []
        for rows in groups:
            xb = _unpack_bf16_pairs(xs_ref[rows, :]).astype(BF16)
            ups.append((jnp.dot(xb, w1b_ref[...], preferred_element_type=F32),
                        jnp.dot(xb, w3b_ref[...], preferred_element_type=F32)))
        for rows, (a, g) in zip(groups, ups):
            hid = (a * _sigmoid(a) * g).astype(BF16)
            o_ref[rows, :] = _pack_bf16_pairs(jnp.dot(hid, w2b_ref[...], preferred_element_type=F32))

    @pl.when(i >= n_used_ref[0])
    def _():
        o_ref[...] = jnp.zeros_like(o_ref)


def _moe_experts(xs, block_exp, n_used, w1, w3, w2, layer):
    n_rows, dp = xs.shape
    n_blocks = n_rows // MOE_BLOCK
    d, f = w1.shape[-2], w1.shape[-1]
    assert dp * 2 == d

    def blk(i, be, nu):
        return jnp.minimum(i, nu[0] - 1)

    grid_spec = pltpu.PrefetchScalarGridSpec(
        num_scalar_prefetch=2,
        grid=(n_blocks,),
        in_specs=[
            pl.BlockSpec((MOE_BLOCK, dp), lambda i, be, nu: (blk(i, be, nu), 0)),
            pl.BlockSpec((None, None, d, f), lambda i, be, nu: (layer, be[blk(i, be, nu)], 0, 0)),
            pl.BlockSpec((None, None, d, f), lambda i, be, nu: (layer, be[blk(i, be, nu)], 0, 0)),
            pl.BlockSpec((None, None, f, d), lambda i, be, nu: (layer, be[blk(i, be, nu)], 0, 0)),
        ],
        out_specs=pl.BlockSpec((MOE_BLOCK, dp), lambda i, be, nu: (i, 0)),
        scratch_shapes=[pltpu.VMEM((d, f), BF16), pltpu.VMEM((d, f), BF16), pltpu.VMEM((f, d), BF16)],
    )
    return pl.pallas_call(
        _experts_kernel,
        grid_spec=grid_spec,
        out_shape=jax.ShapeDtypeStruct((n_rows, dp), xs.dtype),
        compiler_params=_cparams(("arbitrary",)),
        name="moe_experts",
    )(block_exp, n_used, xs, w1, w3, w2)


def _combine_kernel(slots_ref, ys_ref, x1_ref, route_ref, g2_ref, lng_ref, lnb_ref, out_ref, buf_ref, sem_ref,
                    *, alpha):
    t = pl.program_id(0)
    n_tiles = pl.num_programs(0)
    tm = x1_ref.shape[0]

    def gather_tile(tile, slot):
        base = tile * (2 * tm)

        def issue(r, carry):
            for k in range(2):
                src = ys_ref.at[pl.ds(slots_ref[base + 2 * r + k], 1)]
                pltpu.make_async_copy(src, buf_ref.at[slot, k, pl.ds(r, 1)], sem_ref.at[slot, k]).start()
            return carry

        lax.fori_loop(0, tm, issue, 0, unroll=8)

    @pl.when(t == 0)
    def _():
        gather_tile(0, 0)

    @pl.when(t + 1 < n_tiles)
    def _():
        gather_tile(t + 1, (t + 1) % 2)

    slot = t % 2
    for k in range(2):
        pltpu.make_async_copy(ys_ref.at[pl.ds(0, tm)], buf_ref.at[slot, k], sem_ref.at[slot, k]).wait()
    route = route_ref[...]
    y = (route[:, 2:3] * _unpack_bf16_pairs(buf_ref[slot, 0])
         + route[:, 3:4] * _unpack_bf16_pairs(buf_ref[slot, 1]))
    out_ref[...] = _layer_norm(alpha * x1_ref[...] + g2_ref[...] * y, lng_ref[...], lnb_ref[...])


def _moe_combine(st, ys, slots, x1, route, mods, ln_g, ln_b, alpha):
    n, d = x1.shape
    tm = ROW_TILE
    grid_spec = pltpu.PrefetchScalarGridSpec(
        num_scalar_prefetch=1,
        grid=(n // tm,),
        in_specs=[
            pl.BlockSpec(memory_space=pl.ANY),
            pl.BlockSpec((tm, d), lambda t, s: (t, 0)),
            pl.BlockSpec((tm, LANES), lambda t, s: (t, 0)),
            pl.BlockSpec((None, 1, d), lambda t, s: (st.mod_row(t), 0, 5)),
            pl.BlockSpec((1, d), lambda t, s: (0, 0)),
            pl.BlockSpec((1, d), lambda t, s: (0, 0)),
        ],
        out_specs=pl.BlockSpec((tm, d), lambda t, s: (t, 0)),
        scratch_shapes=[pltpu.VMEM((2, 2, tm, ys.shape[1]), ys.dtype), pltpu.SemaphoreType.DMA((2, 2))],
    )
    return pl.pallas_call(
        functools.partial(_combine_kernel, alpha=alpha),
        grid_spec=grid_spec,
        out_shape=jax.ShapeDtypeStruct((n, d), F32),
        compiler_params=_cparams(("arbitrary",)),
        name="moe_combine_ln",
    )(slots, ys, x1, route, mods, ln_g, ln_b)


def _hier_moe(st, h2, route, x1, mods, ln_g, ln_b, w1, w3, w2, layer, alpha):
    n = h2.shape[0]
    ranks, counts = _moe_ranks(route)
    counts = counts[0, :MOE_EXPERTS]
    padded = (counts + MOE_BLOCK - 1) // MOE_BLOCK * MOE_BLOCK
    pad_end = jnp.cumsum(padded).astype(jnp.int32)
    pad_start = pad_end - padded
    expert = route[:, :2].astype(jnp.int32)
    eids = jnp.arange(MOE_EXPERTS, dtype=jnp.int32)
    start_of = jnp.sum(jnp.where(expert[:, :, None] == eids, pad_start, 0), axis=-1)
    slots = (start_of + ranks[:, :2]).reshape(2 * n)
    n_blocks = -(-(2 * n) // MOE_BLOCK) + MOE_EXPERTS
    block_pos = jnp.arange(n_blocks, dtype=jnp.int32) * MOE_BLOCK
    block_exp = jnp.minimum(jnp.sum((pad_end[None, :] <= block_pos[:, None]).astype(jnp.int32), axis=1),
                            MOE_EXPERTS - 1)
    n_used = (pad_end[-1:] // MOE_BLOCK).astype(jnp.int32)
    xs = _moe_dispatch(h2, slots, pad_start, pad_end, n_blocks * MOE_BLOCK)
    ys = _moe_experts(xs, block_exp, n_used, w1, w3, w2, layer)
    return _moe_combine(st, ys, slots, x1, route, mods, ln_g, ln_b, alpha)


def _swap_halves(w):
    half = w.shape[-1] // 2
    return jnp.concatenate([w[..., half:], w[..., :half]], axis=-1)


def _mla_weights(w_in, w_uq, w_ukv, w_o):
    qr, kvr = MLA_Q_RANK, MLA_KV_RANK
    w_pe = w_in[:, qr + kvr:]
    w_in_ext = jnp.concatenate([w_in, _swap_halves(w_pe)], axis=1).astype(BF16)
    uq = w_uq.reshape(qr, MLA_HEADS, MLA_NOPE + MLA_ROPE)
    q_nope = uq[:, :, :MLA_NOPE].reshape(qr, MLA_HEADS * MLA_NOPE)
    q_pe = uq[:, :, MLA_NOPE:]
    q_pe2 = jnp.concatenate([q_pe, _swap_halves(q_pe)], axis=-1).reshape(qr, MLA_HEADS * 2 * MLA_ROPE)
    w_q = jnp.concatenate([q_nope, q_pe2], axis=1).astype(BF16)
    ukv = w_ukv.reshape(kvr, MLA_HEADS, MLA_NOPE + MLA_V)
    w_kv = jnp.concatenate([ukv[:, :, :MLA_NOPE].reshape(kvr, -1), ukv[:, :, MLA_NOPE:].reshape(kvr, -1)],
                           axis=1).astype(BF16)
    return w_in_ext, w_q, w_kv, w_o.astype(BF16)


def _gla_weights(w_in, gate_a, gate_b, gate_bias, w_o):
    d = w_in.shape[0]
    key_dim = gate_b.shape[-1]
    rank = gate_a.shape[-1]
    ga = jnp.concatenate([gate_a[0], gate_a[1], jnp.zeros((d, LANES - 2 * rank), F32)], axis=1)
    w_qkg = jnp.concatenate([w_in[:, :2 * key_dim], ga], axis=1).astype(BF16)
    w_vr = w_in[:, 2 * key_dim:].astype(BF16)
    gb = jnp.zeros((LANES, 2 * key_dim), F32)
    gb = gb.at[:rank, :key_dim].set(gate_b[0]).at[rank:2 * rank, key_dim:].set(gate_b[1])
    bias = jnp.concatenate([gate_bias[0], gate_bias[1]])[None, :]
    return w_qkg, gb.astype(BF16), bias, w_vr, w_o.astype(BF16)


def _rope_table(seq):
    n_rows = seq // GRID_W
    row = jnp.repeat(jnp.arange(n_rows, dtype=F32), GRID_W)
    col = jnp.tile(jnp.arange(GRID_W, dtype=F32), n_rows)
    n_freq = MLA_ROPE // 4
    inv_freq = jnp.power(ROPE_THETA, -jnp.arange(n_freq, dtype=F32) / n_freq)
    ang = jnp.concatenate([row[:, None] * inv_freq, col[:, None] * inv_freq], axis=-1)
    cos, sin = jnp.cos(ang), jnp.sin(ang)
    lat = jnp.concatenate([cos, cos, -sin, sin], axis=-1)
    ident = jnp.concatenate([jnp.ones((ROW_TILE, MLA_ROPE), F32), jnp.zeros((ROW_TILE, MLA_ROPE), F32)], axis=-1)
    return jnp.concatenate([lat, ident], axis=0)


def kernel(x, c, ctx, c_ctx, w_mod, b_mod, ln1_g, ln1_b, ln2_g, ln2_b, mla_w_in, mla_q_norm, mla_w_uq, mla_kv_norm, mla_w_ukv, mla_w_o, gla_w_in, gla_gate_a, gla_gate_b, gla_gate_bias, gla_norm, gla_w_o, moe_w_grp, moe_b_grp, moe_w_exp, moe_b_exp, moe_w1, moe_w3, moe_w2):
    batch, seq, d = x.shape
    ctx_len = ctx.shape[1]
    depth = w_mod.shape[0]
    assert batch + 1 <= 8
    alpha = (2.0 * depth) ** 0.25
    full = _Stream(batch, seq, ctx_len, d)

    cond = jnp.concatenate([c, c_ctx[None, :], jnp.zeros((8 - batch - 1, d), F32)], axis=0)
    mods_all = _adaln_tables(cond, w_mod, b_mod)
    cs_tab = _rope_table(seq)
    xs = jnp.concatenate([x.reshape(batch * seq, d), ctx.reshape(batch * ctx_len, d)], axis=0)

    for i in range(depth):
        last = i == depth - 1
        mods = mods_all[i].reshape(8, 1, N_MOD * d)
        j = i // 2
        n_rows = full.n_lat if last else full.n
        wr = jnp.concatenate([moe_w_grp[i], moe_w_exp[i],
                              jnp.zeros((d, LANES - MOE_GROUPS - MOE_EXPERTS), F32)], axis=1)
        wr_hi = wr.astype(BF16)
        wr_lo = (wr - wr_hi.astype(F32)).astype(BF16)
        wr_split = jnp.concatenate([wr_hi, wr_lo], axis=1)
        rbias = jnp.concatenate([moe_b_grp[i], moe_b_exp[i],
                                 jnp.zeros((LANES - MOE_GROUPS - MOE_EXPERTS,), F32)])[None, :]
        ln1 = (ln1_g[i][None, :], ln1_b[i][None, :])
        if i % 2 == 0:
            w_in_ext, w_q, w_kv, w_o = _mla_weights(mla_w_in[j], mla_w_uq[j], mla_w_ukv[j], mla_w_o[j])
            q, k, v = _mla_proj(full, xs, mods, cs_tab, w_in_ext, mla_q_norm[j][None, :],
                                mla_kv_norm[j][None, :], w_q, w_kv)
            o = _attention(full, q, k, v, latent=True)
            if not last:
                o = jnp.concatenate([o, _attention(full, q, k, v, latent=False)], axis=0)
            x1, h2, route = _mix_out(full, n_rows, [o], [o.shape[1]],
                                     functools.partial(_mla_out_kernel, alpha=alpha),
                                     xs, mods, ln1[0], ln1[1], w_o, wr_split, rbias)
        else:
            w_qkg, gb_ext, gbias, w_vr, w_o = _gla_weights(gla_w_in[j], gla_gate_a[j], gla_gate_b[j],
                                                          gla_gate_bias[j], gla_w_o[j])
            q, k, v, r, gf, gb = _gla_proj(full, xs, mods, w_qkg, gb_ext, gbias, w_vr)
            o_f, o_b = _gla_scan(full, q, k, v, gf, gb)
            vd = v.shape[1]
            x1, h2, route = _mix_out(full, n_rows, [o_f, o_b, r, gla_norm[j][None, :]], [vd, vd, vd],
                                     functools.partial(_gla_out_kernel, alpha=alpha, dv=vd // GLA_HEADS),
                                     xs, mods, ln1[0], ln1[1], w_o, wr_split, rbias)
        xs = _hier_moe(full, h2, route, x1, mods, ln2_g[i][None, :], ln2_b[i][None, :],
                       moe_w1, moe_w3, moe_w2, i, alpha)
    return xs[:batch * seq].reshape(batch, seq, d)
```

```python
import functools
import math

import jax
import jax.numpy as jnp
import numpy as np
from jax import lax
from jax.experimental import pallas as pl
from jax.experimental.pallas import tpu as pltpu

F32 = jnp.float32
BF16 = jnp.bfloat16

GRID_W = 64
N_MOD = 6
NORM_EPS = 1e-6
MLA_HEADS = 16
MLA_Q_RANK = 768
MLA_KV_RANK = 256
MLA_NOPE = 128
MLA_ROPE = 64
MLA_V = 128
ROPE_THETA = 10000.0
GLA_HEADS = 4
GLA_GATE_RANK = 16
GLA_TAU = 16.0
GLA_CHUNK = 64
GLA_SUB = 16
MOE_GROUPS = 4
MOE_PER_GROUP = 8
MOE_EXPERTS = MOE_GROUPS * MOE_PER_GROUP
MOE_BLOCK = 256

LANES = 128
VMEM_LIMIT = 56 * 1024 * 1024

ROW_TILE = 256
RANK_TILE = 1024
MIX_TILE = 512
ATTN_TQ = 1024
ATTN_TK = 1024
ATTN_STREAMS = 2
ATTN_SKEW = 2


def _cparams(sem):
    return pltpu.CompilerParams(dimension_semantics=sem, vmem_limit_bytes=VMEM_LIMIT)


def _sigmoid(x):
    return 1.0 / (1.0 + jnp.exp(-x))


def _layer_norm(z, g, b):
    mu = jnp.mean(z, axis=-1, keepdims=True)
    zc = z - mu
    var = jnp.mean(zc * zc, axis=-1, keepdims=True)
    return zc * lax.rsqrt(var + NORM_EPS) * g + b


def _rms_norm(z, g):
    return z * lax.rsqrt(jnp.mean(z * z, axis=-1, keepdims=True) + NORM_EPS) * g


def _mods_kernel(cond_ref, w_ref, b_ref, o_ref):
    c = cond_ref[...]
    s = c * _sigmoid(c)
    s1 = s.astype(BF16).astype(F32)
    s2 = (s - s1).astype(BF16).astype(F32)
    s3 = s - s1 - s2
    w = w_ref[...]
    wh = w.astype(BF16)
    wl = (w - wh.astype(F32)).astype(BF16)
    rows = s.shape[0]
    top = jnp.dot(jnp.concatenate([s1, s2, s3, jnp.zeros_like(s)], axis=0).astype(BF16), wh,
                  preferred_element_type=F32)
    bot = jnp.dot(jnp.concatenate([s1, s2], axis=0).astype(BF16), wl, preferred_element_type=F32)
    o_ref[...] = (top[:rows] + top[rows:2 * rows] + top[2 * rows:3 * rows]
                  + bot[:rows] + bot[rows:]) + b_ref[...]


def _adaln_tables(cond, w_mod, b_mod):
    depth, d, n6 = w_mod.shape
    tn = 2048
    return pl.pallas_call(
        _mods_kernel,
        grid=(depth, n6 // tn),
        in_specs=[
            pl.BlockSpec((8, d), lambda l, j: (0, 0)),
            pl.BlockSpec((None, d, tn), lambda l, j: (l, 0, j)),
            pl.BlockSpec((None, 1, tn), lambda l, j: (l, 0, j)),
        ],
        out_specs=pl.BlockSpec((None, 8, tn), lambda l, j: (l, 0, j)),
        out_shape=jax.ShapeDtypeStruct((depth, 8, n6), F32),
        compiler_params=_cparams(("parallel", "parallel")),
        name="adaln_tables",
    )(cond, w_mod, b_mod.reshape(depth, 1, n6))


class _Stream:
    def __init__(self, batch, seq, ctx_len, d):
        self.batch, self.seq, self.ctx_len, self.d = batch, seq, ctx_len, d
        self.n_lat = batch * seq
        self.n_ctx = batch * ctx_len
        self.n = self.n_lat + self.n_ctx
        assert seq % ROW_TILE == 0 and ctx_len % ROW_TILE == 0
        self.lat_tiles = self.n_lat // ROW_TILE
        self.tiles = self.n // ROW_TILE
        self.tiles_per_batch = seq // ROW_TILE

    def mod_row(self, t, tm=ROW_TILE):
        return jnp.where(t < self.n_lat // tm, t // (self.seq // tm), self.batch)


def _mod_spec(st, chunk, tm=ROW_TILE):
    d = st.d
    return pl.BlockSpec((None, 1, d), lambda t: (st.mod_row(t, tm), 0, chunk))


def _row_spec(width, tm=ROW_TILE):
    return pl.BlockSpec((tm, width), lambda t: (t, 0))


def _const_spec(shape):
    nd = len(shape)
    return pl.BlockSpec(shape, lambda t: (0,) * nd, pipeline_mode=pl.Buffered(1))


def _half_sum(x):
    return x + pltpu.roll(x, LANES // 2, axis=1)


def _mla_proj_kernel(x_ref, sh_ref, sc_ref, cs_ref, w_in_ref, qg_ref, kvg_ref, w_q_ref, w_kv_ref,
                     q_ref, k_ref, v_ref, *, scale):
    h = x_ref[...] * (1.0 + sc_ref[...]) + sh_ref[...]
    lat = jnp.dot(h.astype(BF16), w_in_ref[...], preferred_element_type=F32)
    cq = _rms_norm(lat[:, :MLA_Q_RANK], qg_ref[...]).astype(BF16)
    ckv = _rms_norm(lat[:, MLA_Q_RANK:MLA_Q_RANK + MLA_KV_RANK], kvg_ref[...]).astype(BF16)
    cs = cs_ref[...]
    lane = lax.broadcasted_iota(jnp.int32, cs.shape, 1)
    kpe = _half_sum(lat[:, MLA_Q_RANK + MLA_KV_RANK:] * cs)
    kpe = jnp.where(lane < MLA_ROPE, kpe, 0.0).astype(BF16)
    qall = jnp.dot(cq, w_q_ref[...], preferred_element_type=F32)
    kvall = jnp.dot(ckv, w_kv_ref[...], preferred_element_type=F32)
    hn = MLA_HEADS * MLA_NOPE
    for hd in range(MLA_HEADS):
        lo = hd * LANES
        q_ref[hd, :, :MLA_NOPE] = (qall[:, lo:lo + LANES] * scale).astype(BF16)
        q_ref[hd, :, MLA_NOPE:] = (_half_sum(qall[:, hn + lo:hn + lo + LANES] * cs) * scale).astype(BF16)
        k_ref[hd, :, :MLA_NOPE] = kvall[:, lo:lo + LANES].astype(BF16)
        k_ref[hd, :, MLA_NOPE:] = kpe
        v_ref[hd] = jnp.transpose(kvall[:, hn + lo:hn + lo + LANES]).astype(BF16)


def _mla_proj(st, x, mods, cs_tab, w_in_ext, q_norm, kv_norm, w_q, w_kv):
    n, d = st.n, st.d
    tm = ROW_TILE
    lat_w = w_in_ext.shape[1]
    qk_w = MLA_NOPE + LANES
    scale = (MLA_NOPE + MLA_ROPE) ** -0.5 * math.log2(math.e)
    seq_tiles = st.tiles_per_batch

    def cs_map(t):
        return (jnp.where(t < st.lat_tiles, t % seq_tiles, seq_tiles), 0)

    head_spec = lambda w: pl.BlockSpec((MLA_HEADS, tm, w), lambda t: (0, t, 0))
    return pl.pallas_call(
        functools.partial(_mla_proj_kernel, scale=scale),
        grid=(st.tiles,),
        in_specs=[
            _row_spec(d), _mod_spec(st, 0), _mod_spec(st, 1),
            pl.BlockSpec((tm, LANES), cs_map),
            _const_spec((d, lat_w)), _const_spec((1, MLA_Q_RANK)), _const_spec((1, MLA_KV_RANK)),
            _const_spec(w_q.shape), _const_spec(w_kv.shape),
        ],
        out_specs=[head_spec(qk_w), head_spec(qk_w),
                   pl.BlockSpec((MLA_HEADS, MLA_V, tm), lambda t: (0, 0, t))],
        out_shape=[
            jax.ShapeDtypeStruct((MLA_HEADS, n, qk_w), BF16),
            jax.ShapeDtypeStruct((MLA_HEADS, n, qk_w), BF16),
            jax.ShapeDtypeStruct((MLA_HEADS, MLA_V, n), BF16),
        ],
        compiler_params=_cparams(("parallel",)),
        name="mla_proj",
    )(x, mods, mods, cs_tab, w_in_ext, q_norm, kv_norm, w_q, w_kv)


def _attn_kernel(q_ref, kc_ref, vct_ref, *rest, n_chunks):
    if n_chunks:
        kl_ref, vlt_ref, o_ref = rest
    else:
        (o_ref,) = rest
    nt = (((1,), (1,)), ((), ()))
    tq = q_ref.shape[0]
    n_streams = ATTN_STREAMS if n_chunks else 1
    rows_q = tq // n_streams

    def chunk_rows(j):
        return pl.ds(j * ATTN_TK, ATTN_TK)

    class Stream:
        def __init__(self, idx):
            self.qrows = pl.ds(idx * rows_q, rows_q)
            self.q = q_ref[self.qrows, :]
            self.stage = 0

        def scores(self, k):
            s = lax.dot_general(k, self.q, nt, preferred_element_type=F32)
            return s, jnp.max(s, axis=0, keepdims=True)

        def advance(self):
            j = self.stage - 1
            if self.stage == 0:
                s, m_new = self.scores(kc_ref[...])
                self.nxt = self.scores(kl_ref[chunk_rows(0), :]) if n_chunks else None
                p = jnp.exp2(s - m_new)
                self.l = jnp.sum(p, axis=0, keepdims=True)
                self.acc = jnp.dot(vct_ref[...], p.astype(BF16), preferred_element_type=F32)
            else:
                s, m_blk = self.nxt
                if j + 1 < n_chunks:
                    self.nxt = self.scores(kl_ref[chunk_rows(j + 1), :])
                m_new = jnp.maximum(self.m, m_blk)
                alpha = jnp.exp2(self.m - m_new)
                p = jnp.exp2(s - m_new)
                self.l = alpha * self.l + jnp.sum(p, axis=0, keepdims=True)
                self.acc = alpha * self.acc + jnp.dot(vlt_ref[:, chunk_rows(j)], p.astype(BF16),
                                                      preferred_element_type=F32)
            self.m = m_new
            self.stage += 1
            if self.stage == n_chunks + 1:
                o_ref[self.qrows, :] = jnp.transpose(self.acc / self.l).astype(o_ref.dtype)

    streams = [Stream(i) for i in range(n_streams)]
    n_stages = n_chunks + 1
    for t in range(n_stages + ATTN_SKEW * (n_streams - 1)):
        for i, st in enumerate(streams):
            if 0 <= t - i * ATTN_SKEW < n_stages:
                st.advance()


def _attention(st, q, k, vt, latent):
    b, seq, lc = st.batch, st.seq, st.ctx_len
    qk_w = q.shape[-1]
    tq = ATTN_TQ if latent else min(ATTN_TQ, lc)
    ctx_blk0 = st.n_lat // lc
    if latent:
        nq = seq // tq
        q_map = lambda bi, h, qi: (h, bi * nq + qi, 0)
        o_map = lambda bi, h, qi: (bi * nq + qi, h)
        rows = st.n_lat
        n_chunks = seq // ATTN_TK
    else:
        nq = lc // tq
        q_map = lambda bi, h, qi: (h, st.n_lat // tq + bi * nq + qi, 0)
        o_map = lambda bi, h, qi: (bi * nq + qi, h)
        rows = st.n_ctx
        n_chunks = 0
    kc_map = lambda bi, h, qi: (h, ctx_blk0 + bi, 0)
    kl_map = lambda bi, h, qi: (h, bi, 0)
    vc_map = lambda bi, h, qi: (h, 0, ctx_blk0 + bi)
    vl_map = lambda bi, h, qi: (h, 0, bi)
    in_specs = [
        pl.BlockSpec((None, tq, qk_w), q_map),
        pl.BlockSpec((None, lc, qk_w), kc_map),
        pl.BlockSpec((None, MLA_V, lc), vc_map),
    ]
    args = [q, k, vt]
    if latent:
        in_specs += [pl.BlockSpec((None, seq, qk_w), kl_map), pl.BlockSpec((None, MLA_V, seq), vl_map)]
        args += [k, vt]
    return pl.pallas_call(
        functools.partial(_attn_kernel, n_chunks=n_chunks),
        grid=(b, MLA_HEADS, nq),
        in_specs=in_specs,
        out_specs=pl.BlockSpec((tq, MLA_V), o_map),
        out_shape=jax.ShapeDtypeStruct((rows, MLA_HEADS * MLA_V), BF16),
        compiler_params=_cparams(("parallel", "parallel", "arbitrary")),
        name="mla_attention_latent" if latent else "mla_attention_context",
    )(*args)


def _gla_qkg_kernel(x_ref, sh_ref, sc_ref, w_ref, gb_ref, bias_ref, q_ref, k_ref, gf_ref, gb_out_ref,
                    *, key_dim, q_scale):
    h = (x_ref[...] * (1.0 + sc_ref[...]) + sh_ref[...]).astype(BF16)
    y = jnp.dot(h, w_ref[...], preferred_element_type=F32)
    q_ref[...] = (y[:, :key_dim] * q_scale).astype(BF16)
    k_ref[...] = y[:, key_dim:2 * key_dim].astype(BF16)
    z = jnp.dot(y[:, 2 * key_dim:].astype(BF16), gb_ref[...], preferred_element_type=F32) + bias_ref[...]
    g = (jnp.minimum(z, 0.0) - jnp.log(1.0 + jnp.exp(-jnp.abs(z)))) * (math.log2(math.e) / GLA_TAU)
    gf_ref[...] = g[:, :key_dim]
    gb_out_ref[...] = g[:, key_dim:]


def _gla_vr_kernel(x_ref, sh_ref, sc_ref, w_ref, v_ref, r_ref, *, value_dim):
    h = (x_ref[...] * (1.0 + sc_ref[...]) + sh_ref[...]).astype(BF16)
    y = jnp.dot(h, w_ref[...], preferred_element_type=F32)
    v_ref[...] = y[:, :value_dim].astype(BF16)
    r_ref[...] = y[:, value_dim:].astype(BF16)


def _gla_proj(st, x, mods, w_qkg, gate_b_ext, gate_bias, w_vr):
    n, d = st.n, st.d
    key_dim = (w_qkg.shape[1] - LANES) // 2
    value_dim = w_vr.shape[1] // 2
    dk = key_dim // GLA_HEADS
    q, k, gf, gb = pl.pallas_call(
        functools.partial(_gla_qkg_kernel, key_dim=key_dim, q_scale=dk ** -0.5),
        grid=(st.tiles,),
        in_specs=[_row_spec(d), _mod_spec(st, 0), _mod_spec(st, 1), _const_spec(w_qkg.shape),
                  _const_spec(gate_b_ext.shape), _const_spec(gate_bias.shape)],
        out_specs=[_row_spec(key_dim)] * 4,
        out_shape=[jax.ShapeDtypeStruct((n, key_dim), BF16), jax.ShapeDtypeStruct((n, key_dim), BF16),
                   jax.ShapeDtypeStruct((n, key_dim), F32), jax.ShapeDtypeStruct((n, key_dim), F32)],
        compiler_params=_cparams(("parallel",)),
        name="gla_proj_qkg",
    )(x, mods, mods, w_qkg, gate_b_ext, gate_bias)
    v, r = pl.pallas_call(
        functools.partial(_gla_vr_kernel, value_dim=value_dim),
        grid=(st.tiles,),
        in_specs=[_row_spec(d), _mod_spec(st, 0), _mod_spec(st, 1), _const_spec(w_vr.shape)],
        out_specs=[_row_spec(value_dim)] * 2,
        out_shape=[jax.ShapeDtypeStruct((n, value_dim), BF16), jax.ShapeDtypeStruct((n, value_dim), BF16)],
        compiler_params=_cparams(("parallel",)),
        name="gla_proj_vr",
    )(x, mods, mods, w_vr)
    return q, k, v, r, gf, gb


def _cumsum_rows(x, reverse):
    n = x.shape[0]
    row = lax.broadcasted_iota(jnp.int32, x.shape, 0)
    s = 1
    while s < n:
        if reverse:
            x = x + jnp.where(row < n - s, pltpu.roll(x, n - s, axis=0), 0.0)
        else:
            x = x + jnp.where(row >= s, pltpu.roll(x, s, axis=0), 0.0)
        s *= 2
    return x


def _gla_chunk(q, k, v, g, state_t, reverse):
    c, dk = q.shape
    nsub = c // GLA_SUB
    half = GLA_SUB // 2
    b = _cumsum_rows(g, reverse)
    b_last = jnp.sum(g, axis=0, keepdims=True)
    nt = (((1,), (1,)), ((), ()))
    qe = (q * jnp.exp2(b)).astype(BF16)
    o = lax.dot_general(qe, state_t.astype(BF16), nt, preferred_element_type=F32)

    q_parts, k_parts = [], []
    for src in (range(1, nsub) if reverse else range(nsub - 1)):
        lo = src * GLA_SUB
        if reverse:
            ref = b[lo:lo + 1, :]
            qd = (q[:lo] * jnp.exp2(b[:lo] - ref)).astype(BF16)
            q_parts.append(jnp.concatenate([qd, jnp.zeros((c - lo, dk), BF16)], axis=0))
        else:
            ref = b[lo + GLA_SUB - 1:lo + GLA_SUB, :]
            qd = (q[lo + GLA_SUB:] * jnp.exp2(b[lo + GLA_SUB:] - ref)).astype(BF16)
            q_parts.append(jnp.concatenate([jnp.zeros((lo + GLA_SUB, dk), BF16), qd], axis=0))
        kd = (k[lo:lo + GLA_SUB] * jnp.exp2(ref - b[lo:lo + GLA_SUB])).astype(BF16)
        pieces = [kd]
        if lo:
            pieces.insert(0, jnp.zeros((lo, dk), BF16))
        if c - lo - GLA_SUB:
            pieces.append(jnp.zeros((c - lo - GLA_SUB, dk), BF16))
        k_parts.append(jnp.concatenate(pieces, axis=0))
    off = lax.dot_general(jnp.concatenate(q_parts, axis=1), jnp.concatenate(k_parts, axis=1), nt,
                          preferred_element_type=F32)
    a_mat = jnp.concatenate([off, jnp.zeros((c, LANES - c), F32)], axis=1)

    ones = jnp.ones((dk, LANES), BF16)
    sub_r = lax.broadcasted_iota(jnp.int32, (half, LANES), 0)
    sub_c = lax.broadcasted_iota(jnp.int32, (half, LANES), 1)
    diag_rows = []
    for a in range(nsub):
        lo = a * GLA_SUB
        terms, spans = [], []
        for jj in range(GLA_SUB):
            if reverse:
                r0, r1 = 0, (half if jj < half else GLA_SUB)
            else:
                r0, r1 = (0 if jj < half else half), GLA_SUB
            kj = k[lo + jj:lo + jj + 1, :]
            bj = b[lo + jj:lo + jj + 1, :]
            terms.append(q[lo + r0:lo + r1] * kj * jnp.exp2(b[lo + r0:lo + r1] - bj))
            spans.append((r0, r1))
        sums = jnp.dot(jnp.concatenate(terms, axis=0).astype(BF16), ones, preferred_element_type=F32)
        blk = [jnp.zeros((half, LANES), F32), jnp.zeros((half, LANES), F32)]
        pos = 0
        for jj, (r0, r1) in enumerate(spans):
            for r in range(r0, r1, half):
                piece = sums[pos:pos + half, :]
                pos += half
                rows = sub_r + r
                keep = (sub_c == lo + jj) & ((rows <= jj) if reverse else (rows >= jj))
                blk[r // half] = blk[r // half] + jnp.where(keep, piece, 0.0)
        diag_rows += blk
    a_mat = a_mat + jnp.concatenate(diag_rows, axis=0)
    o = o + jnp.dot(a_mat[:, :c].astype(BF16), v, preferred_element_type=F32)
    kd = (k * jnp.exp2(b_last - b)).astype(BF16)
    tn = (((0,), (0,)), ((), ()))
    new_state = state_t * jnp.exp2(b_last) + lax.dot_general(v, kd, tn, preferred_element_type=F32)
    return o, new_state


def _gla_scan_kernel(qf_ref, kf_ref, vf_ref, gf_ref, qb_ref, kb_ref, vb_ref, gb_ref, of_ref, ob_ref,
                     sf_ref, sb_ref, *, n_chunks):
    @pl.when(pl.program_id(2) == 0)
    def _():
        sf_ref[...] = jnp.zeros_like(sf_ref)
        sb_ref[...] = jnp.zeros_like(sb_ref)

    def step(q_ref, k_ref, v_ref, g_ref, o_ref, state_ref, cidx, reverse):
        rows = pl.ds(cidx * GLA_CHUNK, GLA_CHUNK)
        o, new_state = _gla_chunk(q_ref[rows, :].astype(F32), k_ref[rows, :].astype(F32), v_ref[rows, :],
                                  g_ref[rows, :], state_ref[...], reverse)
        o_ref[rows, :] = o.astype(o_ref.dtype)
        state_ref[...] = new_state

    for cidx in range(n_chunks):
        step(qf_ref, kf_ref, vf_ref, gf_ref, of_ref, sf_ref, cidx, False)
        step(qb_ref, kb_ref, vb_ref, gb_ref, ob_ref, sb_ref, n_chunks - 1 - cidx, True)


def _gla_scan(st, q, k, v, gf, gb):
    b = st.batch
    key_dim, value_dim = q.shape[1], v.shape[1]
    dk, dv = key_dim // GLA_HEADS, value_dim // GLA_HEADS
    t_rows = ROW_TILE
    nc, nl = st.ctx_len // t_rows, st.seq // t_rows

    def row_block(bi, s, reverse):
        if reverse:
            ctx = st.lat_tiles + bi * nc + (nc - 1 - s)
            lat = bi * nl + (nl - 1 - (s - nc))
        else:
            ctx = st.lat_tiles + bi * nc + s
            lat = bi * nl + (s - nc)
        return jnp.where(s < nc, ctx, lat)

    def spec(w, reverse):
        return pl.BlockSpec((t_rows, w), lambda bi, h, s: (row_block(bi, s, reverse), h))

    dir_specs = lambda reverse: [spec(dk, reverse), spec(dk, reverse), spec(dv, reverse), spec(dk, reverse)]
    out = jax.ShapeDtypeStruct((st.n, value_dim), BF16)
    return pl.pallas_call(
        functools.partial(_gla_scan_kernel, n_chunks=t_rows // GLA_CHUNK),
        grid=(b, GLA_HEADS, nc + nl),
        in_specs=dir_specs(False) + dir_specs(True),
        out_specs=[spec(dv, False), spec(dv, True)],
        out_shape=[out, out],
        scratch_shapes=[pltpu.VMEM((dv, dk), F32), pltpu.VMEM((dv, dk), F32)],
        compiler_params=_cparams(("parallel", "parallel", "arbitrary")),
        name="gla_scan_bidir",
    )(q, k, v, gf, q, k, v, gb)


def _route(logits, n_real):
    lane = lax.broadcasted_iota(jnp.int32, logits.shape, 1)
    neg = jnp.float32(-jnp.inf)
    big = jnp.int32(2 ** 30)
    is_grp = lane < MOE_GROUPS
    gl = jnp.where(is_grp, logits, neg)
    gmax = jnp.max(gl, axis=-1, keepdims=True)
    g_top = jnp.min(jnp.where(gl == gmax, lane, big), axis=-1, keepdims=True)
    p_grp = 1.0 / jnp.sum(jnp.exp(gl - gmax), axis=-1, keepdims=True)
    first = MOE_GROUPS + g_top * MOE_PER_GROUP
    in_grp = (lane >= first) & (lane < first + MOE_PER_GROUP)
    el = jnp.where(in_grp, logits, neg)
    emax = jnp.max(el, axis=-1, keepdims=True)
    pe = jnp.exp(el - emax)
    pe = pe / jnp.sum(pe, axis=-1, keepdims=True)
    v1 = jnp.max(pe, axis=-1, keepdims=True)
    i1 = jnp.min(jnp.where(in_grp & (pe == v1), lane, big), axis=-1, keepdims=True)
    rest = jnp.where(in_grp & (lane != i1), pe, -1.0)
    v2 = jnp.max(rest, axis=-1, keepdims=True)
    i2 = jnp.min(jnp.where(rest == v2, lane, big), axis=-1, keepdims=True)
    denom = v1 + v2
    w1 = p_grp * v1 / denom
    w2 = p_grp * v2 / denom
    e1 = (i1 - MOE_GROUPS).astype(F32)
    e2 = (i2 - MOE_GROUPS).astype(F32)
    del n_real
    return jnp.where(lane == 0, e1, jnp.where(lane == 1, e2, jnp.where(lane == 2, w1, jnp.where(lane == 3, w2, 0.0))))


def _pack_bf16_pairs(x):
    half = x.shape[1] // 2
    bits = lax.bitcast_convert_type(x.astype(BF16).astype(F32), jnp.uint32)
    return bits[:, half:] | (bits[:, :half] >> 16)


def _unpack_bf16_pairs(w):
    lo = lax.bitcast_convert_type(w << 16, F32)
    hi = lax.bitcast_convert_type(w & jnp.uint32(0xFFFF0000), F32)
    return jnp.concatenate([lo, hi], axis=1)


MIX_GROUPS = 2


def _row_groups(tm):
    rows = tm // MIX_GROUPS
    return [pl.ds(i * rows, rows) for i in range(MIX_GROUPS)]


def _post_mix(y, rows, x_ref, g1_ref, lng_ref, lnb_ref, sh2_ref, sc2_ref, wr_ref, rb_ref,
              x1_ref, h2_ref, route_ref, alpha):
    x1 = _layer_norm(alpha * x_ref[rows, :] + g1_ref[...] * y, lng_ref[...], lnb_ref[...])
    x1_ref[rows, :] = x1
    h2 = x1 * (1.0 + sc2_ref[...]) + sh2_ref[...]
    h2_ref[rows, :] = _pack_bf16_pairs(h2)
    n = y.shape[0]
    hi = h2.astype(BF16)
    lo = (h2 - hi.astype(F32)).astype(BF16)
    pr = jnp.dot(jnp.concatenate([hi, lo], axis=0), wr_ref[...], preferred_element_type=F32)
    logits = pr[:n, :LANES] + pr[:n, LANES:] + pr[n:, :LANES] + rb_ref[...]
    route_ref[rows, :] = _route(logits, None)


def _mla_out_kernel(ol_ref, oc_ref, w_o_ref, *rest, alpha, lat_tiles):
    is_lat = pl.program_id(0) < lat_tiles
    groups = _row_groups(ol_ref.shape[0])
    ys = []
    for rows in groups:
        o = jnp.where(is_lat, ol_ref[rows, :], oc_ref[rows, :])
        ys.append(jnp.dot(o, w_o_ref[...], preferred_element_type=F32))
    for rows, y in zip(groups, ys):
        _post_mix(y, rows, *rest, alpha=alpha)


def _gla_out_kernel(of_ref, ob_ref, r_ref, ng_ref, w_o_ref, *rest, alpha, dv):
    groups = _row_groups(of_ref.shape[0])
    ng = ng_ref[...]
    ys = []
    for rows in groups:
        o = of_ref[rows, :].astype(F32) + ob_ref[rows, :].astype(F32)
        r = r_ref[rows, :].astype(F32)
        gate = r * _sigmoid(r)
        parts = []
        for hd in range(GLA_HEADS):
            parts.append(_rms_norm(o[:, hd * dv:(hd + 1) * dv], ng) * gate[:, hd * dv:(hd + 1) * dv])
        u = jnp.concatenate(parts, axis=1).astype(BF16)
        ys.append(jnp.dot(u, w_o_ref[...], preferred_element_type=F32))
    for rows, y in zip(groups, ys):
        _post_mix(y, rows, *rest, alpha=alpha)


def _mix_out(st, rows, mixer_inputs, mixer_widths, kernel, x, mods, ln_g, ln_b, w_o, wr, rbias):
    d = st.d
    tm = MIX_TILE
    assert rows % tm == 0 and st.seq % tm == 0
    mixer_specs = [w if isinstance(w, pl.BlockSpec) else _row_spec(w, tm) for w in mixer_widths]
    mixer_specs += [_const_spec(a.shape) for a in mixer_inputs[len(mixer_widths):]]
    common_specs = [
        _row_spec(d, tm), _mod_spec(st, 2, tm), _const_spec((1, d)), _const_spec((1, d)),
        _mod_spec(st, 3, tm), _mod_spec(st, 4, tm), _const_spec(wr.shape), _const_spec((1, LANES)),
    ]
    return pl.pallas_call(
        kernel,
        grid=(rows // tm,),
        in_specs=mixer_specs + [_const_spec(w_o.shape)] + common_specs,
        out_specs=[_row_spec(d, tm), _row_spec(d // 2, tm), _row_spec(LANES, tm)],
        out_shape=[jax.ShapeDtypeStruct((rows, d), F32), jax.ShapeDtypeStruct((rows, d // 2), jnp.uint32),
                   jax.ShapeDtypeStruct((rows, LANES), F32)],
        compiler_params=_cparams(("parallel",)),
        name="mixer_out_ln_route",
    )(*mixer_inputs, w_o, x, mods, ln_g, ln_b, mods, mods, wr, rbias)


def _rank_kernel(route_ref, rank_ref, counts_ref, run_ref):
    t = pl.program_id(0)

    @pl.when(t == 0)
    def _():
        run_ref[...] = jnp.zeros_like(run_ref)

    route = route_ref[...]
    tm = route.shape[0]
    lane = lax.broadcasted_iota(jnp.int32, route.shape, 1)
    e0 = route[:, 0:1].astype(jnp.int32)
    e1 = route[:, 1:2].astype(jnp.int32)
    oh0 = (lane == e0).astype(F32)
    oh1 = (lane == e1).astype(F32)
    both = oh0 + oh1
    ri = lax.broadcasted_iota(jnp.int32, (tm, tm), 0)
    ci = lax.broadcasted_iota(jnp.int32, (tm, tm), 1)
    tri = (ci < ri).astype(BF16)
    before = jnp.dot(tri, both.astype(BF16), preferred_element_type=F32) + run_ref[0:1, :]
    r0 = jnp.sum(oh0 * before, axis=-1, keepdims=True)
    r1 = jnp.sum(oh1 * before, axis=-1, keepdims=True)
    rank_ref[...] = jnp.where(lane == 0, r0, jnp.where(lane == 1, r1, 0.0)).astype(jnp.int32)
    run_ref[...] = run_ref[...] + jnp.sum(both, axis=0, keepdims=True)
    counts_ref[...] = run_ref[...].astype(jnp.int32)


def _moe_ranks(route):
    n = route.shape[0]
    tm = RANK_TILE if n % RANK_TILE == 0 else ROW_TILE
    return pl.pallas_call(
        _rank_kernel,
        grid=(n // tm,),
        in_specs=[_row_spec(LANES, tm)],
        out_specs=[_row_spec(LANES, tm), _const_spec((8, LANES))],
        out_shape=[jax.ShapeDtypeStruct((n, LANES), jnp.int32), jax.ShapeDtypeStruct((8, LANES), jnp.int32)],
        scratch_shapes=[pltpu.VMEM((8, LANES), F32)],
        compiler_params=_cparams(("arbitrary",)),
        name="moe_ranks",
    )(route)


def _dispatch_kernel(slots_ref, pad_start_ref, pad_end_ref, h_ref, xs_ref, zero_ref, sem_ref, zsem_ref):
    t = pl.program_id(0)
    tm = h_ref.shape[0]

    @pl.when(t == 0)
    def _():
        zero_ref[...] = jnp.zeros_like(zero_ref)
        for e in range(MOE_EXPERTS):
            @pl.when(pad_end_ref[e] > pad_start_ref[e])
            def _():
                start = pl.multiple_of(pad_end_ref[e] - MOE_BLOCK, MOE_BLOCK)
                pltpu.make_async_copy(zero_ref, xs_ref.at[pl.ds(start, MOE_BLOCK)], zsem_ref).start()
        n_used = pad_end_ref[MOE_EXPERTS - 1] // MOE_BLOCK
        n_blocks = xs_ref.shape[0] // MOE_BLOCK

        def fill_tail(i, carry):
            start = pl.multiple_of(i * MOE_BLOCK, MOE_BLOCK)
            pltpu.make_async_copy(zero_ref, xs_ref.at[pl.ds(start, MOE_BLOCK)], zsem_ref).start()
            return carry

        def wait_tail(i, carry):
            pltpu.make_async_copy(zero_ref, xs_ref.at[pl.ds(0, MOE_BLOCK)], zsem_ref).wait()
            return carry

        lax.fori_loop(n_used, n_blocks, fill_tail, 0)
        for e in range(MOE_EXPERTS):
            @pl.when(pad_end_ref[e] > pad_start_ref[e])
            def _():
                pltpu.make_async_copy(zero_ref, xs_ref.at[pl.ds(0, MOE_BLOCK)], zsem_ref).wait()
        lax.fori_loop(n_used, n_blocks, wait_tail, 0)

    base = t * (2 * tm)

    def issue(r, carry):
        s0 = slots_ref[base + 2 * r]
        s1 = slots_ref[base + 2 * r + 1]
        pltpu.make_async_copy(h_ref.at[pl.ds(r, 1)], xs_ref.at[pl.ds(s0, 1)], sem_ref.at[0]).start()
        pltpu.make_async_copy(h_ref.at[pl.ds(r, 1)], xs_ref.at[pl.ds(s1, 1)], sem_ref.at[1]).start()
        return carry

    lax.fori_loop(0, tm, issue, 0, unroll=8)
    pltpu.make_async_copy(h_ref, xs_ref.at[pl.ds(0, tm)], sem_ref.at[0]).wait()
    pltpu.make_async_copy(h_ref, xs_ref.at[pl.ds(0, tm)], sem_ref.at[1]).wait()


def _moe_dispatch(h2, slots, pad_start, pad_end, n_rows):
    n, d = h2.shape
    tm = ROW_TILE
    grid_spec = pltpu.PrefetchScalarGridSpec(
        num_scalar_prefetch=3,
        grid=(n // tm,),
        in_specs=[pl.BlockSpec((tm, d), lambda t, *_: (t, 0))],
        out_specs=pl.BlockSpec(memory_space=pl.ANY),
        scratch_shapes=[pltpu.VMEM((MOE_BLOCK, d), h2.dtype), pltpu.SemaphoreType.DMA((2,)),
                        pltpu.SemaphoreType.DMA(())],
    )
    return pl.pallas_call(
        _dispatch_kernel,
        grid_spec=grid_spec,
        out_shape=jax.ShapeDtypeStruct((n_rows, d), h2.dtype),
        compiler_params=_cparams(("arbitrary",)),
        name="moe_dispatch",
    )(slots, pad_start, pad_end, h2)


def _experts_kernel(block_exp_ref, n_used_ref, xs_ref, w1_ref, w3_ref, w2_ref, o_ref, w1b_ref, w3b_ref, w2b_ref):
    i = pl.program_id(0)

    @pl.when(i < n_used_ref[0])
    def _():
        prev = block_exp_ref[jnp.maximum(i - 1, 0)]

        @pl.when((i == 0) | (block_exp_ref[i] != prev))
        def _():
            w1b_ref[...] = w1_ref[...].astype(BF16)
            w3b_ref[...] = w3_ref[...].astype(BF16)
            w2b_ref[...] = w2_ref[...].astype(BF16)

        groups = _row_groups(xs_ref.shape[0])
        ups = []
        for rows in groups:
            xb = _unpack_bf16_pairs(xs_ref[rows, :]).astype(BF16)
            ups.append((jnp.dot(xb, w1b_ref[...], preferred_element_type=F32),
                        jnp.dot(xb, w3b_ref[...], preferred_element_type=F32)))
        for rows, (a, g) in zip(groups, ups):
            hid = (a * _sigmoid(a) * g).astype(BF16)
            o_ref[rows, :] = _pack_bf16_pairs(jnp.dot(hid, w2b_ref[...], preferred_element_type=F32))

    @pl.when(i >= n_used_ref[0])
    def _():
        o_ref[...] = jnp.zeros_like(o_ref)


def _moe_experts(xs, block_exp, n_used, w1, w3, w2, layer):
    n_rows, dp = xs.shape
    n_blocks = n_rows // MOE_BLOCK
    d, f = w1.shape[-2], w1.shape[-1]
    assert dp * 2 == d

    def blk(i, be, nu):
        return jnp.minimum(i, nu[0] - 1)

    grid_spec = pltpu.PrefetchScalarGridSpec(
        num_scalar_prefetch=2,
        grid=(n_blocks,),
        in_specs=[
            pl.BlockSpec((MOE_BLOCK, dp), lambda i, be, nu: (blk(i, be, nu), 0)),
            pl.BlockSpec((None, None, d, f), lambda i, be, nu: (layer, be[blk(i, be, nu)], 0, 0)),
            pl.BlockSpec((None, None, d, f), lambda i, be, nu: (layer, be[blk(i, be, nu)], 0, 0)),
            pl.BlockSpec((None, None, f, d), lambda i, be, nu: (layer, be[blk(i, be, nu)], 0, 0)),
        ],
        out_specs=pl.BlockSpec((MOE_BLOCK, dp), lambda i, be, nu: (i, 0)),
        scratch_shapes=[pltpu.VMEM((d, f), BF16), pltpu.VMEM((d, f), BF16), pltpu.VMEM((f, d), BF16)],
    )
    return pl.pallas_call(
        _experts_kernel,
        grid_spec=grid_spec,
        out_shape=jax.ShapeDtypeStruct((n_rows, dp), xs.dtype),
        compiler_params=_cparams(("arbitrary",)),
        name="moe_experts",
    )(block_exp, n_used, xs, w1, w3, w2)


def _combine_kernel(slots_ref, ys_ref, x1_ref, route_ref, g2_ref, lng_ref, lnb_ref, out_ref, buf_ref, sem_ref,
                    *, alpha):
    t = pl.program_id(0)
    n_tiles = pl.num_programs(0)
    tm = x1_ref.shape[0]

    def gather_tile(tile, slot):
        base = tile * (2 * tm)

        def issue(r, carry):
            for k in range(2):
                src = ys_ref.at[pl.ds(slots_ref[base + 2 * r + k], 1)]
                pltpu.make_async_copy(src, buf_ref.at[slot, k, pl.ds(r, 1)], sem_ref.at[slot, k]).start()
            return carry

        lax.fori_loop(0, tm, issue, 0, unroll=8)

    @pl.when(t == 0)
    def _():
        gather_tile(0, 0)

    @pl.when(t + 1 < n_tiles)
    def _():
        gather_tile(t + 1, (t + 1) % 2)

    slot = t % 2
    for k in range(2):
        pltpu.make_async_copy(ys_ref.at[pl.ds(0, tm)], buf_ref.at[slot, k], sem_ref.at[slot, k]).wait()
    route = route_ref[...]
    y = (route[:, 2:3] * _unpack_bf16_pairs(buf_ref[slot, 0])
         + route[:, 3:4] * _unpack_bf16_pairs(buf_ref[slot, 1]))
    out_ref[...] = _layer_norm(alpha * x1_ref[...] + g2_ref[...] * y, lng_ref[...], lnb_ref[...])


def _moe_combine(st, ys, slots, x1, route, mods, ln_g, ln_b, alpha):
    n, d = x1.shape
    tm = ROW_TILE
    grid_spec = pltpu.PrefetchScalarGridSpec(
        num_scalar_prefetch=1,
        grid=(n // tm,),
        in_specs=[
            pl.BlockSpec(memory_space=pl.ANY),
            pl.BlockSpec((tm, d), lambda t, s: (t, 0)),
            pl.BlockSpec((tm, LANES), lambda t, s: (t, 0)),
            pl.BlockSpec((None, 1, d), lambda t, s: (st.mod_row(t), 0, 5)),
            pl.BlockSpec((1, d), lambda t, s: (0, 0)),
            pl.BlockSpec((1, d), lambda t, s: (0, 0)),
        ],
        out_specs=pl.BlockSpec((tm, d), lambda t, s: (t, 0)),
        scratch_shapes=[pltpu.VMEM((2, 2, tm, ys.shape[1]), ys.dtype), pltpu.SemaphoreType.DMA((2, 2))],
    )
    return pl.pallas_call(
        functools.partial(_combine_kernel, alpha=alpha),
        grid_spec=grid_spec,
        out_shape=jax.ShapeDtypeStruct((n, d), F32),
        compiler_params=_cparams(("arbitrary",)),
        name="moe_combine_ln",
    )(slots, ys, x1, route, mods, ln_g, ln_b)


def _hier_moe(st, h2, route, x1, mods, ln_g, ln_b, w1, w3, w2, layer, alpha):
    n = h2.shape[0]
    ranks, counts = _moe_ranks(route)
    counts = counts[0, :MOE_EXPERTS]
    padded = (counts + MOE_BLOCK - 1) // MOE_BLOCK * MOE_BLOCK
    pad_end = jnp.cumsum(padded).astype(jnp.int32)
    pad_start = pad_end - padded
    expert = route[:, :2].astype(jnp.int32)
    eids = jnp.arange(MOE_EXPERTS, dtype=jnp.int32)
    start_of = jnp.sum(jnp.where(expert[:, :, None] == eids, pad_start, 0), axis=-1)
    slots = (start_of + ranks[:, :2]).reshape(2 * n)
    n_blocks = -(-(2 * n) // MOE_BLOCK) + MOE_EXPERTS
    block_pos = jnp.arange(n_blocks, dtype=jnp.int32) * MOE_BLOCK
    block_exp = jnp.minimum(jnp.sum((pad_end[None, :] <= block_pos[:, None]).astype(jnp.int32), axis=1),
                            MOE_EXPERTS - 1)
    n_used = (pad_end[-1:] // MOE_BLOCK).astype(jnp.int32)
    xs = _moe_dispatch(h2, slots, pad_start, pad_end, n_blocks * MOE_BLOCK)
    ys = _moe_experts(xs, block_exp, n_used, w1, w3, w2, layer)
    return _moe_combine(st, ys, slots, x1, route, mods, ln_g, ln_b, alpha)


def _swap_halves(w):
    half = w.shape[-1] // 2
    return jnp.concatenate([w[..., half:], w[..., :half]], axis=-1)


def _mla_weights(w_in, w_uq, w_ukv, w_o):
    qr, kvr = MLA_Q_RANK, MLA_KV_RANK
    w_pe = w_in[:, qr + kvr:]
    w_in_ext = jnp.concatenate([w_in, _swap_halves(w_pe)], axis=1).astype(BF16)
    uq = w_uq.reshape(qr, MLA_HEADS, MLA_NOPE + MLA_ROPE)
    q_nope = uq[:, :, :MLA_NOPE].reshape(qr, MLA_HEADS * MLA_NOPE)
    q_pe = uq[:, :, MLA_NOPE:]
    q_pe2 = jnp.concatenate([q_pe, _swap_halves(q_pe)], axis=-1).reshape(qr, MLA_HEADS * 2 * MLA_ROPE)
    w_q = jnp.concatenate([q_nope, q_pe2], axis=1).astype(BF16)
    ukv = w_ukv.reshape(kvr, MLA_HEADS, MLA_NOPE + MLA_V)
    w_kv = jnp.concatenate([ukv[:, :, :MLA_NOPE].reshape(kvr, -1), ukv[:, :, MLA_NOPE:].reshape(kvr, -1)],
                           axis=1).astype(BF16)
    return w_in_ext, w_q, w_kv, w_o.astype(BF16)


def _gla_weights(w_in, gate_a, gate_b, gate_bias, w_o):
    d = w_in.shape[0]
    key_dim = gate_b.shape[-1]
    rank = gate_a.shape[-1]
    ga = jnp.concatenate([gate_a[0], gate_a[1], jnp.zeros((d, LANES - 2 * rank), F32)], axis=1)
    w_qkg = jnp.concatenate([w_in[:, :2 * key_dim], ga], axis=1).astype(BF16)
    w_vr = w_in[:, 2 * key_dim:].astype(BF16)
    gb = jnp.zeros((LANES, 2 * key_dim), F32)
    gb = gb.at[:rank, :key_dim].set(gate_b[0]).at[rank:2 * rank, key_dim:].set(gate_b[1])
    bias = jnp.concatenate([gate_bias[0], gate_bias[1]])[None, :]
    return w_qkg, gb.astype(BF16), bias, w_vr, w_o.astype(BF16)


def _rope_table(seq):
    n_rows = seq // GRID_W
    row = jnp.repeat(jnp.arange(n_rows, dtype=F32), GRID_W)
    col = jnp.tile(jnp.arange(GRID_W, dtype=F32), n_rows)
    n_freq = MLA_ROPE // 4
    inv_freq = jnp.power(ROPE_THETA, -jnp.arange(n_freq, dtype=F32) / n_freq)
    ang = jnp.concatenate([row[:, None] * inv_freq, col[:, None] * inv_freq], axis=-1)
    cos, sin = jnp.cos(ang), jnp.sin(ang)
    lat = jnp.concatenate([cos, cos, -sin, sin], axis=-1)
    ident = jnp.concatenate([jnp.ones((ROW_TILE, MLA_ROPE), F32), jnp.zeros((ROW_TILE, MLA_ROPE), F32)], axis=-1)
    return jnp.concatenate([lat, ident], axis=0)


def kernel(x, c, ctx, c_ctx, w_mod, b_mod, ln1_g, ln1_b, ln2_g, ln2_b, mla_w_in, mla_q_norm, mla_w_uq, mla_kv_norm, mla_w_ukv, mla_w_o, gla_w_in, gla_gate_a, gla_gate_b, gla_gate_bias, gla_norm, gla_w_o, moe_w_grp, moe_b_grp, moe_w_exp, moe_b_exp, moe_w1, moe_w3, moe_w2):
    batch, seq, d = x.shape
    ctx_len = ctx.shape[1]
    depth = w_mod.shape[0]
    assert batch + 1 <= 8
    alpha = (2.0 * depth) ** 0.25
    full = _Stream(batch, seq, ctx_len, d)

    cond = jnp.concatenate([c, c_ctx[None, :], jnp.zeros((8 - batch - 1, d), F32)], axis=0)
    mods_all = _adaln_tables(cond, w_mod, b_mod)
    cs_tab = _rope_table(seq)
    xs = jnp.concatenate([x.reshape(batch * seq, d), ctx.reshape(batch * ctx_len, d)], axis=0)

    for i in range(depth):
        last = i == depth - 1
        mods = mods_all[i].reshape(8, 1, N_MOD * d)
        j = i // 2
        n_rows = full.n_lat if last else full.n
        wr = jnp.concatenate([moe_w_grp[i], moe_w_exp[i],
                              jnp.zeros((d, LANES - MOE_GROUPS - MOE_EXPERTS), F32)], axis=1)
        wr_hi = wr.astype(BF16)
        wr_lo = (wr - wr_hi.astype(F32)).astype(BF16)
        wr_split = jnp.concatenate([wr_hi, wr_lo], axis=1)
        rbias = jnp.concatenate([moe_b_grp[i], moe_b_exp[i],
                                 jnp.zeros((LANES - MOE_GROUPS - MOE_EXPERTS,), F32)])[None, :]
        ln1 = (ln1_g[i][None, :], ln1_b[i][None, :])
        if i % 2 == 0:
            w_in_ext, w_q, w_kv, w_o = _mla_weights(mla_w_in[j], mla_w_uq[j], mla_w_ukv[j], mla_w_o[j])
            q, k, v = _mla_proj(full, xs, mods, cs_tab, w_in_ext, mla_q_norm[j][None, :],
                                mla_kv_norm[j][None, :], w_q, w_kv)
            o_lat = _attention(full, q, k, v, latent=True)
            o_ctx = o_lat if last else _attention(full, q, k, v, latent=False)
            lat_tiles = full.n_lat // MIX_TILE
            ow = o_lat.shape[1]
            o_specs = [pl.BlockSpec((MIX_TILE, ow), lambda t: (jnp.minimum(t, lat_tiles - 1), 0)),
                       pl.BlockSpec((MIX_TILE, ow), lambda t: (jnp.maximum(t - lat_tiles, 0), 0))]
            x1, h2, route = _mix_out(full, n_rows, [o_lat, o_ctx], o_specs,
                                     functools.partial(_mla_out_kernel, alpha=alpha, lat_tiles=lat_tiles),
                                     xs, mods, ln1[0], ln1[1], w_o, wr_split, rbias)
        else:
            w_qkg, gb_ext, gbias, w_vr, w_o = _gla_weights(gla_w_in[j], gla_gate_a[j], gla_gate_b[j],
                                                          gla_gate_bias[j], gla_w_o[j])
            q, k, v, r, gf, gb = _gla_proj(full, xs, mods, w_qkg, gb_ext, gbias, w_vr)
            o_f, o_b = _gla_scan(full, q, k, v, gf, gb)
            vd = v.shape[1]
            x1, h2, route = _mix_out(full, n_rows, [o_f, o_b, r, gla_norm[j][None, :]], [vd, vd, vd],
                                     functools.partial(_gla_out_kernel, alpha=alpha, dv=vd // GLA_HEADS),
                                     xs, mods, ln1[0], ln1[1], w_o, wr_split, rbias)
        xs = _hier_moe(full, h2, route, x1, mods, ln2_g[i][None, :], ln2_b[i][None, :],
                       moe_w1, moe_w3, moe_w2, i, alpha)
    return xs[:batch * seq].reshape(batch, seq, d)
```

```python
import functools
import math

import jax
import jax.numpy as jnp
import numpy as np
from jax import lax
from jax.experimental import pallas as pl
from jax.experimental.pallas import tpu as pltpu

F32 = jnp.float32
BF16 = jnp.bfloat16

GRID_W = 64
N_MOD = 6
NORM_EPS = 1e-6
MLA_HEADS = 16
MLA_Q_RANK = 768
MLA_KV_RANK = 256
MLA_NOPE = 128
MLA_ROPE = 64
MLA_V = 128
ROPE_THETA = 10000.0
GLA_HEADS = 4
GLA_GATE_RANK = 16
GLA_TAU = 16.0
GLA_CHUNK = 64
GLA_SUB = 16
MOE_GROUPS = 4
MOE_PER_GROUP = 8
MOE_EXPERTS = MOE_GROUPS * MOE_PER_GROUP
MOE_BLOCK = 256

LANES = 128
VMEM_LIMIT = 56 * 1024 * 1024

ROW_TILE = 256
RANK_TILE = 1024
MIX_TILE = 512
ATTN_TQ = 2048
ATTN_TK = 1024
ATTN_STREAMS = 4
ATTN_SKEW = 2


def _cparams(sem):
    return pltpu.CompilerParams(dimension_semantics=sem, vmem_limit_bytes=VMEM_LIMIT)


def _sigmoid(x):
    return 1.0 / (1.0 + jnp.exp(-x))


def _layer_norm(z, g, b):
    mu = jnp.mean(z, axis=-1, keepdims=True)
    zc = z - mu
    var = jnp.mean(zc * zc, axis=-1, keepdims=True)
    return zc * lax.rsqrt(var + NORM_EPS) * g + b


def _rms_norm(z, g):
    return z * lax.rsqrt(jnp.mean(z * z, axis=-1, keepdims=True) + NORM_EPS) * g


def _mods_kernel(cond_ref, w_ref, b_ref, o_ref):
    c = cond_ref[...]
    s = c * _sigmoid(c)
    s1 = s.astype(BF16).astype(F32)
    s2 = (s - s1).astype(BF16).astype(F32)
    s3 = s - s1 - s2
    w = w_ref[...]
    wh = w.astype(BF16)
    wl = (w - wh.astype(F32)).astype(BF16)
    rows = s.shape[0]
    top = jnp.dot(jnp.concatenate([s1, s2, s3, jnp.zeros_like(s)], axis=0).astype(BF16), wh,
                  preferred_element_type=F32)
    bot = jnp.dot(jnp.concatenate([s1, s2], axis=0).astype(BF16), wl, preferred_element_type=F32)
    o_ref[...] = (top[:rows] + top[rows:2 * rows] + top[2 * rows:3 * rows]
                  + bot[:rows] + bot[rows:]) + b_ref[...]


def _adaln_tables(cond, w_mod, b_mod):
    depth, d, n6 = w_mod.shape
    tn = 2048
    return pl.pallas_call(
        _mods_kernel,
        grid=(depth, n6 // tn),
        in_specs=[
            pl.BlockSpec((8, d), lambda l, j: (0, 0)),
            pl.BlockSpec((None, d, tn), lambda l, j: (l, 0, j)),
            pl.BlockSpec((None, 1, tn), lambda l, j: (l, 0, j)),
        ],
        out_specs=pl.BlockSpec((None, 8, tn), lambda l, j: (l, 0, j)),
        out_shape=jax.ShapeDtypeStruct((depth, 8, n6), F32),
        compiler_params=_cparams(("parallel", "parallel")),
        name="adaln_tables",
    )(cond, w_mod, b_mod.reshape(depth, 1, n6))


class _Stream:
    def __init__(self, batch, seq, ctx_len, d):
        self.batch, self.seq, self.ctx_len, self.d = batch, seq, ctx_len, d
        self.n_lat = batch * seq
        self.n_ctx = batch * ctx_len
        self.n = self.n_lat + self.n_ctx
        assert seq % ROW_TILE == 0 and ctx_len % ROW_TILE == 0
        self.lat_tiles = self.n_lat // ROW_TILE
        self.tiles = self.n // ROW_TILE
        self.tiles_per_batch = seq // ROW_TILE

    def mod_row(self, t, tm=ROW_TILE):
        return jnp.where(t < self.n_lat // tm, t // (self.seq // tm), self.batch)


def _mod_spec(st, chunk, tm=ROW_TILE):
    d = st.d
    return pl.BlockSpec((None, 1, d), lambda t: (st.mod_row(t, tm), 0, chunk))


def _row_spec(width, tm=ROW_TILE):
    return pl.BlockSpec((tm, width), lambda t: (t, 0))


def _const_spec(shape):
    nd = len(shape)
    return pl.BlockSpec(shape, lambda t: (0,) * nd, pipeline_mode=pl.Buffered(1))


def _half_sum(x):
    return x + pltpu.roll(x, LANES // 2, axis=1)


def _mla_proj_kernel(x_ref, sh_ref, sc_ref, cs_ref, w_in_ref, qg_ref, kvg_ref, w_q_ref, w_kv_ref,
                     q_ref, k_ref, v_ref, *, scale):
    h = x_ref[...] * (1.0 + sc_ref[...]) + sh_ref[...]
    lat = jnp.dot(h.astype(BF16), w_in_ref[...], preferred_element_type=F32)
    cq = _rms_norm(lat[:, :MLA_Q_RANK], qg_ref[...]).astype(BF16)
    ckv = _rms_norm(lat[:, MLA_Q_RANK:MLA_Q_RANK + MLA_KV_RANK], kvg_ref[...]).astype(BF16)
    cs = cs_ref[...]
    lane = lax.broadcasted_iota(jnp.int32, cs.shape, 1)
    kpe = _half_sum(lat[:, MLA_Q_RANK + MLA_KV_RANK:] * cs)
    kpe = jnp.where(lane < MLA_ROPE, kpe, 0.0).astype(BF16)
    qall = jnp.dot(cq, w_q_ref[...], preferred_element_type=F32)
    kvall = jnp.dot(ckv, w_kv_ref[...], preferred_element_type=F32)
    hn = MLA_HEADS * MLA_NOPE
    for hd in range(MLA_HEADS):
        lo = hd * LANES
        q_ref[hd, :, :MLA_NOPE] = (qall[:, lo:lo + LANES] * scale).astype(BF16)
        q_ref[hd, :, MLA_NOPE:] = (_half_sum(qall[:, hn + lo:hn + lo + LANES] * cs) * scale).astype(BF16)
        k_ref[hd, :, :MLA_NOPE] = kvall[:, lo:lo + LANES].astype(BF16)
        k_ref[hd, :, MLA_NOPE:] = kpe
        v_ref[hd] = jnp.transpose(kvall[:, hn + lo:hn + lo + LANES]).astype(BF16)


def _mla_proj(st, x, mods, cs_tab, w_in_ext, q_norm, kv_norm, w_q, w_kv):
    n, d = st.n, st.d
    tm = ROW_TILE
    lat_w = w_in_ext.shape[1]
    qk_w = MLA_NOPE + LANES
    scale = (MLA_NOPE + MLA_ROPE) ** -0.5 * math.log2(math.e)
    seq_tiles = st.tiles_per_batch

    def cs_map(t):
        return (jnp.where(t < st.lat_tiles, t % seq_tiles, seq_tiles), 0)

    head_spec = lambda w: pl.BlockSpec((MLA_HEADS, tm, w), lambda t: (0, t, 0))
    return pl.pallas_call(
        functools.partial(_mla_proj_kernel, scale=scale),
        grid=(st.tiles,),
        in_specs=[
            _row_spec(d), _mod_spec(st, 0), _mod_spec(st, 1),
            pl.BlockSpec((tm, LANES), cs_map),
            _const_spec((d, lat_w)), _const_spec((1, MLA_Q_RANK)), _const_spec((1, MLA_KV_RANK)),
            _const_spec(w_q.shape), _const_spec(w_kv.shape),
        ],
        out_specs=[head_spec(qk_w), head_spec(qk_w),
                   pl.BlockSpec((MLA_HEADS, MLA_V, tm), lambda t: (0, 0, t))],
        out_shape=[
            jax.ShapeDtypeStruct((MLA_HEADS, n, qk_w), BF16),
            jax.ShapeDtypeStruct((MLA_HEADS, n, qk_w), BF16),
            jax.ShapeDtypeStruct((MLA_HEADS, MLA_V, n), BF16),
        ],
        compiler_params=_cparams(("parallel",)),
        name="mla_proj",
    )(x, mods, mods, cs_tab, w_in_ext, q_norm, kv_norm, w_q, w_kv)


def _attn_kernel(q_ref, kc_ref, vct_ref, *rest, n_chunks):
    if n_chunks:
        kl_ref, vlt_ref, o_ref = rest
    else:
        (o_ref,) = rest
    nt = (((1,), (1,)), ((), ()))
    tq = q_ref.shape[0]
    n_streams = ATTN_STREAMS if n_chunks else 1
    rows_q = tq // n_streams

    def chunk_rows(j):
        return pl.ds(j * ATTN_TK, ATTN_TK)

    class Stream:
        def __init__(self, idx):
            self.qrows = pl.ds(idx * rows_q, rows_q)
            self.q = q_ref[self.qrows, :]
            self.stage = 0

        def scores(self, k):
            s = lax.dot_general(k, self.q, nt, preferred_element_type=F32)
            return s, jnp.max(s, axis=0, keepdims=True)

        def advance(self):
            j = self.stage - 1
            if self.stage == 0:
                s, m_new = self.scores(kc_ref[...])
                self.nxt = self.scores(kl_ref[chunk_rows(0), :]) if n_chunks else None
                p = jnp.exp2(s - m_new)
                self.l = jnp.sum(p, axis=0, keepdims=True)
                self.acc = jnp.dot(vct_ref[...], p.astype(BF16), preferred_element_type=F32)
            else:
                s, m_blk = self.nxt
                if j + 1 < n_chunks:
                    self.nxt = self.scores(kl_ref[chunk_rows(j + 1), :])
                m_new = jnp.maximum(self.m, m_blk)
                alpha = jnp.exp2(self.m - m_new)
                p = jnp.exp2(s - m_new)
                self.l = alpha * self.l + jnp.sum(p, axis=0, keepdims=True)
                self.acc = alpha * self.acc + jnp.dot(vlt_ref[:, chunk_rows(j)], p.astype(BF16),
                                                      preferred_element_type=F32)
            self.m = m_new
            self.stage += 1
            if self.stage == n_chunks + 1:
                o_ref[self.qrows, :] = jnp.transpose(self.acc / self.l).astype(o_ref.dtype)

    streams = [Stream(i) for i in range(n_streams)]
    n_stages = n_chunks + 1
    for t in range(n_stages + ATTN_SKEW * (n_streams - 1)):
        for i, st in enumerate(streams):
            if 0 <= t - i * ATTN_SKEW < n_stages:
                st.advance()


def _attention(st, q, k, vt, latent):
    b, seq, lc = st.batch, st.seq, st.ctx_len
    qk_w = q.shape[-1]
    tq = min(ATTN_TQ, seq if latent else lc)
    assert seq % tq == 0 and seq % ATTN_TK == 0 and lc % min(ATTN_TQ, lc) == 0
    ctx_blk0 = st.n_lat // lc
    if latent:
        nq = seq // tq
        q_map = lambda bi, h, qi: (h, bi * nq + qi, 0)
        o_map = lambda bi, h, qi: (bi * nq + qi, h)
        rows = st.n_lat
        n_chunks = seq // ATTN_TK
    else:
        nq = lc // tq
        q_map = lambda bi, h, qi: (h, st.n_lat // tq + bi * nq + qi, 0)
        o_map = lambda bi, h, qi: (bi * nq + qi, h)
        rows = st.n_ctx
        n_chunks = 0
    kc_map = lambda bi, h, qi: (h, ctx_blk0 + bi, 0)
    kl_map = lambda bi, h, qi: (h, bi, 0)
    vc_map = lambda bi, h, qi: (h, 0, ctx_blk0 + bi)
    vl_map = lambda bi, h, qi: (h, 0, bi)
    in_specs = [
        pl.BlockSpec((None, tq, qk_w), q_map),
        pl.BlockSpec((None, lc, qk_w), kc_map),
        pl.BlockSpec((None, MLA_V, lc), vc_map),
    ]
    args = [q, k, vt]
    if latent:
        in_specs += [pl.BlockSpec((None, seq, qk_w), kl_map), pl.BlockSpec((None, MLA_V, seq), vl_map)]
        args += [k, vt]
    return pl.pallas_call(
        functools.partial(_attn_kernel, n_chunks=n_chunks),
        grid=(b, MLA_HEADS, nq),
        in_specs=in_specs,
        out_specs=pl.BlockSpec((tq, MLA_V), o_map),
        out_shape=jax.ShapeDtypeStruct((rows, MLA_HEADS * MLA_V), BF16),
        compiler_params=_cparams(("parallel", "parallel", "arbitrary")),
        name="mla_attention_latent" if latent else "mla_attention_context",
    )(*args)


def _gla_qkg_kernel(x_ref, sh_ref, sc_ref, w_ref, gb_ref, bias_ref, q_ref, k_ref, gf_ref, gb_out_ref,
                    *, key_dim, q_scale):
    h = (x_ref[...] * (1.0 + sc_ref[...]) + sh_ref[...]).astype(BF16)
    y = jnp.dot(h, w_ref[...], preferred_element_type=F32)
    q_ref[...] = (y[:, :key_dim] * q_scale).astype(BF16)
    k_ref[...] = y[:, key_dim:2 * key_dim].astype(BF16)
    z = jnp.dot(y[:, 2 * key_dim:].astype(BF16), gb_ref[...], preferred_element_type=F32) + bias_ref[...]
    g = (jnp.minimum(z, 0.0) - jnp.log(1.0 + jnp.exp(-jnp.abs(z)))) * (math.log2(math.e) / GLA_TAU)
    gf_ref[...] = g[:, :key_dim]
    gb_out_ref[...] = g[:, key_dim:]


def _gla_vr_kernel(x_ref, sh_ref, sc_ref, w_ref, v_ref, r_ref, *, value_dim):
    h = (x_ref[...] * (1.0 + sc_ref[...]) + sh_ref[...]).astype(BF16)
    y = jnp.dot(h, w_ref[...], preferred_element_type=F32)
    v_ref[...] = y[:, :value_dim].astype(BF16)
    r_ref[...] = y[:, value_dim:].astype(BF16)


def _gla_proj(st, x, mods, w_qkg, gate_b_ext, gate_bias, w_vr):
    n, d = st.n, st.d
    key_dim = (w_qkg.shape[1] - LANES) // 2
    value_dim = w_vr.shape[1] // 2
    dk = key_dim // GLA_HEADS
    q, k, gf, gb = pl.pallas_call(
        functools.partial(_gla_qkg_kernel, key_dim=key_dim, q_scale=dk ** -0.5),
        grid=(st.tiles,),
        in_specs=[_row_spec(d), _mod_spec(st, 0), _mod_spec(st, 1), _const_spec(w_qkg.shape),
                  _const_spec(gate_b_ext.shape), _const_spec(gate_bias.shape)],
        out_specs=[_row_spec(key_dim)] * 4,
        out_shape=[jax.ShapeDtypeStruct((n, key_dim), BF16), jax.ShapeDtypeStruct((n, key_dim), BF16),
                   jax.ShapeDtypeStruct((n, key_dim), F32), jax.ShapeDtypeStruct((n, key_dim), F32)],
        compiler_params=_cparams(("parallel",)),
        name="gla_proj_qkg",
    )(x, mods, mods, w_qkg, gate_b_ext, gate_bias)
    v, r = pl.pallas_call(
        functools.partial(_gla_vr_kernel, value_dim=value_dim),
        grid=(st.tiles,),
        in_specs=[_row_spec(d), _mod_spec(st, 0), _mod_spec(st, 1), _const_spec(w_vr.shape)],
        out_specs=[_row_spec(value_dim)] * 2,
        out_shape=[jax.ShapeDtypeStruct((n, value_dim), BF16), jax.ShapeDtypeStruct((n, value_dim), BF16)],
        compiler_params=_cparams(("parallel",)),
        name="gla_proj_vr",
    )(x, mods, mods, w_vr)
    return q, k, v, r, gf, gb


def _cumsum_rows(x, reverse):
    n = x.shape[0]
    row = lax.broadcasted_iota(jnp.int32, x.shape, 0)
    s = 1
    while s < n:
        if reverse:
            x = x + jnp.where(row < n - s, pltpu.roll(x, n - s, axis=0), 0.0)
        else:
            x = x + jnp.where(row >= s, pltpu.roll(x, s, axis=0), 0.0)
        s *= 2
    return x


def _gla_chunk(q, k, v, g, state_t, reverse):
    c, dk = q.shape
    nsub = c // GLA_SUB
    half = GLA_SUB // 2
    b = _cumsum_rows(g, reverse)
    b_last = jnp.sum(g, axis=0, keepdims=True)
    nt = (((1,), (1,)), ((), ()))
    qe = (q * jnp.exp2(b)).astype(BF16)
    o = lax.dot_general(qe, state_t.astype(BF16), nt, preferred_element_type=F32)

    q_parts, k_parts = [], []
    for src in (range(1, nsub) if reverse else range(nsub - 1)):
        lo = src * GLA_SUB
        if reverse:
            ref = b[lo:lo + 1, :]
            qd = (q[:lo] * jnp.exp2(b[:lo] - ref)).astype(BF16)
            q_parts.append(jnp.concatenate([qd, jnp.zeros((c - lo, dk), BF16)], axis=0))
        else:
            ref = b[lo + GLA_SUB - 1:lo + GLA_SUB, :]
            qd = (q[lo + GLA_SUB:] * jnp.exp2(b[lo + GLA_SUB:] - ref)).astype(BF16)
            q_parts.append(jnp.concatenate([jnp.zeros((lo + GLA_SUB, dk), BF16), qd], axis=0))
        kd = (k[lo:lo + GLA_SUB] * jnp.exp2(ref - b[lo:lo + GLA_SUB])).astype(BF16)
        pieces = [kd]
        if lo:
            pieces.insert(0, jnp.zeros((lo, dk), BF16))
        if c - lo - GLA_SUB:
            pieces.append(jnp.zeros((c - lo - GLA_SUB, dk), BF16))
        k_parts.append(jnp.concatenate(pieces, axis=0))
    off = lax.dot_general(jnp.concatenate(q_parts, axis=1), jnp.concatenate(k_parts, axis=1), nt,
                          preferred_element_type=F32)
    a_mat = jnp.concatenate([off, jnp.zeros((c, LANES - c), F32)], axis=1)

    ones = jnp.ones((dk, LANES), BF16)
    sub_r = lax.broadcasted_iota(jnp.int32, (half, LANES), 0)
    sub_c = lax.broadcasted_iota(jnp.int32, (half, LANES), 1)
    diag_rows = []
    for a in range(nsub):
        lo = a * GLA_SUB
        terms, spans = [], []
        for jj in range(GLA_SUB):
            if reverse:
                r0, r1 = 0, (half if jj < half else GLA_SUB)
            else:
                r0, r1 = (0 if jj < half else half), GLA_SUB
            kj = k[lo + jj:lo + jj + 1, :]
            bj = b[lo + jj:lo + jj + 1, :]
            terms.append(q[lo + r0:lo + r1] * kj * jnp.exp2(b[lo + r0:lo + r1] - bj))
            spans.append((r0, r1))
        sums = jnp.dot(jnp.concatenate(terms, axis=0).astype(BF16), ones, preferred_element_type=F32)
        blk = [jnp.zeros((half, LANES), F32), jnp.zeros((half, LANES), F32)]
        pos = 0
        for jj, (r0, r1) in enumerate(spans):
            for r in range(r0, r1, half):
                piece = sums[pos:pos + half, :]
                pos += half
                rows = sub_r + r
                keep = (sub_c == lo + jj) & ((rows <= jj) if reverse else (rows >= jj))
                blk[r // half] = blk[r // half] + jnp.where(keep, piece, 0.0)
        diag_rows += blk
    a_mat = a_mat + jnp.concatenate(diag_rows, axis=0)
    o = o + jnp.dot(a_mat[:, :c].astype(BF16), v, preferred_element_type=F32)
    kd = (k * jnp.exp2(b_last - b)).astype(BF16)
    tn = (((0,), (0,)), ((), ()))
    new_state = state_t * jnp.exp2(b_last) + lax.dot_general(v, kd, tn, preferred_element_type=F32)
    return o, new_state


def _gla_scan_kernel(qf_ref, kf_ref, vf_ref, gf_ref, qb_ref, kb_ref, vb_ref, gb_ref, of_ref, ob_ref,
                     sf_ref, sb_ref, *, n_chunks):
    @pl.when(pl.program_id(2) == 0)
    def _():
        sf_ref[...] = jnp.zeros_like(sf_ref)
        sb_ref[...] = jnp.zeros_like(sb_ref)

    def step(q_ref, k_ref, v_ref, g_ref, o_ref, state_ref, cidx, reverse):
        rows = pl.ds(cidx * GLA_CHUNK, GLA_CHUNK)
        o, new_state = _gla_chunk(q_ref[rows, :].astype(F32), k_ref[rows, :].astype(F32), v_ref[rows, :],
                                  g_ref[rows, :], state_ref[...], reverse)
        o_ref[rows, :] = o.astype(o_ref.dtype)
        state_ref[...] = new_state

    for cidx in range(n_chunks):
        step(qf_ref, kf_ref, vf_ref, gf_ref, of_ref, sf_ref, cidx, False)
        step(qb_ref, kb_ref, vb_ref, gb_ref, ob_ref, sb_ref, n_chunks - 1 - cidx, True)


def _gla_scan(st, q, k, v, gf, gb):
    b = st.batch
    key_dim, value_dim = q.shape[1], v.shape[1]
    dk, dv = key_dim // GLA_HEADS, value_dim // GLA_HEADS
    t_rows = ROW_TILE
    nc, nl = st.ctx_len // t_rows, st.seq // t_rows

    def row_block(bi, s, reverse):
        if reverse:
            ctx = st.lat_tiles + bi * nc + (nc - 1 - s)
            lat = bi * nl + (nl - 1 - (s - nc))
        else:
            ctx = st.lat_tiles + bi * nc + s
            lat = bi * nl + (s - nc)
        return jnp.where(s < nc, ctx, lat)

    def spec(w, reverse):
        return pl.BlockSpec((t_rows, w), lambda bi, h, s: (row_block(bi, s, reverse), h))

    dir_specs = lambda reverse: [spec(dk, reverse), spec(dk, reverse), spec(dv, reverse), spec(dk, reverse)]
    out = jax.ShapeDtypeStruct((st.n, value_dim), BF16)
    return pl.pallas_call(
        functools.partial(_gla_scan_kernel, n_chunks=t_rows // GLA_CHUNK),
        grid=(b, GLA_HEADS, nc + nl),
        in_specs=dir_specs(False) + dir_specs(True),
        out_specs=[spec(dv, False), spec(dv, True)],
        out_shape=[out, out],
        scratch_shapes=[pltpu.VMEM((dv, dk), F32), pltpu.VMEM((dv, dk), F32)],
        compiler_params=_cparams(("parallel", "parallel", "arbitrary")),
        name="gla_scan_bidir",
    )(q, k, v, gf, q, k, v, gb)


def _route(logits, n_real):
    lane = lax.broadcasted_iota(jnp.int32, logits.shape, 1)
    neg = jnp.float32(-jnp.inf)
    big = jnp.int32(2 ** 30)
    is_grp = lane < MOE_GROUPS
    gl = jnp.where(is_grp, logits, neg)
    gmax = jnp.max(gl, axis=-1, keepdims=True)
    g_top = jnp.min(jnp.where(gl == gmax, lane, big), axis=-1, keepdims=True)
    p_grp = 1.0 / jnp.sum(jnp.exp(gl - gmax), axis=-1, keepdims=True)
    first = MOE_GROUPS + g_top * MOE_PER_GROUP
    in_grp = (lane >= first) & (lane < first + MOE_PER_GROUP)
    el = jnp.where(in_grp, logits, neg)
    emax = jnp.max(el, axis=-1, keepdims=True)
    pe = jnp.exp(el - emax)
    pe = pe / jnp.sum(pe, axis=-1, keepdims=True)
    v1 = jnp.max(pe, axis=-1, keepdims=True)
    i1 = jnp.min(jnp.where(in_grp & (pe == v1), lane, big), axis=-1, keepdims=True)
    rest = jnp.where(in_grp & (lane != i1), pe, -1.0)
    v2 = jnp.max(rest, axis=-1, keepdims=True)
    i2 = jnp.min(jnp.where(rest == v2, lane, big), axis=-1, keepdims=True)
    denom = v1 + v2
    w1 = p_grp * v1 / denom
    w2 = p_grp * v2 / denom
    e1 = (i1 - MOE_GROUPS).astype(F32)
    e2 = (i2 - MOE_GROUPS).astype(F32)
    del n_real
    return jnp.where(lane == 0, e1, jnp.where(lane == 1, e2, jnp.where(lane == 2, w1, jnp.where(lane == 3, w2, 0.0))))


def _pack_bf16_pairs(x):
    half = x.shape[1] // 2
    bits = lax.bitcast_convert_type(x.astype(BF16).astype(F32), jnp.uint32)
    return bits[:, half:] | (bits[:, :half] >> 16)


def _unpack_bf16_pairs(w):
    lo = lax.bitcast_convert_type(w << 16, F32)
    hi = lax.bitcast_convert_type(w & jnp.uint32(0xFFFF0000), F32)
    return jnp.concatenate([lo, hi], axis=1)


MIX_GROUPS = 2


def _row_groups(tm):
    rows = tm // MIX_GROUPS
    return [pl.ds(i * rows, rows) for i in range(MIX_GROUPS)]


def _post_mix(y, rows, x_ref, g1_ref, lng_ref, lnb_ref, sh2_ref, sc2_ref, wr_ref, rb_ref,
              x1_ref, h2_ref, route_ref, alpha):
    x1 = _layer_norm(alpha * x_ref[rows, :] + g1_ref[...] * y, lng_ref[...], lnb_ref[...])
    x1_ref[rows, :] = x1
    h2 = x1 * (1.0 + sc2_ref[...]) + sh2_ref[...]
    h2_ref[rows, :] = _pack_bf16_pairs(h2)
    n = y.shape[0]
    hi = h2.astype(BF16)
    lo = (h2 - hi.astype(F32)).astype(BF16)
    pr = jnp.dot(jnp.concatenate([hi, lo], axis=0), wr_ref[...], preferred_element_type=F32)
    logits = pr[:n, :LANES] + pr[:n, LANES:] + pr[n:, :LANES] + rb_ref[...]
    route_ref[rows, :] = _route(logits, None)


def _mla_out_kernel(ol_ref, oc_ref, w_o_ref, *rest, alpha, lat_tiles):
    is_lat = pl.program_id(0) < lat_tiles
    groups = _row_groups(ol_ref.shape[0])
    ys = []
    for rows in groups:
        o = jnp.where(is_lat, ol_ref[rows, :], oc_ref[rows, :])
        ys.append(jnp.dot(o, w_o_ref[...], preferred_element_type=F32))
    for rows, y in zip(groups, ys):
        _post_mix(y, rows, *rest, alpha=alpha)


def _gla_out_kernel(of_ref, ob_ref, r_ref, ng_ref, w_o_ref, *rest, alpha, dv):
    groups = _row_groups(of_ref.shape[0])
    ng = ng_ref[...]
    ys = []
    for rows in groups:
        o = of_ref[rows, :].astype(F32) + ob_ref[rows, :].astype(F32)
        r = r_ref[rows, :].astype(F32)
        gate = r * _sigmoid(r)
        parts = []
        for hd in range(GLA_HEADS):
            parts.append(_rms_norm(o[:, hd * dv:(hd + 1) * dv], ng) * gate[:, hd * dv:(hd + 1) * dv])
        u = jnp.concatenate(parts, axis=1).astype(BF16)
        ys.append(jnp.dot(u, w_o_ref[...], preferred_element_type=F32))
    for rows, y in zip(groups, ys):
        _post_mix(y, rows, *rest, alpha=alpha)


def _mix_out(st, rows, mixer_inputs, mixer_widths, kernel, x, mods, ln_g, ln_b, w_o, wr, rbias):
    d = st.d
    tm = MIX_TILE
    assert rows % tm == 0 and st.seq % tm == 0
    mixer_specs = [w if isinstance(w, pl.BlockSpec) else _row_spec(w, tm) for w in mixer_widths]
    mixer_specs += [_const_spec(a.shape) for a in mixer_inputs[len(mixer_widths):]]
    common_specs = [
        _row_spec(d, tm), _mod_spec(st, 2, tm), _const_spec((1, d)), _const_spec((1, d)),
        _mod_spec(st, 3, tm), _mod_spec(st, 4, tm), _const_spec(wr.shape), _const_spec((1, LANES)),
    ]
    return pl.pallas_call(
        kernel,
        grid=(rows // tm,),
        in_specs=mixer_specs + [_const_spec(w_o.shape)] + common_specs,
        out_specs=[_row_spec(d, tm), _row_spec(d // 2, tm), _row_spec(LANES, tm)],
        out_shape=[jax.ShapeDtypeStruct((rows, d), F32), jax.ShapeDtypeStruct((rows, d // 2), jnp.uint32),
                   jax.ShapeDtypeStruct((rows, LANES), F32)],
        compiler_params=_cparams(("parallel",)),
        name="mixer_out_ln_route",
    )(*mixer_inputs, w_o, x, mods, ln_g, ln_b, mods, mods, wr, rbias)


def _rank_kernel(route_ref, rank_ref, counts_ref, run_ref):
    t = pl.program_id(0)

    @pl.when(t == 0)
    def _():
        run_ref[...] = jnp.zeros_like(run_ref)

    route = route_ref[...]
    tm = route.shape[0]
    lane = lax.broadcasted_iota(jnp.int32, route.shape, 1)
    e0 = route[:, 0:1].astype(jnp.int32)
    e1 = route[:, 1:2].astype(jnp.int32)
    oh0 = (lane == e0).astype(F32)
    oh1 = (lane == e1).astype(F32)
    both = oh0 + oh1
    ri = lax.broadcasted_iota(jnp.int32, (tm, tm), 0)
    ci = lax.broadcasted_iota(jnp.int32, (tm, tm), 1)
    tri = (ci < ri).astype(BF16)
    before = jnp.dot(tri, both.astype(BF16), preferred_element_type=F32) + run_ref[0:1, :]
    r0 = jnp.sum(oh0 * before, axis=-1, keepdims=True)
    r1 = jnp.sum(oh1 * before, axis=-1, keepdims=True)
    rank_ref[...] = jnp.where(lane == 0, r0, jnp.where(lane == 1, r1, 0.0)).astype(jnp.int32)
    run_ref[...] = run_ref[...] + jnp.sum(both, axis=0, keepdims=True)
    counts_ref[...] = run_ref[...].astype(jnp.int32)


def _moe_ranks(route):
    n = route.shape[0]
    tm = RANK_TILE if n % RANK_TILE == 0 else ROW_TILE
    return pl.pallas_call(
        _rank_kernel,
        grid=(n // tm,),
        in_specs=[_row_spec(LANES, tm)],
        out_specs=[_row_spec(LANES, tm), _const_spec((8, LANES))],
        out_shape=[jax.ShapeDtypeStruct((n, LANES), jnp.int32), jax.ShapeDtypeStruct((8, LANES), jnp.int32)],
        scratch_shapes=[pltpu.VMEM((8, LANES), F32)],
        compiler_params=_cparams(("arbitrary",)),
        name="moe_ranks",
    )(route)


def _dispatch_kernel(slots_ref, pad_start_ref, pad_end_ref, h_ref, xs_ref, zero_ref, sem_ref, zsem_ref):
    t = pl.program_id(0)
    tm = h_ref.shape[0]

    @pl.when(t == 0)
    def _():
        zero_ref[...] = jnp.zeros_like(zero_ref)
        for e in range(MOE_EXPERTS):
            @pl.when(pad_end_ref[e] > pad_start_ref[e])
            def _():
                start = pl.multiple_of(pad_end_ref[e] - MOE_BLOCK, MOE_BLOCK)
                pltpu.make_async_copy(zero_ref, xs_ref.at[pl.ds(start, MOE_BLOCK)], zsem_ref).start()
        n_used = pad_end_ref[MOE_EXPERTS - 1] // MOE_BLOCK
        n_blocks = xs_ref.shape[0] // MOE_BLOCK

        def fill_tail(i, carry):
            start = pl.multiple_of(i * MOE_BLOCK, MOE_BLOCK)
            pltpu.make_async_copy(zero_ref, xs_ref.at[pl.ds(start, MOE_BLOCK)], zsem_ref).start()
            return carry

        def wait_tail(i, carry):
            pltpu.make_async_copy(zero_ref, xs_ref.at[pl.ds(0, MOE_BLOCK)], zsem_ref).wait()
            return carry

        lax.fori_loop(n_used, n_blocks, fill_tail, 0)
        for e in range(MOE_EXPERTS):
            @pl.when(pad_end_ref[e] > pad_start_ref[e])
            def _():
                pltpu.make_async_copy(zero_ref, xs_ref.at[pl.ds(0, MOE_BLOCK)], zsem_ref).wait()
        lax.fori_loop(n_used, n_blocks, wait_tail, 0)

    base = t * (2 * tm)

    def issue(r, carry):
        s0 = slots_ref[base + 2 * r]
        s1 = slots_ref[base + 2 * r + 1]
        pltpu.make_async_copy(h_ref.at[pl.ds(r, 1)], xs_ref.at[pl.ds(s0, 1)], sem_ref.at[0]).start()
        pltpu.make_async_copy(h_ref.at[pl.ds(r, 1)], xs_ref.at[pl.ds(s1, 1)], sem_ref.at[1]).start()
        return carry

    lax.fori_loop(0, tm, issue, 0, unroll=8)
    pltpu.make_async_copy(h_ref, xs_ref.at[pl.ds(0, tm)], sem_ref.at[0]).wait()
    pltpu.make_async_copy(h_ref, xs_ref.at[pl.ds(0, tm)], sem_ref.at[1]).wait()


def _moe_dispatch(h2, slots, pad_start, pad_end, n_rows):
    n, d = h2.shape
    tm = ROW_TILE
    grid_spec = pltpu.PrefetchScalarGridSpec(
        num_scalar_prefetch=3,
        grid=(n // tm,),
        in_specs=[pl.BlockSpec((tm, d), lambda t, *_: (t, 0))],
        out_specs=pl.BlockSpec(memory_space=pl.ANY),
        scratch_shapes=[pltpu.VMEM((MOE_BLOCK, d), h2.dtype), pltpu.SemaphoreType.DMA((2,)),
                        pltpu.SemaphoreType.DMA(())],
    )
    return pl.pallas_call(
        _dispatch_kernel,
        grid_spec=grid_spec,
        out_shape=jax.ShapeDtypeStruct((n_rows, d), h2.dtype),
        compiler_params=_cparams(("arbitrary",)),
        name="moe_dispatch",
    )(slots, pad_start, pad_end, h2)


def _experts_kernel(block_exp_ref, n_used_ref, xs_ref, w1_ref, w3_ref, w2_ref, o_ref, w1b_ref, w3b_ref, w2b_ref):
    i = pl.program_id(0)

    @pl.when(i < n_used_ref[0])
    def _():
        prev = block_exp_ref[jnp.maximum(i - 1, 0)]

        @pl.when((i == 0) | (block_exp_ref[i] != prev))
        def _():
            w1b_ref[...] = w1_ref[...].astype(BF16)
            w3b_ref[...] = w3_ref[...].astype(BF16)
            w2b_ref[...] = w2_ref[...].astype(BF16)

        groups = _row_groups(xs_ref.shape[0])
        ups = []
        for rows in groups:
            xb = _unpack_bf16_pairs(xs_ref[rows, :]).astype(BF16)
            ups.append((jnp.dot(xb, w1b_ref[...], preferred_element_type=F32),
                        jnp.dot(xb, w3b_ref[...], preferred_element_type=F32)))
        for rows, (a, g) in zip(groups, ups):
            hid = (a * _sigmoid(a) * g).astype(BF16)
            o_ref[rows, :] = _pack_bf16_pairs(jnp.dot(hid, w2b_ref[...], preferred_element_type=F32))

    @pl.when(i >= n_used_ref[0])
    def _():
        o_ref[...] = jnp.zeros_like(o_ref)


def _moe_experts(xs, block_exp, n_used, w1, w3, w2, layer):
    n_rows, dp = xs.shape
    n_blocks = n_rows // MOE_BLOCK
    d, f = w1.shape[-2], w1.shape[-1]
    assert dp * 2 == d

    def blk(i, be, nu):
        return jnp.minimum(i, nu[0] - 1)

    grid_spec = pltpu.PrefetchScalarGridSpec(
        num_scalar_prefetch=2,
        grid=(n_blocks,),
        in_specs=[
            pl.BlockSpec((MOE_BLOCK, dp), lambda i, be, nu: (blk(i, be, nu), 0)),
            pl.BlockSpec((None, None, d, f), lambda i, be, nu: (layer, be[blk(i, be, nu)], 0, 0)),
            pl.BlockSpec((None, None, d, f), lambda i, be, nu: (layer, be[blk(i, be, nu)], 0, 0)),
            pl.BlockSpec((None, None, f, d), lambda i, be, nu: (layer, be[blk(i, be, nu)], 0, 0)),
        ],
        out_specs=pl.BlockSpec((MOE_BLOCK, dp), lambda i, be, nu: (i, 0)),
        scratch_shapes=[pltpu.VMEM((d, f), BF16), pltpu.VMEM((d, f), BF16), pltpu.VMEM((f, d), BF16)],
    )
    return pl.pallas_call(
        _experts_kernel,
        grid_spec=grid_spec,
        out_shape=jax.ShapeDtypeStruct((n_rows, dp), xs.dtype),
        compiler_params=_cparams(("arbitrary",)),
        name="moe_experts",
    )(block_exp, n_used, xs, w1, w3, w2)


def _combine_kernel(slots_ref, ys_ref, x1_ref, route_ref, g2_ref, lng_ref, lnb_ref, out_ref, buf_ref, sem_ref,
                    *, alpha):
    t = pl.program_id(0)
    n_tiles = pl.num_programs(0)
    tm = x1_ref.shape[0]

    def gather_tile(tile, slot):
        base = tile * (2 * tm)

        def issue(r, carry):
            for k in range(2):
                src = ys_ref.at[pl.ds(slots_ref[base + 2 * r + k], 1)]
                pltpu.make_async_copy(src, buf_ref.at[slot, k, pl.ds(r, 1)], sem_ref.at[slot, k]).start()
            return carry

        lax.fori_loop(0, tm, issue, 0, unroll=8)

    @pl.when(t == 0)
    def _():
        gather_tile(0, 0)

    @pl.when(t + 1 < n_tiles)
    def _():
        gather_tile(t + 1, (t + 1) % 2)

    slot = t % 2
    for k in range(2):
        pltpu.make_async_copy(ys_ref.at[pl.ds(0, tm)], buf_ref.at[slot, k], sem_ref.at[slot, k]).wait()
    route = route_ref[...]
    y = (route[:, 2:3] * _unpack_bf16_pairs(buf_ref[slot, 0])
         + route[:, 3:4] * _unpack_bf16_pairs(buf_ref[slot, 1]))
    out_ref[...] = _layer_norm(alpha * x1_ref[...] + g2_ref[...] * y, lng_ref[...], lnb_ref[...])


def _moe_combine(st, ys, slots, x1, route, mods, ln_g, ln_b, alpha):
    n, d = x1.shape
    tm = ROW_TILE
    grid_spec = pltpu.PrefetchScalarGridSpec(
        num_scalar_prefetch=1,
        grid=(n // tm,),
        in_specs=[
            pl.BlockSpec(memory_space=pl.ANY),
            pl.BlockSpec((tm, d), lambda t, s: (t, 0)),
            pl.BlockSpec((tm, LANES), lambda t, s: (t, 0)),
            pl.BlockSpec((None, 1, d), lambda t, s: (st.mod_row(t), 0, 5)),
            pl.BlockSpec((1, d), lambda t, s: (0, 0)),
            pl.BlockSpec((1, d), lambda t, s: (0, 0)),
        ],
        out_specs=pl.BlockSpec((tm, d), lambda t, s: (t, 0)),
        scratch_shapes=[pltpu.VMEM((2, 2, tm, ys.shape[1]), ys.dtype), pltpu.SemaphoreType.DMA((2, 2))],
    )
    return pl.pallas_call(
        functools.partial(_combine_kernel, alpha=alpha),
        grid_spec=grid_spec,
        out_shape=jax.ShapeDtypeStruct((n, d), F32),
        compiler_params=_cparams(("arbitrary",)),
        name="moe_combine_ln",
    )(slots, ys, x1, route, mods, ln_g, ln_b)


def _hier_moe(st, h2, route, x1, mods, ln_g, ln_b, w1, w3, w2, layer, alpha):
    n = h2.shape[0]
    ranks, counts = _moe_ranks(route)
    counts = counts[0, :MOE_EXPERTS]
    padded = (counts + MOE_BLOCK - 1) // MOE_BLOCK * MOE_BLOCK
    pad_end = jnp.cumsum(padded).astype(jnp.int32)
    pad_start = pad_end - padded
    expert = route[:, :2].astype(jnp.int32)
    eids = jnp.arange(MOE_EXPERTS, dtype=jnp.int32)
    start_of = jnp.sum(jnp.where(expert[:, :, None] == eids, pad_start, 0), axis=-1)
    slots = (start_of + ranks[:, :2]).reshape(2 * n)
    n_blocks = -(-(2 * n) // MOE_BLOCK) + MOE_EXPERTS
    block_pos = jnp.arange(n_blocks, dtype=jnp.int32) * MOE_BLOCK
    block_exp = jnp.minimum(jnp.sum((pad_end[None, :] <= block_pos[:, None]).astype(jnp.int32), axis=1),
                            MOE_EXPERTS - 1)
    n_used = (pad_end[-1:] // MOE_BLOCK).astype(jnp.int32)
    xs = _moe_dispatch(h2, slots, pad_start, pad_end, n_blocks * MOE_BLOCK)
    ys = _moe_experts(xs, block_exp, n_used, w1, w3, w2, layer)
    return _moe_combine(st, ys, slots, x1, route, mods, ln_g, ln_b, alpha)


def _swap_halves(w):
    half = w.shape[-1] // 2
    return jnp.concatenate([w[..., half:], w[..., :half]], axis=-1)


def _mla_weights(w_in, w_uq, w_ukv, w_o):
    qr, kvr = MLA_Q_RANK, MLA_KV_RANK
    w_pe = w_in[:, qr + kvr:]
    w_in_ext = jnp.concatenate([w_in, _swap_halves(w_pe)], axis=1).astype(BF16)
    uq = w_uq.reshape(qr, MLA_HEADS, MLA_NOPE + MLA_ROPE)
    q_nope = uq[:, :, :MLA_NOPE].reshape(qr, MLA_HEADS * MLA_NOPE)
    q_pe = uq[:, :, MLA_NOPE:]
    q_pe2 = jnp.concatenate([q_pe, _swap_halves(q_pe)], axis=-1).reshape(qr, MLA_HEADS * 2 * MLA_ROPE)
    w_q = jnp.concatenate([q_nope, q_pe2], axis=1).astype(BF16)
    ukv = w_ukv.reshape(kvr, MLA_HEADS, MLA_NOPE + MLA_V)
    w_kv = jnp.concatenate([ukv[:, :, :MLA_NOPE].reshape(kvr, -1), ukv[:, :, MLA_NOPE:].reshape(kvr, -1)],
                           axis=1).astype(BF16)
    return w_in_ext, w_q, w_kv, w_o.astype(BF16)


def _gla_weights(w_in, gate_a, gate_b, gate_bias, w_o):
    d = w_in.shape[0]
    key_dim = gate_b.shape[-1]
    rank = gate_a.shape[-1]
    ga = jnp.concatenate([gate_a[0], gate_a[1], jnp.zeros((d, LANES - 2 * rank), F32)], axis=1)
    w_qkg = jnp.concatenate([w_in[:, :2 * key_dim], ga], axis=1).astype(BF16)
    w_vr = w_in[:, 2 * key_dim:].astype(BF16)
    gb = jnp.zeros((LANES, 2 * key_dim), F32)
    gb = gb.at[:rank, :key_dim].set(gate_b[0]).at[rank:2 * rank, key_dim:].set(gate_b[1])
    bias = jnp.concatenate([gate_bias[0], gate_bias[1]])[None, :]
    return w_qkg, gb.astype(BF16), bias, w_vr, w_o.astype(BF16)


def _rope_table(seq):
    n_rows = seq // GRID_W
    row = jnp.repeat(jnp.arange(n_rows, dtype=F32), GRID_W)
    col = jnp.tile(jnp.arange(GRID_W, dtype=F32), n_rows)
    n_freq = MLA_ROPE // 4
    inv_freq = jnp.power(ROPE_THETA, -jnp.arange(n_freq, dtype=F32) / n_freq)
    ang = jnp.concatenate([row[:, None] * inv_freq, col[:, None] * inv_freq], axis=-1)
    cos, sin = jnp.cos(ang), jnp.sin(ang)
    lat = jnp.concatenate([cos, cos, -sin, sin], axis=-1)
    ident = jnp.concatenate([jnp.ones((ROW_TILE, MLA_ROPE), F32), jnp.zeros((ROW_TILE, MLA_ROPE), F32)], axis=-1)
    return jnp.concatenate([lat, ident], axis=0)


def kernel(x, c, ctx, c_ctx, w_mod, b_mod, ln1_g, ln1_b, ln2_g, ln2_b, mla_w_in, mla_q_norm, mla_w_uq, mla_kv_norm, mla_w_ukv, mla_w_o, gla_w_in, gla_gate_a, gla_gate_b, gla_gate_bias, gla_norm, gla_w_o, moe_w_grp, moe_b_grp, moe_w_exp, moe_b_exp, moe_w1, moe_w3, moe_w2):
    batch, seq, d = x.shape
    ctx_len = ctx.shape[1]
    depth = w_mod.shape[0]
    assert batch + 1 <= 8
    alpha = (2.0 * depth) ** 0.25
    full = _Stream(batch, seq, ctx_len, d)

    cond = jnp.concatenate([c, c_ctx[None, :], jnp.zeros((8 - batch - 1, d), F32)], axis=0)
    mods_all = _adaln_tables(cond, w_mod, b_mod)
    cs_tab = _rope_table(seq)
    xs = jnp.concatenate([x.reshape(batch * seq, d), ctx.reshape(batch * ctx_len, d)], axis=0)

    for i in range(depth):
        last = i == depth - 1
        mods = mods_all[i].reshape(8, 1, N_MOD * d)
        j = i // 2
        n_rows = full.n_lat if last else full.n
        wr = jnp.concatenate([moe_w_grp[i], moe_w_exp[i],
                              jnp.zeros((d, LANES - MOE_GROUPS - MOE_EXPERTS), F32)], axis=1)
        wr_hi = wr.astype(BF16)
        wr_lo = (wr - wr_hi.astype(F32)).astype(BF16)
        wr_split = jnp.concatenate([wr_hi, wr_lo], axis=1)
        rbias = jnp.concatenate([moe_b_grp[i], moe_b_exp[i],
                                 jnp.zeros((LANES - MOE_GROUPS - MOE_EXPERTS,), F32)])[None, :]
        ln1 = (ln1_g[i][None, :], ln1_b[i][None, :])
        if i % 2 == 0:
            w_in_ext, w_q, w_kv, w_o = _mla_weights(mla_w_in[j], mla_w_uq[j], mla_w_ukv[j], mla_w_o[j])
            q, k, v = _mla_proj(full, xs, mods, cs_tab, w_in_ext, mla_q_norm[j][None, :],
                                mla_kv_norm[j][None, :], w_q, w_kv)
            o_lat = _attention(full, q, k, v, latent=True)
            o_ctx = o_lat if last else _attention(full, q, k, v, latent=False)
            lat_tiles = full.n_lat // MIX_TILE
            ow = o_lat.shape[1]
            o_specs = [pl.BlockSpec((MIX_TILE, ow), lambda t: (jnp.minimum(t, lat_tiles - 1), 0)),
                       pl.BlockSpec((MIX_TILE, ow), lambda t: (jnp.maximum(t - lat_tiles, 0), 0))]
            x1, h2, route = _mix_out(full, n_rows, [o_lat, o_ctx], o_specs,
                                     functools.partial(_mla_out_kernel, alpha=alpha, lat_tiles=lat_tiles),
                                     xs, mods, ln1[0], ln1[1], w_o, wr_split, rbias)
        else:
            w_qkg, gb_ext, gbias, w_vr, w_o = _gla_weights(gla_w_in[j], gla_gate_a[j], gla_gate_b[j],
                                                          gla_gate_bias[j], gla_w_o[j])
            q, k, v, r, gf, gb = _gla_proj(full, xs, mods, w_qkg, gb_ext, gbias, w_vr)
            o_f, o_b = _gla_scan(full, q, k, v, gf, gb)
            vd = v.shape[1]
            x1, h2, route = _mix_out(full, n_rows, [o_f, o_b, r, gla_norm[j][None, :]], [vd, vd, vd],
                                     functools.partial(_gla_out_kernel, alpha=alpha, dv=vd // GLA_HEADS),
                                     xs, mods, ln1[0], ln1[1], w_o, wr_split, rbias)
        xs = _hier_moe(full, h2, route, x1, mods, ln2_g[i][None, :], ln2_b[i][None, :],
                       moe_w1, moe_w3, moe_w2, i, alpha)
    return xs[:batch * seq].reshape(batch, seq, d)
```

```python
import functools
import math

import jax
import jax.numpy as jnp
import numpy as np
from jax import lax
from jax.experimental import pallas as pl
from jax.experimental.pallas import tpu as pltpu

F32 = jnp.float32
BF16 = jnp.bfloat16

GRID_W = 64
N_MOD = 6
NORM_EPS = 1e-6
MLA_HEADS = 16
MLA_Q_RANK = 768
MLA_KV_RANK = 256
MLA_NOPE = 128
MLA_ROPE = 64
MLA_V = 128
ROPE_THETA = 10000.0
GLA_HEADS = 4
GLA_GATE_RANK = 16
GLA_TAU = 16.0
GLA_CHUNK = 64
GLA_SUB = 16
MOE_GROUPS = 4
MOE_PER_GROUP = 8
MOE_EXPERTS = MOE_GROUPS * MOE_PER_GROUP
MOE_BLOCK = 256

LANES = 128
VMEM_LIMIT = 56 * 1024 * 1024

ROW_TILE = 256
RANK_TILE = 1024
MIX_TILE = 512
ATTN_TQ = 2048
ATTN_TK = 1024
ATTN_STREAMS = 4
ATTN_SKEW = 2


def _cparams(sem):
    return pltpu.CompilerParams(dimension_semantics=sem, vmem_limit_bytes=VMEM_LIMIT)


def _sigmoid(x):
    return 1.0 / (1.0 + jnp.exp(-x))


def _layer_norm(z, g, b):
    mu = jnp.mean(z, axis=-1, keepdims=True)
    zc = z - mu
    var = jnp.mean(zc * zc, axis=-1, keepdims=True)
    return zc * lax.rsqrt(var + NORM_EPS) * g + b


def _rms_norm(z, g):
    return z * lax.rsqrt(jnp.mean(z * z, axis=-1, keepdims=True) + NORM_EPS) * g


def _mods_kernel(cond_ref, w_ref, b_ref, o_ref):
    c = cond_ref[...]
    s = c * _sigmoid(c)
    s1 = s.astype(BF16).astype(F32)
    s2 = (s - s1).astype(BF16).astype(F32)
    s3 = s - s1 - s2
    w = w_ref[...]
    wh = w.astype(BF16)
    wl = (w - wh.astype(F32)).astype(BF16)
    rows = s.shape[0]
    top = jnp.dot(jnp.concatenate([s1, s2, s3, jnp.zeros_like(s)], axis=0).astype(BF16), wh,
                  preferred_element_type=F32)
    bot = jnp.dot(jnp.concatenate([s1, s2], axis=0).astype(BF16), wl, preferred_element_type=F32)
    o_ref[...] = (top[:rows] + top[rows:2 * rows] + top[2 * rows:3 * rows]
                  + bot[:rows] + bot[rows:]) + b_ref[...]


def _adaln_tables(cond, w_mod, b_mod):
    depth, d, n6 = w_mod.shape
    tn = 2048
    return pl.pallas_call(
        _mods_kernel,
        grid=(depth, n6 // tn),
        in_specs=[
            pl.BlockSpec((8, d), lambda l, j: (0, 0)),
            pl.BlockSpec((None, d, tn), lambda l, j: (l, 0, j)),
            pl.BlockSpec((None, 1, tn), lambda l, j: (l, 0, j)),
        ],
        out_specs=pl.BlockSpec((None, 8, tn), lambda l, j: (l, 0, j)),
        out_shape=jax.ShapeDtypeStruct((depth, 8, n6), F32),
        compiler_params=_cparams(("parallel", "parallel")),
        name="adaln_tables",
    )(cond, w_mod, b_mod.reshape(depth, 1, n6))


class _Stream:
    def __init__(self, batch, seq, ctx_len, d):
        self.batch, self.seq, self.ctx_len, self.d = batch, seq, ctx_len, d
        self.n_lat = batch * seq
        self.n_ctx = batch * ctx_len
        self.n = self.n_lat + self.n_ctx
        assert seq % ROW_TILE == 0 and ctx_len % ROW_TILE == 0
        self.lat_tiles = self.n_lat // ROW_TILE
        self.tiles = self.n // ROW_TILE
        self.tiles_per_batch = seq // ROW_TILE

    def mod_row(self, t, tm=ROW_TILE):
        return jnp.where(t < self.n_lat // tm, t // (self.seq // tm), self.batch)


def _mod_spec(st, chunk, tm=ROW_TILE):
    d = st.d
    return pl.BlockSpec((None, 1, d), lambda t: (st.mod_row(t, tm), 0, chunk))


def _row_spec(width, tm=ROW_TILE):
    return pl.BlockSpec((tm, width), lambda t: (t, 0))


def _const_spec(shape):
    nd = len(shape)
    return pl.BlockSpec(shape, lambda t: (0,) * nd, pipeline_mode=pl.Buffered(1))


def _half_sum(x):
    return x + pltpu.roll(x, LANES // 2, axis=1)


def _mla_proj_kernel(x_ref, sh_ref, sc_ref, cs_ref, w_in_ref, qg_ref, kvg_ref, w_q_ref, w_kv_ref,
                     q_ref, k_ref, v_ref, *, scale):
    h = x_ref[...] * (1.0 + sc_ref[...]) + sh_ref[...]
    lat = jnp.dot(h.astype(BF16), w_in_ref[...], preferred_element_type=F32)
    cq = _rms_norm(lat[:, :MLA_Q_RANK], qg_ref[...]).astype(BF16)
    ckv = _rms_norm(lat[:, MLA_Q_RANK:MLA_Q_RANK + MLA_KV_RANK], kvg_ref[...]).astype(BF16)
    cs = cs_ref[...]
    lane = lax.broadcasted_iota(jnp.int32, cs.shape, 1)
    kpe = _half_sum(lat[:, MLA_Q_RANK + MLA_KV_RANK:] * cs)
    kpe = jnp.where(lane < MLA_ROPE, kpe, 0.0).astype(BF16)
    qall = jnp.dot(cq, w_q_ref[...], preferred_element_type=F32)
    kvall = jnp.dot(ckv, w_kv_ref[...], preferred_element_type=F32)
    hn = MLA_HEADS * MLA_NOPE
    for hd in range(MLA_HEADS):
        lo = hd * LANES
        q_ref[hd, :, :MLA_NOPE] = (qall[:, lo:lo + LANES] * scale).astype(BF16)
        q_ref[hd, :, MLA_NOPE:] = (_half_sum(qall[:, hn + lo:hn + lo + LANES] * cs) * scale).astype(BF16)
        k_ref[hd, :, :MLA_NOPE] = kvall[:, lo:lo + LANES].astype(BF16)
        k_ref[hd, :, MLA_NOPE:] = kpe
        v_ref[hd] = jnp.transpose(kvall[:, hn + lo:hn + lo + LANES]).astype(BF16)


def _mla_proj(st, x, mods, cs_tab, w_in_ext, q_norm, kv_norm, w_q, w_kv):
    n, d = st.n, st.d
    tm = ROW_TILE
    lat_w = w_in_ext.shape[1]
    qk_w = MLA_NOPE + LANES
    scale = (MLA_NOPE + MLA_ROPE) ** -0.5 * math.log2(math.e)
    seq_tiles = st.tiles_per_batch

    def cs_map(t):
        return (jnp.where(t < st.lat_tiles, t % seq_tiles, seq_tiles), 0)

    head_spec = lambda w: pl.BlockSpec((MLA_HEADS, tm, w), lambda t: (0, t, 0))
    return pl.pallas_call(
        functools.partial(_mla_proj_kernel, scale=scale),
        grid=(st.tiles,),
        in_specs=[
            _row_spec(d), _mod_spec(st, 0), _mod_spec(st, 1),
            pl.BlockSpec((tm, LANES), cs_map),
            _const_spec((d, lat_w)), _const_spec((1, MLA_Q_RANK)), _const_spec((1, MLA_KV_RANK)),
            _const_spec(w_q.shape), _const_spec(w_kv.shape),
        ],
        out_specs=[head_spec(qk_w), head_spec(qk_w),
                   pl.BlockSpec((MLA_HEADS, MLA_V, tm), lambda t: (0, 0, t))],
        out_shape=[
            jax.ShapeDtypeStruct((MLA_HEADS, n, qk_w), BF16),
            jax.ShapeDtypeStruct((MLA_HEADS, n, qk_w), BF16),
            jax.ShapeDtypeStruct((MLA_HEADS, MLA_V, n), BF16),
        ],
        compiler_params=_cparams(("parallel",)),
        name="mla_proj",
    )(x, mods, mods, cs_tab, w_in_ext, q_norm, kv_norm, w_q, w_kv)


def _attn_kernel(q_ref, kc_ref, vct_ref, *rest, n_chunks):
    if n_chunks:
        kl_ref, vlt_ref, o_ref = rest
    else:
        (o_ref,) = rest
    nt = (((1,), (1,)), ((), ()))
    tq = q_ref.shape[0]
    n_streams = ATTN_STREAMS if n_chunks else 1
    rows_q = tq // n_streams

    def chunk_rows(j):
        return pl.ds(j * ATTN_TK, ATTN_TK)

    class Stream:
        def __init__(self, idx):
            self.qrows = pl.ds(idx * rows_q, rows_q)
            self.q = q_ref[self.qrows, :]
            self.stage = 0

        def scores(self, k):
            s = lax.dot_general(k, self.q, nt, preferred_element_type=F32)
            return s, jnp.max(s, axis=0, keepdims=True)

        def advance(self):
            j = self.stage - 1
            if self.stage == 0:
                s, m_new = self.scores(kc_ref[...])
                self.nxt = self.scores(kl_ref[chunk_rows(0), :]) if n_chunks else None
                p = jnp.exp2(s - m_new)
                self.l = jnp.sum(p, axis=0, keepdims=True)
                self.acc = jnp.dot(vct_ref[...], p.astype(BF16), preferred_element_type=F32)
            else:
                s, m_blk = self.nxt
                if j + 1 < n_chunks:
                    self.nxt = self.scores(kl_ref[chunk_rows(j + 1), :])
                m_new = jnp.maximum(self.m, m_blk)
                alpha = jnp.exp2(self.m - m_new)
                p = jnp.exp2(s - m_new)
                self.l = alpha * self.l + jnp.sum(p, axis=0, keepdims=True)
                self.acc = alpha * self.acc + jnp.dot(vlt_ref[:, chunk_rows(j)], p.astype(BF16),
                                                      preferred_element_type=F32)
            self.m = m_new
            self.stage += 1
            if self.stage == n_chunks + 1:
                o_ref[self.qrows, :] = jnp.transpose(self.acc / self.l).astype(o_ref.dtype)

    streams = [Stream(i) for i in range(n_streams)]
    n_stages = n_chunks + 1
    for t in range(n_stages + ATTN_SKEW * (n_streams - 1)):
        for i, st in enumerate(streams):
            if 0 <= t - i * ATTN_SKEW < n_stages:
                st.advance()


def _attention(st, q, k, vt, latent):
    b, seq, lc = st.batch, st.seq, st.ctx_len
    qk_w = q.shape[-1]
    tq = min(ATTN_TQ, seq if latent else lc)
    assert seq % tq == 0 and seq % ATTN_TK == 0 and lc % min(ATTN_TQ, lc) == 0
    ctx_blk0 = st.n_lat // lc
    if latent:
        nq = seq // tq
        q_map = lambda bi, h, qi: (h, bi * nq + qi, 0)
        o_map = lambda bi, h, qi: (bi * nq + qi, h)
        rows = st.n_lat
        n_chunks = seq // ATTN_TK
    else:
        nq = lc // tq
        q_map = lambda bi, h, qi: (h, st.n_lat // tq + bi * nq + qi, 0)
        o_map = lambda bi, h, qi: (bi * nq + qi, h)
        rows = st.n_ctx
        n_chunks = 0
    kc_map = lambda bi, h, qi: (h, ctx_blk0 + bi, 0)
    kl_map = lambda bi, h, qi: (h, bi, 0)
    vc_map = lambda bi, h, qi: (h, 0, ctx_blk0 + bi)
    vl_map = lambda bi, h, qi: (h, 0, bi)
    in_specs = [
        pl.BlockSpec((None, tq, qk_w), q_map),
        pl.BlockSpec((None, lc, qk_w), kc_map),
        pl.BlockSpec((None, MLA_V, lc), vc_map),
    ]
    args = [q, k, vt]
    if latent:
        in_specs += [pl.BlockSpec((None, seq, qk_w), kl_map), pl.BlockSpec((None, MLA_V, seq), vl_map)]
        args += [k, vt]
    return pl.pallas_call(
        functools.partial(_attn_kernel, n_chunks=n_chunks),
        grid=(b, MLA_HEADS, nq),
        in_specs=in_specs,
        out_specs=pl.BlockSpec((tq, MLA_V), o_map),
        out_shape=jax.ShapeDtypeStruct((rows, MLA_HEADS * MLA_V), BF16),
        compiler_params=_cparams(("parallel", "parallel", "arbitrary")),
        name="mla_attention_latent" if latent else "mla_attention_context",
    )(*args)


def _gla_qkg_kernel(x_ref, sh_ref, sc_ref, w_ref, gb_ref, bias_ref, q_ref, k_ref, gf_ref, gb_out_ref,
                    *, key_dim, q_scale):
    h = (x_ref[...] * (1.0 + sc_ref[...]) + sh_ref[...]).astype(BF16)
    y = jnp.dot(h, w_ref[...], preferred_element_type=F32)
    q_ref[...] = (y[:, :key_dim] * q_scale).astype(BF16)
    k_ref[...] = y[:, key_dim:2 * key_dim].astype(BF16)
    z = jnp.dot(y[:, 2 * key_dim:].astype(BF16), gb_ref[...], preferred_element_type=F32) + bias_ref[...]
    g = (jnp.minimum(z, 0.0) - jnp.log(1.0 + jnp.exp(-jnp.abs(z)))) * (math.log2(math.e) / GLA_TAU)
    gf_ref[...] = g[:, :key_dim]
    gb_out_ref[...] = g[:, key_dim:]


def _gla_vr_kernel(x_ref, sh_ref, sc_ref, w_ref, v_ref, r_ref, *, value_dim):
    h = (x_ref[...] * (1.0 + sc_ref[...]) + sh_ref[...]).astype(BF16)
    y = jnp.dot(h, w_ref[...], preferred_element_type=F32)
    v_ref[...] = y[:, :value_dim].astype(BF16)
    r_ref[...] = y[:, value_dim:].astype(BF16)


def _gla_proj(st, x, mods, w_qkg, gate_b_ext, gate_bias, w_vr):
    n, d = st.n, st.d
    key_dim = (w_qkg.shape[1] - LANES) // 2
    value_dim = w_vr.shape[1] // 2
    dk = key_dim // GLA_HEADS
    q, k, gf, gb = pl.pallas_call(
        functools.partial(_gla_qkg_kernel, key_dim=key_dim, q_scale=dk ** -0.5),
        grid=(st.tiles,),
        in_specs=[_row_spec(d), _mod_spec(st, 0), _mod_spec(st, 1), _const_spec(w_qkg.shape),
                  _const_spec(gate_b_ext.shape), _const_spec(gate_bias.shape)],
        out_specs=[_row_spec(key_dim)] * 4,
        out_shape=[jax.ShapeDtypeStruct((n, key_dim), BF16), jax.ShapeDtypeStruct((n, key_dim), BF16),
                   jax.ShapeDtypeStruct((n, key_dim), F32), jax.ShapeDtypeStruct((n, key_dim), F32)],
        compiler_params=_cparams(("parallel",)),
        name="gla_proj_qkg",
    )(x, mods, mods, w_qkg, gate_b_ext, gate_bias)
    v, r = pl.pallas_call(
        functools.partial(_gla_vr_kernel, value_dim=value_dim),
        grid=(st.tiles,),
        in_specs=[_row_spec(d), _mod_spec(st, 0), _mod_spec(st, 1), _const_spec(w_vr.shape)],
        out_specs=[_row_spec(value_dim)] * 2,
        out_shape=[jax.ShapeDtypeStruct((n, value_dim), BF16), jax.ShapeDtypeStruct((n, value_dim), BF16)],
        compiler_params=_cparams(("parallel",)),
        name="gla_proj_vr",
    )(x, mods, mods, w_vr)
    return q, k, v, r, gf, gb


def _cumsum_rows(x, reverse):
    n = x.shape[0]
    row = lax.broadcasted_iota(jnp.int32, x.shape, 0)
    s = 1
    while s < n:
        if reverse:
            x = x + jnp.where(row < n - s, pltpu.roll(x, n - s, axis=0), 0.0)
        else:
            x = x + jnp.where(row >= s, pltpu.roll(x, s, axis=0), 0.0)
        s *= 2
    return x


def _gla_chunk(q, k, v, g, state_t, reverse):
    c, dk = q.shape
    nsub = c // GLA_SUB
    half = GLA_SUB // 2
    b = _cumsum_rows(g, reverse)
    b_last = jnp.sum(g, axis=0, keepdims=True)
    nt = (((1,), (1,)), ((), ()))
    qe = (q * jnp.exp2(b)).astype(BF16)
    o = lax.dot_general(qe, state_t.astype(BF16), nt, preferred_element_type=F32)

    q_parts, k_parts = [], []
    for src in (range(1, nsub) if reverse else range(nsub - 1)):
        lo = src * GLA_SUB
        if reverse:
            ref = b[lo:lo + 1, :]
            qd = (q[:lo] * jnp.exp2(b[:lo] - ref)).astype(BF16)
            q_parts.append(jnp.concatenate([qd, jnp.zeros((c - lo, dk), BF16)], axis=0))
        else:
            ref = b[lo + GLA_SUB - 1:lo + GLA_SUB, :]
            qd = (q[lo + GLA_SUB:] * jnp.exp2(b[lo + GLA_SUB:] - ref)).astype(BF16)
            q_parts.append(jnp.concatenate([jnp.zeros((lo + GLA_SUB, dk), BF16), qd], axis=0))
        kd = (k[lo:lo + GLA_SUB] * jnp.exp2(ref - b[lo:lo + GLA_SUB])).astype(BF16)
        pieces = [kd]
        if lo:
            pieces.insert(0, jnp.zeros((lo, dk), BF16))
        if c - lo - GLA_SUB:
            pieces.append(jnp.zeros((c - lo - GLA_SUB, dk), BF16))
        k_parts.append(jnp.concatenate(pieces, axis=0))
    off = lax.dot_general(jnp.concatenate(q_parts, axis=1), jnp.concatenate(k_parts, axis=1), nt,
                          preferred_element_type=F32)
    a_mat = jnp.concatenate([off, jnp.zeros((c, LANES - c), F32)], axis=1)

    ones = jnp.ones((dk, LANES), BF16)
    sub_r = lax.broadcasted_iota(jnp.int32, (half, LANES), 0)
    sub_c = lax.broadcasted_iota(jnp.int32, (half, LANES), 1)
    diag_rows = []
    for a in range(nsub):
        lo = a * GLA_SUB
        terms, spans = [], []
        for jj in range(GLA_SUB):
            if reverse:
                r0, r1 = 0, (half if jj < half else GLA_SUB)
            else:
                r0, r1 = (0 if jj < half else half), GLA_SUB
            kj = k[lo + jj:lo + jj + 1, :]
            bj = b[lo + jj:lo + jj + 1, :]
            terms.append(q[lo + r0:lo + r1] * kj * jnp.exp2(b[lo + r0:lo + r1] - bj))
            spans.append((r0, r1))
        sums = jnp.dot(jnp.concatenate(terms, axis=0).astype(BF16), ones, preferred_element_type=F32)
        blk = [jnp.zeros((half, LANES), F32), jnp.zeros((half, LANES), F32)]
        pos = 0
        for jj, (r0, r1) in enumerate(spans):
            for r in range(r0, r1, half):
                piece = sums[pos:pos + half, :]
                pos += half
                rows = sub_r + r
                keep = (sub_c == lo + jj) & ((rows <= jj) if reverse else (rows >= jj))
                blk[r // half] = blk[r // half] + jnp.where(keep, piece, 0.0)
        diag_rows += blk
    a_mat = a_mat + jnp.concatenate(diag_rows, axis=0)
    o = o + jnp.dot(a_mat[:, :c].astype(BF16), v, preferred_element_type=F32)
    kd = (k * jnp.exp2(b_last - b)).astype(BF16)
    tn = (((0,), (0,)), ((), ()))
    new_state = state_t * jnp.exp2(b_last) + lax.dot_general(v, kd, tn, preferred_element_type=F32)
    return o, new_state


def _gla_scan_kernel(qf_ref, kf_ref, vf_ref, gf_ref, qb_ref, kb_ref, vb_ref, gb_ref, of_ref, ob_ref,
                     sf_ref, sb_ref, *, n_chunks):
    @pl.when(pl.program_id(2) == 0)
    def _():
        sf_ref[...] = jnp.zeros_like(sf_ref)
        sb_ref[...] = jnp.zeros_like(sb_ref)

    def step(q_ref, k_ref, v_ref, g_ref, o_ref, state_ref, cidx, reverse):
        rows = pl.ds(cidx * GLA_CHUNK, GLA_CHUNK)
        o, new_state = _gla_chunk(q_ref[rows, :].astype(F32), k_ref[rows, :].astype(F32), v_ref[rows, :],
                                  g_ref[rows, :], state_ref[...], reverse)
        o_ref[rows, :] = o.astype(o_ref.dtype)
        state_ref[...] = new_state

    for cidx in range(n_chunks):
        step(qf_ref, kf_ref, vf_ref, gf_ref, of_ref, sf_ref, cidx, False)
        step(qb_ref, kb_ref, vb_ref, gb_ref, ob_ref, sb_ref, n_chunks - 1 - cidx, True)


def _gla_scan(st, q, k, v, gf, gb):
    b = st.batch
    key_dim, value_dim = q.shape[1], v.shape[1]
    dk, dv = key_dim // GLA_HEADS, value_dim // GLA_HEADS
    t_rows = ROW_TILE
    nc, nl = st.ctx_len // t_rows, st.seq // t_rows

    def row_block(bi, s, reverse):
        if reverse:
            ctx = st.lat_tiles + bi * nc + (nc - 1 - s)
            lat = bi * nl + (nl - 1 - (s - nc))
        else:
            ctx = st.lat_tiles + bi * nc + s
            lat = bi * nl + (s - nc)
        return jnp.where(s < nc, ctx, lat)

    def spec(w, reverse):
        return pl.BlockSpec((t_rows, w), lambda bi, h, s: (row_block(bi, s, reverse), h))

    dir_specs = lambda reverse: [spec(dk, reverse), spec(dk, reverse), spec(dv, reverse), spec(dk, reverse)]
    out = jax.ShapeDtypeStruct((st.n, value_dim), BF16)
    return pl.pallas_call(
        functools.partial(_gla_scan_kernel, n_chunks=t_rows // GLA_CHUNK),
        grid=(b, GLA_HEADS, nc + nl),
        in_specs=dir_specs(False) + dir_specs(True),
        out_specs=[spec(dv, False), spec(dv, True)],
        out_shape=[out, out],
        scratch_shapes=[pltpu.VMEM((dv, dk), F32), pltpu.VMEM((dv, dk), F32)],
        compiler_params=_cparams(("parallel", "parallel", "arbitrary")),
        name="gla_scan_bidir",
    )(q, k, v, gf, q, k, v, gb)


def _route(logits, n_real):
    lane = lax.broadcasted_iota(jnp.int32, logits.shape, 1)
    neg = jnp.float32(-jnp.inf)
    big = jnp.int32(2 ** 30)
    is_grp = lane < MOE_GROUPS
    gl = jnp.where(is_grp, logits, neg)
    gmax = jnp.max(gl, axis=-1, keepdims=True)
    g_top = jnp.min(jnp.where(gl == gmax, lane, big), axis=-1, keepdims=True)
    p_grp = 1.0 / jnp.sum(jnp.exp(gl - gmax), axis=-1, keepdims=True)
    first = MOE_GROUPS + g_top * MOE_PER_GROUP
    in_grp = (lane >= first) & (lane < first + MOE_PER_GROUP)
    el = jnp.where(in_grp, logits, neg)
    emax = jnp.max(el, axis=-1, keepdims=True)
    pe = jnp.exp(el - emax)
    pe = pe / jnp.sum(pe, axis=-1, keepdims=True)
    v1 = jnp.max(pe, axis=-1, keepdims=True)
    i1 = jnp.min(jnp.where(in_grp & (pe == v1), lane, big), axis=-1, keepdims=True)
    rest = jnp.where(in_grp & (lane != i1), pe, -1.0)
    v2 = jnp.max(rest, axis=-1, keepdims=True)
    i2 = jnp.min(jnp.where(rest == v2, lane, big), axis=-1, keepdims=True)
    denom = v1 + v2
    w1 = p_grp * v1 / denom
    w2 = p_grp * v2 / denom
    e1 = (i1 - MOE_GROUPS).astype(F32)
    e2 = (i2 - MOE_GROUPS).astype(F32)
    del n_real
    return jnp.where(lane == 0, e1, jnp.where(lane == 1, e2, jnp.where(lane == 2, w1, jnp.where(lane == 3, w2, 0.0))))


def _pack_bf16_pairs(x):
    half = x.shape[1] // 2
    bits = lax.bitcast_convert_type(x.astype(BF16).astype(F32), jnp.uint32)
    return bits[:, half:] | (bits[:, :half] >> 16)


def _unpack_bf16_pairs(w):
    lo = lax.bitcast_convert_type(w << 16, F32)
    hi = lax.bitcast_convert_type(w & jnp.uint32(0xFFFF0000), F32)
    return jnp.concatenate([lo, hi], axis=1)


MIX_GROUPS = 2


def _row_groups(tm):
    rows = tm // MIX_GROUPS
    return [pl.ds(i * rows, rows) for i in range(MIX_GROUPS)]


def _post_mix(y, rows, x_ref, g1_ref, lng_ref, lnb_ref, sh2_ref, sc2_ref, wr_ref, rb_ref,
              x1_ref, h2_ref, route_ref, alpha):
    x1 = _layer_norm(alpha * x_ref[rows, :] + g1_ref[...] * y, lng_ref[...], lnb_ref[...])
    x1_ref[rows, :] = x1
    h2 = x1 * (1.0 + sc2_ref[...]) + sh2_ref[...]
    h2_ref[rows, :] = _pack_bf16_pairs(h2)
    n = y.shape[0]
    hi = h2.astype(BF16)
    lo = (h2 - hi.astype(F32)).astype(BF16)
    pr = jnp.dot(jnp.concatenate([hi, lo], axis=0), wr_ref[...], preferred_element_type=F32)
    logits = pr[:n, :LANES] + pr[:n, LANES:] + pr[n:, :LANES] + rb_ref[...]
    route_ref[rows, :] = _route(logits, None)


def _mla_out_kernel(ol_ref, oc_ref, w_o_ref, *rest, alpha, lat_tiles):
    is_lat = pl.program_id(0) < lat_tiles
    groups = _row_groups(ol_ref.shape[0])
    ys = []
    for rows in groups:
        o = jnp.where(is_lat, ol_ref[rows, :], oc_ref[rows, :])
        ys.append(jnp.dot(o, w_o_ref[...], preferred_element_type=F32))
    for rows, y in zip(groups, ys):
        _post_mix(y, rows, *rest, alpha=alpha)


def _gla_out_kernel(of_ref, ob_ref, r_ref, ng_ref, w_o_ref, *rest, alpha, dv):
    groups = _row_groups(of_ref.shape[0])
    ng = ng_ref[...]
    ys = []
    for rows in groups:
        o = of_ref[rows, :].astype(F32) + ob_ref[rows, :].astype(F32)
        r = r_ref[rows, :].astype(F32)
        gate = r * _sigmoid(r)
        parts = []
        for hd in range(GLA_HEADS):
            parts.append(_rms_norm(o[:, hd * dv:(hd + 1) * dv], ng) * gate[:, hd * dv:(hd + 1) * dv])
        u = jnp.concatenate(parts, axis=1).astype(BF16)
        ys.append(jnp.dot(u, w_o_ref[...], preferred_element_type=F32))
    for rows, y in zip(groups, ys):
        _post_mix(y, rows, *rest, alpha=alpha)


def _mix_out(st, rows, mixer_inputs, mixer_widths, kernel, x, mods, ln_g, ln_b, w_o, wr, rbias):
    d = st.d
    tm = MIX_TILE
    assert rows % tm == 0 and st.seq % tm == 0
    mixer_specs = [w if isinstance(w, pl.BlockSpec) else _row_spec(w, tm) for w in mixer_widths]
    mixer_specs += [_const_spec(a.shape) for a in mixer_inputs[len(mixer_widths):]]
    common_specs = [
        _row_spec(d, tm), _mod_spec(st, 2, tm), _const_spec((1, d)), _const_spec((1, d)),
        _mod_spec(st, 3, tm), _mod_spec(st, 4, tm), _const_spec(wr.shape), _const_spec((1, LANES)),
    ]
    return pl.pallas_call(
        kernel,
        grid=(rows // tm,),
        in_specs=mixer_specs + [_const_spec(w_o.shape)] + common_specs,
        out_specs=[_row_spec(d, tm), _row_spec(d // 2, tm), _row_spec(LANES, tm)],
        out_shape=[jax.ShapeDtypeStruct((rows, d), F32), jax.ShapeDtypeStruct((rows, d // 2), jnp.uint32),
                   jax.ShapeDtypeStruct((rows, LANES), F32)],
        compiler_params=_cparams(("parallel",)),
        name="mixer_out_ln_route",
    )(*mixer_inputs, w_o, x, mods, ln_g, ln_b, mods, mods, wr, rbias)


def _rank_kernel(route_ref, rank_ref, counts_ref, run_ref):
    t = pl.program_id(0)

    @pl.when(t == 0)
    def _():
        run_ref[...] = jnp.zeros_like(run_ref)

    route = route_ref[...]
    tm = route.shape[0]
    lane = lax.broadcasted_iota(jnp.int32, route.shape, 1)
    e0 = route[:, 0:1].astype(jnp.int32)
    e1 = route[:, 1:2].astype(jnp.int32)
    oh0 = (lane == e0).astype(F32)
    oh1 = (lane == e1).astype(F32)
    both = oh0 + oh1
    ri = lax.broadcasted_iota(jnp.int32, (tm, tm), 0)
    ci = lax.broadcasted_iota(jnp.int32, (tm, tm), 1)
    tri = (ci < ri).astype(BF16)
    before = jnp.dot(tri, both.astype(BF16), preferred_element_type=F32) + run_ref[0:1, :]
    r0 = jnp.sum(oh0 * before, axis=-1, keepdims=True)
    r1 = jnp.sum(oh1 * before, axis=-1, keepdims=True)
    rank_ref[...] = jnp.where(lane == 0, r0, jnp.where(lane == 1, r1, 0.0)).astype(jnp.int32)
    run_ref[...] = run_ref[...] + jnp.sum(both, axis=0, keepdims=True)
    counts_ref[...] = run_ref[...].astype(jnp.int32)


def _moe_ranks(route):
    n = route.shape[0]
    tm = RANK_TILE if n % RANK_TILE == 0 else ROW_TILE
    return pl.pallas_call(
        _rank_kernel,
        grid=(n // tm,),
        in_specs=[_row_spec(LANES, tm)],
        out_specs=[_row_spec(LANES, tm), _const_spec((8, LANES))],
        out_shape=[jax.ShapeDtypeStruct((n, LANES), jnp.int32), jax.ShapeDtypeStruct((8, LANES), jnp.int32)],
        scratch_shapes=[pltpu.VMEM((8, LANES), F32)],
        compiler_params=_cparams(("arbitrary",)),
        name="moe_ranks",
    )(route)


def _dispatch_kernel(slots_ref, pad_start_ref, pad_end_ref, h_ref, xs_ref, zero_ref, sem_ref, zsem_ref):
    t = pl.program_id(0)
    tm = h_ref.shape[0]

    @pl.when(t == 0)
    def _():
        zero_ref[...] = jnp.zeros_like(zero_ref)
        for e in range(MOE_EXPERTS):
            @pl.when(pad_end_ref[e] > pad_start_ref[e])
            def _():
                start = pl.multiple_of(pad_end_ref[e] - MOE_BLOCK, MOE_BLOCK)
                pltpu.make_async_copy(zero_ref, xs_ref.at[pl.ds(start, MOE_BLOCK)], zsem_ref).start()
        n_used = pad_end_ref[MOE_EXPERTS - 1] // MOE_BLOCK
        n_blocks = xs_ref.shape[0] // MOE_BLOCK

        def fill_tail(i, carry):
            start = pl.multiple_of(i * MOE_BLOCK, MOE_BLOCK)
            pltpu.make_async_copy(zero_ref, xs_ref.at[pl.ds(start, MOE_BLOCK)], zsem_ref).start()
            return carry

        def wait_tail(i, carry):
            pltpu.make_async_copy(zero_ref, xs_ref.at[pl.ds(0, MOE_BLOCK)], zsem_ref).wait()
            return carry

        lax.fori_loop(n_used, n_blocks, fill_tail, 0)
        for e in range(MOE_EXPERTS):
            @pl.when(pad_end_ref[e] > pad_start_ref[e])
            def _():
                pltpu.make_async_copy(zero_ref, xs_ref.at[pl.ds(0, MOE_BLOCK)], zsem_ref).wait()
        lax.fori_loop(n_used, n_blocks, wait_tail, 0)

    base = t * (2 * tm)

    def issue(r, carry):
        s0 = slots_ref[base + 2 * r]
        s1 = slots_ref[base + 2 * r + 1]
        pltpu.make_async_copy(h_ref.at[pl.ds(r, 1)], xs_ref.at[pl.ds(s0, 1)], sem_ref.at[0]).start(priority=0)
        pltpu.make_async_copy(h_ref.at[pl.ds(r, 1)], xs_ref.at[pl.ds(s1, 1)], sem_ref.at[1]).start(priority=1)
        return carry

    lax.fori_loop(0, tm, issue, 0, unroll=8)
    pltpu.make_async_copy(h_ref, xs_ref.at[pl.ds(0, tm)], sem_ref.at[0]).wait()
    pltpu.make_async_copy(h_ref, xs_ref.at[pl.ds(0, tm)], sem_ref.at[1]).wait()


def _moe_dispatch(h2, slots, pad_start, pad_end, n_rows):
    n, d = h2.shape
    tm = ROW_TILE
    grid_spec = pltpu.PrefetchScalarGridSpec(
        num_scalar_prefetch=3,
        grid=(n // tm,),
        in_specs=[pl.BlockSpec((tm, d), lambda t, *_: (t, 0))],
        out_specs=pl.BlockSpec(memory_space=pl.ANY),
        scratch_shapes=[pltpu.VMEM((MOE_BLOCK, d), h2.dtype), pltpu.SemaphoreType.DMA((2,)),
                        pltpu.SemaphoreType.DMA(())],
    )
    return pl.pallas_call(
        _dispatch_kernel,
        grid_spec=grid_spec,
        out_shape=jax.ShapeDtypeStruct((n_rows, d), h2.dtype),
        compiler_params=_cparams(("arbitrary",)),
        name="moe_dispatch",
    )(slots, pad_start, pad_end, h2)


def _experts_kernel(block_exp_ref, n_used_ref, xs_ref, w1_ref, w3_ref, w2_ref, o_ref, w1b_ref, w3b_ref, w2b_ref):
    i = pl.program_id(0)

    @pl.when(i < n_used_ref[0])
    def _():
        prev = block_exp_ref[jnp.maximum(i - 1, 0)]

        @pl.when((i == 0) | (block_exp_ref[i] != prev))
        def _():
            w1b_ref[...] = w1_ref[...].astype(BF16)
            w3b_ref[...] = w3_ref[...].astype(BF16)
            w2b_ref[...] = w2_ref[...].astype(BF16)

        groups = _row_groups(xs_ref.shape[0])
        ups = []
        for rows in groups:
            xb = _unpack_bf16_pairs(xs_ref[rows, :]).astype(BF16)
            ups.append((jnp.dot(xb, w1b_ref[...], preferred_element_type=F32),
                        jnp.dot(xb, w3b_ref[...], preferred_element_type=F32)))
        for rows, (a, g) in zip(groups, ups):
            hid = (a * _sigmoid(a) * g).astype(BF16)
            o_ref[rows, :] = _pack_bf16_pairs(jnp.dot(hid, w2b_ref[...], preferred_element_type=F32))

    @pl.when(i >= n_used_ref[0])
    def _():
        o_ref[...] = jnp.zeros_like(o_ref)


def _moe_experts(xs, block_exp, n_used, w1, w3, w2, layer):
    n_rows, dp = xs.shape
    n_blocks = n_rows // MOE_BLOCK
    d, f = w1.shape[-2], w1.shape[-1]
    assert dp * 2 == d

    def blk(i, be, nu):
        return jnp.minimum(i, nu[0] - 1)

    grid_spec = pltpu.PrefetchScalarGridSpec(
        num_scalar_prefetch=2,
        grid=(n_blocks,),
        in_specs=[
            pl.BlockSpec((MOE_BLOCK, dp), lambda i, be, nu: (blk(i, be, nu), 0)),
            pl.BlockSpec((None, None, d, f), lambda i, be, nu: (layer, be[blk(i, be, nu)], 0, 0)),
            pl.BlockSpec((None, None, d, f), lambda i, be, nu: (layer, be[blk(i, be, nu)], 0, 0)),
            pl.BlockSpec((None, None, f, d), lambda i, be, nu: (layer, be[blk(i, be, nu)], 0, 0)),
        ],
        out_specs=pl.BlockSpec((MOE_BLOCK, dp), lambda i, be, nu: (i, 0)),
        scratch_shapes=[pltpu.VMEM((d, f), BF16), pltpu.VMEM((d, f), BF16), pltpu.VMEM((f, d), BF16)],
    )
    return pl.pallas_call(
        _experts_kernel,
        grid_spec=grid_spec,
        out_shape=jax.ShapeDtypeStruct((n_rows, dp), xs.dtype),
        compiler_params=_cparams(("arbitrary",)),
        name="moe_experts",
    )(block_exp, n_used, xs, w1, w3, w2)


def _combine_kernel(slots_ref, ys_ref, x1_ref, route_ref, g2_ref, lng_ref, lnb_ref, out_ref, buf_ref, sem_ref,
                    *, alpha):
    t = pl.program_id(0)
    n_tiles = pl.num_programs(0)
    tm = x1_ref.shape[0]

    def gather_tile(tile, slot):
        base = tile * (2 * tm)

        def issue(r, carry):
            for k in range(2):
                src = ys_ref.at[pl.ds(slots_ref[base + 2 * r + k], 1)]
                pltpu.make_async_copy(src, buf_ref.at[slot, k, pl.ds(r, 1)], sem_ref.at[slot, k]).start(priority=k)
            return carry

        lax.fori_loop(0, tm, issue, 0, unroll=8)

    @pl.when(t == 0)
    def _():
        gather_tile(0, 0)

    @pl.when(t + 1 < n_tiles)
    def _():
        gather_tile(t + 1, (t + 1) % 2)

    slot = t % 2
    for k in range(2):
        pltpu.make_async_copy(ys_ref.at[pl.ds(0, tm)], buf_ref.at[slot, k], sem_ref.at[slot, k]).wait()
    route = route_ref[...]
    y = (route[:, 2:3] * _unpack_bf16_pairs(buf_ref[slot, 0])
         + route[:, 3:4] * _unpack_bf16_pairs(buf_ref[slot, 1]))
    out_ref[...] = _layer_norm(alpha * x1_ref[...] + g2_ref[...] * y, lng_ref[...], lnb_ref[...])


def _moe_combine(st, ys, slots, x1, route, mods, ln_g, ln_b, alpha):
    n, d = x1.shape
    tm = ROW_TILE
    grid_spec = pltpu.PrefetchScalarGridSpec(
        num_scalar_prefetch=1,
        grid=(n // tm,),
        in_specs=[
            pl.BlockSpec(memory_space=pl.ANY),
            pl.BlockSpec((tm, d), lambda t, s: (t, 0)),
            pl.BlockSpec((tm, LANES), lambda t, s: (t, 0)),
            pl.BlockSpec((None, 1, d), lambda t, s: (st.mod_row(t), 0, 5)),
            pl.BlockSpec((1, d), lambda t, s: (0, 0)),
            pl.BlockSpec((1, d), lambda t, s: (0, 0)),
        ],
        out_specs=pl.BlockSpec((tm, d), lambda t, s: (t, 0)),
        scratch_shapes=[pltpu.VMEM((2, 2, tm, ys.shape[1]), ys.dtype), pltpu.SemaphoreType.DMA((2, 2))],
    )
    return pl.pallas_call(
        functools.partial(_combine_kernel, alpha=alpha),
        grid_spec=grid_spec,
        out_shape=jax.ShapeDtypeStruct((n, d), F32),
        compiler_params=_cparams(("arbitrary",)),
        name="moe_combine_ln",
    )(slots, ys, x1, route, mods, ln_g, ln_b)


def _hier_moe(st, h2, route, x1, mods, ln_g, ln_b, w1, w3, w2, layer, alpha):
    n = h2.shape[0]
    ranks, counts = _moe_ranks(route)
    counts = counts[0, :MOE_EXPERTS]
    padded = (counts + MOE_BLOCK - 1) // MOE_BLOCK * MOE_BLOCK
    pad_end = jnp.cumsum(padded).astype(jnp.int32)
    pad_start = pad_end - padded
    expert = route[:, :2].astype(jnp.int32)
    eids = jnp.arange(MOE_EXPERTS, dtype=jnp.int32)
    start_of = jnp.sum(jnp.where(expert[:, :, None] == eids, pad_start, 0), axis=-1)
    slots = (start_of + ranks[:, :2]).reshape(2 * n)
    n_blocks = -(-(2 * n) // MOE_BLOCK) + MOE_EXPERTS
    block_pos = jnp.arange(n_blocks, dtype=jnp.int32) * MOE_BLOCK
    block_exp = jnp.minimum(jnp.sum((pad_end[None, :] <= block_pos[:, None]).astype(jnp.int32), axis=1),
                            MOE_EXPERTS - 1)
    n_used = (pad_end[-1:] // MOE_BLOCK).astype(jnp.int32)
    xs = _moe_dispatch(h2, slots, pad_start, pad_end, n_blocks * MOE_BLOCK)
    ys = _moe_experts(xs, block_exp, n_used, w1, w3, w2, layer)
    return _moe_combine(st, ys, slots, x1, route, mods, ln_g, ln_b, alpha)


def _swap_halves(w):
    half = w.shape[-1] // 2
    return jnp.concatenate([w[..., half:], w[..., :half]], axis=-1)


def _mla_weights(w_in, w_uq, w_ukv, w_o):
    qr, kvr = MLA_Q_RANK, MLA_KV_RANK
    w_pe = w_in[:, qr + kvr:]
    w_in_ext = jnp.concatenate([w_in, _swap_halves(w_pe)], axis=1).astype(BF16)
    uq = w_uq.reshape(qr, MLA_HEADS, MLA_NOPE + MLA_ROPE)
    q_nope = uq[:, :, :MLA_NOPE].reshape(qr, MLA_HEADS * MLA_NOPE)
    q_pe = uq[:, :, MLA_NOPE:]
    q_pe2 = jnp.concatenate([q_pe, _swap_halves(q_pe)], axis=-1).reshape(qr, MLA_HEADS * 2 * MLA_ROPE)
    w_q = jnp.concatenate([q_nope, q_pe2], axis=1).astype(BF16)
    ukv = w_ukv.reshape(kvr, MLA_HEADS, MLA_NOPE + MLA_V)
    w_kv = jnp.concatenate([ukv[:, :, :MLA_NOPE].reshape(kvr, -1), ukv[:, :, MLA_NOPE:].reshape(kvr, -1)],
                           axis=1).astype(BF16)
    return w_in_ext, w_q, w_kv, w_o.astype(BF16)


def _gla_weights(w_in, gate_a, gate_b, gate_bias, w_o):
    d = w_in.shape[0]
    key_dim = gate_b.shape[-1]
    rank = gate_a.shape[-1]
    ga = jnp.concatenate([gate_a[0], gate_a[1], jnp.zeros((d, LANES - 2 * rank), F32)], axis=1)
    w_qkg = jnp.concatenate([w_in[:, :2 * key_dim], ga], axis=1).astype(BF16)
    w_vr = w_in[:, 2 * key_dim:].astype(BF16)
    gb = jnp.zeros((LANES, 2 * key_dim), F32)
    gb = gb.at[:rank, :key_dim].set(gate_b[0]).at[rank:2 * rank, key_dim:].set(gate_b[1])
    bias = jnp.concatenate([gate_bias[0], gate_bias[1]])[None, :]
    return w_qkg, gb.astype(BF16), bias, w_vr, w_o.astype(BF16)


def _rope_table(seq):
    n_rows = seq // GRID_W
    row = jnp.repeat(jnp.arange(n_rows, dtype=F32), GRID_W)
    col = jnp.tile(jnp.arange(GRID_W, dtype=F32), n_rows)
    n_freq = MLA_ROPE // 4
    inv_freq = jnp.power(ROPE_THETA, -jnp.arange(n_freq, dtype=F32) / n_freq)
    ang = jnp.concatenate([row[:, None] * inv_freq, col[:, None] * inv_freq], axis=-1)
    cos, sin = jnp.cos(ang), jnp.sin(ang)
    lat = jnp.concatenate([cos, cos, -sin, sin], axis=-1)
    ident = jnp.concatenate([jnp.ones((ROW_TILE, MLA_ROPE), F32), jnp.zeros((ROW_TILE, MLA_ROPE), F32)], axis=-1)
    return jnp.concatenate([lat, ident], axis=0)


def kernel(x, c, ctx, c_ctx, w_mod, b_mod, ln1_g, ln1_b, ln2_g, ln2_b, mla_w_in, mla_q_norm, mla_w_uq, mla_kv_norm, mla_w_ukv, mla_w_o, gla_w_in, gla_gate_a, gla_gate_b, gla_gate_bias, gla_norm, gla_w_o, moe_w_grp, moe_b_grp, moe_w_exp, moe_b_exp, moe_w1, moe_w3, moe_w2):
    batch, seq, d = x.shape
    ctx_len = ctx.shape[1]
    depth = w_mod.shape[0]
    assert batch + 1 <= 8
    alpha = (2.0 * depth) ** 0.25
    full = _Stream(batch, seq, ctx_len, d)

    cond = jnp.concatenate([c, c_ctx[None, :], jnp.zeros((8 - batch - 1, d), F32)], axis=0)
    mods_all = _adaln_tables(cond, w_mod, b_mod)
    cs_tab = _rope_table(seq)
    xs = jnp.concatenate([x.reshape(batch * seq, d), ctx.reshape(batch * ctx_len, d)], axis=0)

    for i in range(depth):
        last = i == depth - 1
        mods = mods_all[i].reshape(8, 1, N_MOD * d)
        j = i // 2
        n_rows = full.n_lat if last else full.n
        wr = jnp.concatenate([moe_w_grp[i], moe_w_exp[i],
                              jnp.zeros((d, LANES - MOE_GROUPS - MOE_EXPERTS), F32)], axis=1)
        wr_hi = wr.astype(BF16)
        wr_lo = (wr - wr_hi.astype(F32)).astype(BF16)
        wr_split = jnp.concatenate([wr_hi, wr_lo], axis=1)
        rbias = jnp.concatenate([moe_b_grp[i], moe_b_exp[i],
                                 jnp.zeros((LANES - MOE_GROUPS - MOE_EXPERTS,), F32)])[None, :]
        ln1 = (ln1_g[i][None, :], ln1_b[i][None, :])
        if i % 2 == 0:
            w_in_ext, w_q, w_kv, w_o = _mla_weights(mla_w_in[j], mla_w_uq[j], mla_w_ukv[j], mla_w_o[j])
            q, k, v = _mla_proj(full, xs, mods, cs_tab, w_in_ext, mla_q_norm[j][None, :],
                                mla_kv_norm[j][None, :], w_q, w_kv)
            o_lat = _attention(full, q, k, v, latent=True)
            o_ctx = o_lat if last else _attention(full, q, k, v, latent=False)
            lat_tiles = full.n_lat // MIX_TILE
            ow = o_lat.shape[1]
            o_specs = [pl.BlockSpec((MIX_TILE, ow), lambda t: (jnp.minimum(t, lat_tiles - 1), 0)),
                       pl.BlockSpec((MIX_TILE, ow), lambda t: (jnp.maximum(t - lat_tiles, 0), 0))]
            x1, h2, route = _mix_out(full, n_rows, [o_lat, o_ctx], o_specs,
                                     functools.partial(_mla_out_kernel, alpha=alpha, lat_tiles=lat_tiles),
                                     xs, mods, ln1[0], ln1[1], w_o, wr_split, rbias)
        else:
            w_qkg, gb_ext, gbias, w_vr, w_o = _gla_weights(gla_w_in[j], gla_gate_a[j], gla_gate_b[j],
                                                          gla_gate_bias[j], gla_w_o[j])
            q, k, v, r, gf, gb = _gla_proj(full, xs, mods, w_qkg, gb_ext, gbias, w_vr)
            o_f, o_b = _gla_scan(full, q, k, v, gf, gb)
            vd = v.shape[1]
            x1, h2, route = _mix_out(full, n_rows, [o_f, o_b, r, gla_norm[j][None, :]], [vd, vd, vd],
                                     functools.partial(_gla_out_kernel, alpha=alpha, dv=vd // GLA_HEADS),
                                     xs, mods, ln1[0], ln1[1], w_o, wr_split, rbias)
        xs = _hier_moe(full, h2, route, x1, mods, ln2_g[i][None, :], ln2_b[i][None, :],
                       moe_w1, moe_w3, moe_w2, i, alpha)
    return xs[:batch * seq].reshape(batch, seq, d)
```
